```python
import math
import jax
import jax.numpy as jnp
from jax import lax
import numpy as np

D_MODEL = 1024
BATCH = 1
SEQ = 16384
DEPTH = 2

CTX_LEN = 256
GRID_W = 64
BRANCH_W = 256
N_BRANCH = 4
CONF_K = 31
SCONV_K = 3
DIFF_HEADS = 4
DIFF_DK = 32
DIFF_DV = 2 * DIFF_DK
RET_HEADS = 4
RET_DK = 32
RET_DV = 2 * RET_DK
RET_CHUNK = 128
Q_BLOCK = 128
ROPE_BASE = 10000.0
N_EXPERTS = 16
N_GROUPS = 4
EXPERTS_PER_GROUP = N_EXPERTS // N_GROUPS
TOP_K = 2
D_FF_EXPERT = 512
MOE_BLOCK = 256
EPS = 1e-6
IN_SPLITS = (2 * BRANCH_W,
             3 * BRANCH_W,
             DIFF_HEADS * 2 * DIFF_DK,
             DIFF_HEADS * 2 * DIFF_DK,
             DIFF_HEADS * DIFF_DV,
             RET_HEADS * RET_DK,
             RET_HEADS * RET_DK,
             RET_HEADS * RET_DV,
             BRANCH_W)
IN_COLS = sum(IN_SPLITS)

kernel_name = 'hybrid_gated_branch_dit_moe'


def rms_norm(x, g):
    xf = x.astype(jnp.float32)
    y = xf * lax.rsqrt(jnp.mean(xf * xf, axis=-1, keepdims=True) + EPS)
    return (y * g.astype(jnp.float32)).astype(x.dtype)


def layer_norm(x, g=None, b=None):
    xf = x.astype(jnp.float32)
    mu = jnp.mean(xf, axis=-1, keepdims=True)
    var = jnp.mean(jnp.square(xf - mu), axis=-1, keepdims=True)
    y = (xf - mu) * lax.rsqrt(var + EPS)
    if g is not None:
        y = y * g.astype(jnp.float32) + b.astype(jnp.float32)
    return y.astype(x.dtype)


def adaln(cv, w, b):
    return jnp.split(jax.nn.silu(cv) @ w + b, 6, axis=-1)


def modulate(h, shift, scale):
    return h * (1.0 + scale) + shift


def split_cols(z):
    points = np.cumsum(np.array(IN_SPLITS))[:-1].tolist()
    return jnp.split(z, points, axis=-1)


def axial_rope(length, dim):
    rows = length // GRID_W
    row = jnp.repeat(jnp.arange(rows, dtype=jnp.float32), GRID_W)
    col = jnp.tile(jnp.arange(GRID_W, dtype=jnp.float32), rows)
    nf = dim // 4
    inv = ROPE_BASE ** (-jnp.arange(nf, dtype=jnp.float32) / nf)
    ang = jnp.concatenate([row[:, None] * inv, col[:, None] * inv], axis=-1)
    return jnp.cos(ang), jnp.sin(ang)


def apply_rope(x, cos, sin):
    x1, x2 = jnp.split(x.astype(jnp.float32), 2, axis=-1)
    c = cos[None, :, None, :]
    s = sin[None, :, None, :]
    return jnp.concatenate([x1 * c - x2 * s, x1 * s + x2 * c], axis=-1).astype(x.dtype)


def depthwise_conv(x, w):
    k = w.shape[0]
    pad = (k - 1) // 2
    return lax.conv_general_dilated(x, w[:, None, :].astype(x.dtype), window_strides=(1,),
                                    padding=[(pad, pad)], dimension_numbers=('NWC', 'WIO', 'NWC'),
                                    feature_group_count=x.shape[-1])


def conformer_conv(z, w_dw, b_dw, g_n, b_n):
    a, gt = jnp.split(z, 2, axis=-1)
    u = a * jax.nn.sigmoid(gt)
    u = depthwise_conv(u, w_dw) + b_dw
    return jax.nn.silu(layer_norm(u, g_n, b_n))


def short_conv(z, w_dw):
    bg, cg, xv = jnp.split(z, 3, axis=-1)
    return bg * depthwise_conv(cg * xv, w_dw)


def split_maps(z, rope):
    b, l, _ = z.shape
    z = z.reshape(b, l, DIFF_HEADS, 2, DIFF_DK)
    z1, z2 = z[..., 0, :], z[..., 1, :]
    if rope is not None:
        z1 = apply_rope(z1, *rope)
        z2 = apply_rope(z2, *rope)
    return z1, z2


def two_map_attention(q1, q2, k1, k2, v, lam):
    scale = DIFF_DK ** -0.5
    s1 = jnp.einsum('bqhd,bkhd->bhqk', q1, k1).astype(jnp.float32) * scale
    s2 = jnp.einsum('bqhd,bkhd->bhqk', q2, k2).astype(jnp.float32) * scale
    a = jax.nn.softmax(s1, axis=-1) - lam * jax.nn.softmax(s2, axis=-1)
    return jnp.einsum('bhqk,bkhe->bqhe', a.astype(v.dtype), v)


def diff_attention_latent(q1, q2, k1, k2, v, lam):
    b, l, h, dk = q1.shape
    nb = l // Q_BLOCK
    qb1 = jnp.moveaxis(q1.reshape(b, nb, Q_BLOCK, h, dk), 1, 0)
    qb2 = jnp.moveaxis(q2.reshape(b, nb, Q_BLOCK, h, dk), 1, 0)
    o = lax.map(lambda qs: two_map_attention(qs[0], qs[1], k1, k2, v, lam), (qb1, qb2))
    return jnp.moveaxis(o, 0, 1).reshape(b, l, h, DIFF_DV)


def diff_lambda(lq1, lk1, lq2, lk2, lam_init):
    f = jnp.float32
    return (jnp.exp(jnp.sum(lq1.astype(f) * lk1.astype(f))) -
            jnp.exp(jnp.sum(lq2.astype(f) * lk2.astype(f))) + lam_init)


def diff_out(o, g, lam_init):
    b, l = o.shape[:2]
    return (rms_norm(o, g) * (1.0 - lam_init)).reshape(b, l, DIFF_HEADS * DIFF_DV)


def retention_chunked(q, k, v, log_gamma, s0):
    f = jnp.float32
    b, l, h, dk = q.shape
    dv = v.shape[-1]
    n = l // RET_CHUNK
    qc = q.reshape(b, n, RET_CHUNK, h, dk)
    kc = k.reshape(b, n, RET_CHUNK, h, dk)
    vc = v.reshape(b, n, RET_CHUNK, h, dv)
    lg = log_gamma.astype(f)
    pos = jnp.arange(RET_CHUNK, dtype=f)
    diff = pos[:, None] - pos[None, :]
    dmask = jnp.exp(jnp.where(diff[None] >= 0, diff[None] * lg[:, None, None], -jnp.inf))
    scores = jnp.einsum('bnihd,bnjhd->bnhij', qc, kc).astype(f) * dmask
    o_intra = jnp.einsum('bnhij,bnjhe->bnihe', scores, vc.astype(f))
    q_dec = jnp.exp((pos + 1.0)[:, None] * lg[None, :])
    k_dec = jnp.exp((RET_CHUNK - 1.0 - pos)[:, None] * lg[None, :])
    kv = jnp.einsum('bnjhd,bnjhe->bnhde', kc.astype(f) * k_dec[:, :, None], vc.astype(f))
    chunk_dec = jnp.exp(RET_CHUNK * lg)[None, :, None, None]

    def step(s, kv_n):
        return chunk_dec * s + kv_n, s

    s_final, s_start = lax.scan(step, s0.astype(f), jnp.moveaxis(kv, 1, 0))
    o_cross = jnp.einsum('bnihd,nbhde->bnihe', qc.astype(f) * q_dec[:, :, None], s_start)
    o = (o_intra + o_cross).reshape(b, l, h, dv)
    return o.astype(v.dtype), s_final


def bidir_retention(q, k, v, ld_f, ld_b, s0_f, s0_b):
    o_f, s_f = retention_chunked(q, k, v, ld_f, s0_f)
    o_b, s_b = retention_chunked(jnp.flip(q, 1), jnp.flip(k, 1), jnp.flip(v, 1), ld_b, s0_b)
    return o_f + jnp.flip(o_b, 1), s_f, s_b


def ret_qkv(z, rope):
    b, l, _ = z[5].shape
    q = z[5].reshape(b, l, RET_HEADS, RET_DK)
    k = z[6].reshape(b, l, RET_HEADS, RET_DK) * (RET_DK ** -0.5)
    v = z[7].reshape(b, l, RET_HEADS, RET_DV)
    if rope is not None:
        q = apply_rope(q, *rope)
        k = apply_rope(k, *rope)
    return q, k, v


def retention_out(o, g):
    b, l = o.shape[:2]
    return jax.nn.silu(g) * layer_norm(o).reshape(b, l, RET_HEADS * RET_DV)


def merge_branches(h, ys, w_gate, b_gate, w_branch, w_o):
    gates = jax.nn.sigmoid((h @ w_gate + b_gate).astype(jnp.float32)).astype(h.dtype)
    gates = gates.reshape(h.shape[:-1] + (N_BRANCH, D_MODEL))
    m = gates[..., 0, :] * (ys[0] @ w_branch[0])
    for i in range(1, N_BRANCH):
        m = m + gates[..., i, :] * (ys[i] @ w_branch[i])
    return m @ w_o


def token_mixers(hl, hc, p, lam_init, last):
    b, l, _ = hl.shape
    zl = split_cols(hl @ p['w_in'])
    zc = split_cols(hc @ p['w_in'])
    rope = axial_rope(l, DIFF_DK)
    lam = diff_lambda(p['lam_q1'], p['lam_k1'], p['lam_q2'], p['lam_k2'], lam_init)
    kc1, kc2 = split_maps(zc[3], None)
    vc = zc[4].reshape(b, -1, DIFF_HEADS, DIFF_DV)
    q1, q2 = split_maps(zl[2], rope)
    k1, k2 = split_maps(zl[3], rope)
    vl = zl[4].reshape(b, l, DIFF_HEADS, DIFF_DV)
    o_attn = diff_attention_latent(q1, q2, jnp.concatenate([kc1, k1], 1), jnp.concatenate([kc2, k2], 1),
                                   jnp.concatenate([vc, vl], 1), lam)
    qrc, krc, vrc = ret_qkv(zc, None)
    qrl, krl, vrl = ret_qkv(zl, rope)
    s0 = jnp.zeros((b, RET_HEADS, RET_DK, RET_DV), jnp.float32)
    orc, s_f, s_b = bidir_retention(qrc, krc, vrc, p['ret_ld_f'], p['ret_ld_b'], s0, s0)
    orl, _, _ = bidir_retention(qrl, krl, vrl, p['ret_ld_f'], p['ret_ld_b'], s_f, s_b)
    ys_l = [conformer_conv(zl[0], p['conv_a_w'], p['conv_a_b'], p['conv_a_g'], p['conv_a_beta']),
            short_conv(zl[1], p['conv_b_w']),
            diff_out(o_attn, p['diff_g'], lam_init),
            retention_out(orl, zl[8])]
    yl = merge_branches(hl, ys_l, p['w_gate'], p['b_gate'], p['w_branch'], p['w_o'])
    if last:
        return yl, None
    qc1, qc2 = split_maps(zc[2], None)
    ys_c = [conformer_conv(zc[0], p['conv_a_w'], p['conv_a_b'], p['conv_a_g'], p['conv_a_beta']),
            short_conv(zc[1], p['conv_b_w']),
            diff_out(two_map_attention(qc1, qc2, kc1, kc2, vc, lam), p['diff_g'], lam_init),
            retention_out(orc, zc[8])]
    yc = merge_branches(hc, ys_c, p['w_gate'], p['b_gate'], p['w_branch'], p['w_o'])
    return yl, yc


def grouped_moe(h, w_router, b_router, w1, w3, w2):
    f = jnp.float32
    t = h.shape[0]
    s = jax.nn.sigmoid((h @ w_router).astype(f))
    sb = s + b_router.astype(f)
    gscore = lax.top_k(sb.reshape(t, N_GROUPS, EXPERTS_PER_GROUP), 2)[0].sum(-1)
    gsel = jnp.argmax(gscore, axis=-1)
    egroup = jnp.arange(N_EXPERTS) // EXPERTS_PER_GROUP
    sb_m = jnp.where(egroup[None, :] == gsel[:, None], sb, -jnp.inf)
    _, eidx = lax.top_k(sb_m, TOP_K)
    wsel = jnp.take_along_axis(s, eidx, axis=-1)
    wsel = wsel / jnp.sum(wsel, axis=-1, keepdims=True)
    a = t * TOP_K
    e_flat = eidx.reshape(a)
    w_flat = wsel.reshape(a)
    tok_flat = jnp.repeat(jnp.arange(t, dtype=jnp.int32), TOP_K)
    order = jnp.argsort(e_flat)
    e_sorted = e_flat[order]
    counts = jnp.bincount(e_flat, length=N_EXPERTS)
    start = jnp.cumsum(counts) - counts
    padded = (counts + MOE_BLOCK - 1) // MOE_BLOCK * MOE_BLOCK
    pend = jnp.cumsum(padded)
    pstart = pend - padded
    pos = pstart[e_sorted] + jnp.arange(a) - start[e_sorted]
    cap = (a + N_EXPERTS * MOE_BLOCK + MOE_BLOCK - 1) // MOE_BLOCK * MOE_BLOCK
    nb = cap // MOE_BLOCK
    buf_tok = jnp.zeros((cap,), jnp.int32).at[pos].set(tok_flat[order])
    buf_w = jnp.zeros((cap,), f).at[pos].set(w_flat[order])
    blk_e = jnp.minimum(jnp.searchsorted(pend, jnp.arange(nb) * MOE_BLOCK, side='right'), N_EXPERTS - 1)

    def expert_block(args):
        idx, wt, e = args
        xb = h[idx]
        u = jax.nn.silu(xb @ w1[e]) * (xb @ w3[e])
        return (u @ w2[e]) * wt[:, None].astype(h.dtype)

    y = lax.map(expert_block, (buf_tok.reshape(nb, MOE_BLOCK), buf_w.reshape(nb, MOE_BLOCK), blk_e))
    return jnp.zeros_like(h).at[buf_tok].add(y.reshape(cap, D_MODEL))


def setup_inputs(seed: int = 0) -> dict:
    key = jax.random.key(seed)
    ks = jax.random.split(key, 40)
    D = D_MODEL

    def nrm(k, shape, scale):
        return jax.random.normal(k, shape, jnp.float32) * scale

    base_ld = jnp.log(1.0 - 2.0 ** (-5.0 - jnp.arange(RET_HEADS, dtype=jnp.float32)))
    return {
        'x': nrm(ks[0], (BATCH, SEQ, D), 1.0),
        'c': nrm(ks[1], (BATCH, D), 1.0),
        'ctx': nrm(ks[2], (BATCH, CTX_LEN, D), 1.0),
        'c_ctx': nrm(ks[3], (D,), 1.0),
        'w_mod': nrm(ks[4], (DEPTH, D, 6 * D), 0.3 * D ** -0.5),
        'b_mod': nrm(ks[5], (DEPTH, 6 * D), 0.02),
        'g_norm1': 1.0 + nrm(ks[6], (DEPTH, D), 0.02),
        'g_norm2': 1.0 + nrm(ks[7], (DEPTH, D), 0.02),
        'w_in': nrm(ks[8], (DEPTH, D, IN_COLS), D ** -0.5),
        'conv_a_w': nrm(ks[9], (DEPTH, CONF_K, BRANCH_W), CONF_K ** -0.5),
        'conv_a_b': nrm(ks[10], (DEPTH, BRANCH_W), 0.02),
        'conv_a_g': 1.0 + nrm(ks[11], (DEPTH, BRANCH_W), 0.02),
        'conv_a_beta': nrm(ks[12], (DEPTH, BRANCH_W), 0.02),
        'conv_b_w': nrm(ks[13], (DEPTH, SCONV_K, BRANCH_W), SCONV_K ** -0.5),
        'lam_q1': nrm(ks[14], (DEPTH, DIFF_DK), 0.1),
        'lam_k1': nrm(ks[15], (DEPTH, DIFF_DK), 0.1),
        'lam_q2': nrm(ks[16], (DEPTH, DIFF_DK), 0.1),
        'lam_k2': nrm(ks[17], (DEPTH, DIFF_DK), 0.1),
        'diff_g': 1.0 + nrm(ks[18], (DEPTH, DIFF_DV), 0.02),
        'ret_ld_f': base_ld[None, :] * (1.0 + nrm(ks[19], (DEPTH, RET_HEADS), 0.1)),
        'ret_ld_b': base_ld[None, :] * (1.0 + nrm(ks[20], (DEPTH, RET_HEADS), 0.1)),
        'w_gate': nrm(ks[21], (DEPTH, D, N_BRANCH * D), D ** -0.5),
        'b_gate': nrm(ks[22], (DEPTH, N_BRANCH * D), 0.02),
        'w_branch': nrm(ks[23], (DEPTH, N_BRANCH, BRANCH_W, D), BRANCH_W ** -0.5),
        'w_o': nrm(ks[24], (DEPTH, D, D), D ** -0.5),
        'w_router': nrm(ks[25], (D, N_EXPERTS), D ** -0.5),
        'b_router': nrm(ks[26], (N_EXPERTS,), 0.01),
        'w1_e': nrm(ks[27], (DEPTH, N_EXPERTS, D, D_FF_EXPERT), D ** -0.5),
        'w3_e': nrm(ks[28], (DEPTH, N_EXPERTS, D, D_FF_EXPERT), D ** -0.5),
        'w2_e': nrm(ks[29], (DEPTH, N_EXPERTS, D_FF_EXPERT, D), D_FF_EXPERT ** -0.5),
        'g_final': 1.0 + nrm(ks[30], (D,), 0.02),
    }


def reference(x, c, ctx, c_ctx, w_mod, b_mod, g_norm1, g_norm2, w_in, conv_a_w, conv_a_b, conv_a_g,
              conv_a_beta, conv_b_w, lam_q1, lam_k1, lam_q2, lam_k2, diff_g, ret_ld_f, ret_ld_b,
              w_gate, b_gate, w_branch, w_o, w_router, b_router, w1_e, w3_e, w2_e, g_final):
    xl, xc = x, ctx
    for l in range(DEPTH):
        last = l == DEPTH - 1
        lam_init = 0.8 - 0.6 * math.exp(-0.3 * l)
        p = {'w_in': w_in[l], 'conv_a_w': conv_a_w[l], 'conv_a_b': conv_a_b[l], 'conv_a_g': conv_a_g[l],
             'conv_a_beta': conv_a_beta[l], 'conv_b_w': conv_b_w[l], 'lam_q1': lam_q1[l], 'lam_k1': lam_k1[l],
             'lam_q2': lam_q2[l], 'lam_k2': lam_k2[l], 'diff_g': diff_g[l], 'ret_ld_f': ret_ld_f[l],
             'ret_ld_b': ret_ld_b[l], 'w_gate': w_gate[l], 'b_gate': b_gate[l], 'w_branch': w_branch[l],
             'w_o': w_o[l]}
        mod_l = [m[:, None, :] for m in adaln(c, w_mod[l], b_mod[l])]
        mod_c = adaln(c_ctx, w_mod[l], b_mod[l])
        hl = modulate(rms_norm(xl, g_norm1[l]), mod_l[0], mod_l[1])
        hc = modulate(rms_norm(xc, g_norm1[l]), mod_c[0], mod_c[1])
        yl, yc = token_mixers(hl, hc, p, lam_init, last)
        xl = xl + mod_l[2] * yl
        hl2 = modulate(rms_norm(xl, g_norm2[l]), mod_l[3], mod_l[4])
        n_lat = hl2.shape[0] * hl2.shape[1]
        if last:
            ml = grouped_moe(hl2.reshape(n_lat, D_MODEL), w_router, b_router, w1_e[l], w3_e[l], w2_e[l])
            xl = xl + mod_l[5] * ml.reshape(xl.shape)
        else:
            xc = xc + mod_c[2] * yc
            hc2 = modulate(rms_norm(xc, g_norm2[l]), mod_c[3], mod_c[4])
            tokens = jnp.concatenate([hl2.reshape(n_lat, D_MODEL), hc2.reshape(-1, D_MODEL)], axis=0)
            m = grouped_moe(tokens, w_router, b_router, w1_e[l], w3_e[l], w2_e[l])
            xl = xl + mod_l[5] * m[:n_lat].reshape(xl.shape)
            xc = xc + mod_c[5] * m[n_lat:].reshape(xc.shape)
    return rms_norm(xl, g_final)
```

```python
import functools
import math

import numpy as np
import jax
import jax.numpy as jnp
from jax import lax
from jax.experimental import pallas as pl
from jax.experimental.pallas import tpu as pltpu

F32 = jnp.float32
BF16 = jnp.bfloat16
I32 = jnp.int32
HI = lax.Precision.HIGHEST

D = 1024
DEPTH = 2
GRID_W = 64
BW = 256
CONF_K = 31
SCONV_K = 3
HEADS = 4
DK = 32
DV = 64
CHUNK = 128
ROPE_BASE = 10000.0
N_EXPERTS = 16
N_GROUPS = 4
EPG = N_EXPERTS // N_GROUPS
D_FF = 512
MOE_BLOCK = 256
EPS = 1e-6
HALO = 16
LANES = 128
W_COLS = 3072
VMEM_LIMIT = 56 * 1024 * 1024


def _cparams(*sem):
    return pltpu.CompilerParams(dimension_semantics=sem, vmem_limit_bytes=VMEM_LIMIT)


def _row_tile(n, pref):
    return pref if n % pref == 0 else n


def _mod_kernel(c_ref, w_ref, b_ref, o_ref):
    a = c_ref[...]
    a = a * jax.nn.sigmoid(a)
    o_ref[0] = jnp.dot(a, w_ref[0], preferred_element_type=F32, precision=HI) + b_ref[0]


def _adaln(c, c_ctx, w_mod, b_mod):
    cs = jnp.zeros((8, D), F32).at[0].set(c[0]).at[1].set(c_ctx)
    return pl.pallas_call(
        _mod_kernel,
        grid=(DEPTH, 6),
        in_specs=[pl.BlockSpec((8, D), lambda l, j: (0, 0)),
                  pl.BlockSpec((1, D, D), lambda l, j: (l, 0, j)),
                  pl.BlockSpec((1, 1, D), lambda l, j: (l, 0, j))],
        out_specs=pl.BlockSpec((1, 8, D), lambda l, j: (l, 0, j)),
        out_shape=jax.ShapeDtypeStruct((DEPTH, 8, 6 * D), F32),
        compiler_params=_cparams("parallel", "parallel"),
    )(cs, w_mod, b_mod.reshape(DEPTH, 1, 6 * D))


def _rope(x, cos, sinp):
    lane = lax.broadcasted_iota(I32, (1, LANES), 1)
    first = (lane % 32) < 16
    outs = []
    for c in range(x.shape[1] // LANES):
        xc = x[:, c * LANES:(c + 1) * LANES]
        sw = jnp.where(first, pltpu.roll(xc, LANES - 16, 1), pltpu.roll(xc, 16, 1))
        outs.append(xc * cos + sw * sinp)
    return outs[0] if len(outs) == 1 else jnp.concatenate(outs, axis=-1)


def _inproj_kernel(x_ref, g_ref, sh_ref, sc_ref, w_ref, cos_ref, sin_ref, vone_ref,
                   h_ref, u_ref, bg_ref, cx_ref, q_ref, k_ref, v_ref,
                   rq_ref, rk_ref, rv_ref, rg_ref, *, rope):
    x = x_ref[...]
    y = x * lax.rsqrt(jnp.mean(x * x, axis=-1, keepdims=True) + EPS) * g_ref[...]
    hb = (y * (1.0 + sc_ref[...]) + sh_ref[...]).astype(BF16)
    h_ref[...] = hb

    def seg(a, b):
        return jnp.dot(hb, w_ref[:, a:b], preferred_element_type=F32)

    z = seg(0, 512)
    u_ref[...] = z[:, :BW] * jax.nn.sigmoid(z[:, BW:])
    z = seg(512, 1280)
    bg_ref[...] = z[:, :BW]
    cx_ref[...] = z[:, BW:2 * BW] * z[:, 2 * BW:]
    zq = seg(1280, 1536)
    zk = seg(1536, 1792)
    zrq = seg(2304, 2432)
    zrk = seg(2432, 2560)
    if rope:
        cos = cos_ref[...]
        sinp = sin_ref[...]
        zq, zk = _rope(zq, cos, sinp), _rope(zk, cos, sinp)
        zrq, zrk = _rope(zrq, cos, sinp), _rope(zrk, cos, sinp)
    q_ref[...] = (zq * (DK ** -0.5)).astype(BF16)
    k_ref[...] = zk.astype(BF16)
    v_ref[...] = (seg(1792, 2304) + vone_ref[...]).astype(BF16)
    rq_ref[...] = zrq
    rk_ref[...] = zrk * (DK ** -0.5)
    rv_ref[...] = seg(2560, 2816)
    rg_ref[...] = seg(2816, 3072)


def _inproj(x, g, shift, scale, w, cos, sinp, vone, rope):
    n = x.shape[0]
    tm = _row_tile(n, 512)
    row = lambda c: pl.BlockSpec((tm, c), lambda i: (i, 0))
    vec = lambda c: pl.BlockSpec((1, c), lambda i: (0, 0))
    widths = [(D, BF16), (BW, F32), (BW, F32), (BW, F32), (256, BF16), (256, BF16), (512, BF16),
              (128, F32), (128, F32), (256, F32), (256, F32)]
    return pl.pallas_call(
        functools.partial(_inproj_kernel, rope=rope),
        grid=(n // tm,),
        in_specs=[row(D), vec(D), vec(D), vec(D),
                  pl.BlockSpec((D, W_COLS), lambda i: (0, 0)),
                  row(LANES), row(LANES), vec(512)],
        out_specs=[row(c) for c, _ in widths],
        out_shape=[jax.ShapeDtypeStruct((n, c), dt) for c, dt in widths],
        compiler_params=_cparams("parallel"),
    )(x, g, shift, scale, w, cos, sinp, vone)


def _conv_kernel(up_ref, um_ref, un_ref, cp_ref, cm_ref, cn_ref, bg_ref,
                 wa_ref, ba_ref, ga_ref, bta_ref, wb_ref, y0_ref, y1_ref, eu, ec):
    i = pl.program_id(0)
    last = pl.num_programs(0) - 1
    tm = um_ref.shape[0]
    for ext, p_ref, m_ref, n_ref in ((eu, up_ref, um_ref, un_ref), (ec, cp_ref, cm_ref, cn_ref)):
        ext[0:HALO, :] = jnp.where(i > 0, p_ref[...], 0.0)
        ext[HALO:HALO + tm, :] = m_ref[...]
        ext[HALO + tm:, :] = jnp.where(i < last, n_ref[...], 0.0)
    rc = 32
    pad_a = (CONF_K - 1) // 2
    pad_b = (SCONV_K - 1) // 2
    for r0 in range(0, tm, rc):
        acc = jnp.zeros((rc, BW), F32)
        for k in range(CONF_K):
            acc = acc + eu[pl.ds(HALO + r0 + k - pad_a, rc), :] * wa_ref[k:k + 1, :]
        acc = acc + ba_ref[...]
        mu = jnp.mean(acc, axis=-1, keepdims=True)
        xc = acc - mu
        var = jnp.mean(xc * xc, axis=-1, keepdims=True)
        yn = xc * lax.rsqrt(var + EPS) * ga_ref[...] + bta_ref[...]
        y0_ref[pl.ds(r0, rc), :] = (yn * jax.nn.sigmoid(yn)).astype(BF16)
        accb = jnp.zeros((rc, BW), F32)
        for k in range(SCONV_K):
            accb = accb + ec[pl.ds(HALO + r0 + k - pad_b, rc), :] * wb_ref[k:k + 1, :]
        y1_ref[pl.ds(r0, rc), :] = (bg_ref[pl.ds(r0, rc), :] * accb).astype(BF16)


def _convs(u, cx, bg, wa, ba, ga, bta, wb):
    n = u.shape[0]
    tm = _row_tile(n, 256)
    hb = tm // HALO
    nh = n // HALO
    prev = pl.BlockSpec((HALO, BW), lambda i: (jnp.maximum(i * hb - 1, 0), 0))
    main = pl.BlockSpec((tm, BW), lambda i: (i, 0))
    nxt = pl.BlockSpec((HALO, BW), lambda i: (jnp.minimum((i + 1) * hb, nh - 1), 0))
    vec = lambda r: pl.BlockSpec((r, BW), lambda i: (0, 0))
    return pl.pallas_call(
        _conv_kernel,
        grid=(n // tm,),
        in_specs=[prev, main, nxt, prev, main, nxt, main,
                  vec(CONF_K), vec(1), vec(1), vec(1), vec(SCONV_K)],
        out_specs=[main, main],
        out_shape=[jax.ShapeDtypeStruct((n, BW), BF16)] * 2,
        scratch_shapes=[pltpu.VMEM((tm + 2 * HALO, BW), F32)] * 2,
        compiler_params=_cparams("parallel"),
    )(u, u, u, cx, cx, cx, bg, wa, ba, ga, bta, wb)


def _attn_kernel(lam_ref, q_ref, k_ref, v_ref, g_ref, o_ref, *, tk, nk, out_scale):
    h = pl.program_id(0)
    q = q_ref[...]
    tq = q.shape[0]
    lane_q = lax.broadcasted_iota(I32, (1, HEADS * 2 * DK), 1)
    lane_o = lax.broadcasted_iota(I32, (1, LANES), 1)
    outs = []
    for m in range(2):
        lo = (h * 2 + m) * DK
        qm = jnp.where((lane_q >= lo) & (lane_q < lo + DK), q, jnp.zeros_like(q))

        def body(j, carry, qm=qm):
            mx, acc = carry
            start = pl.multiple_of(j * tk, tk)
            s = lax.dot_general(qm, k_ref[pl.ds(start, tk), :], (((1,), (1,)), ((), ())),
                                preferred_element_type=F32)
            mnew = jnp.maximum(mx, jnp.max(s, axis=-1, keepdims=True))
            alpha = jnp.exp(mx - mnew)
            p = jnp.exp(s - mnew).astype(BF16)
            pv = jnp.dot(p, v_ref[pl.ds(start, tk), :], preferred_element_type=F32)
            return mnew, alpha * acc + pv

        init = (jnp.full((tq, 1), -jnp.inf, F32), jnp.zeros((tq, LANES), F32))
        _, acc = lax.fori_loop(0, nk, body, init)
        outs.append(acc / acc[:, DV:DV + 1])
    o = outs[0] - lam_ref[0, 0] * outs[1]
    valid = lane_o < DV
    ms = jnp.sum(jnp.where(valid, o * o, 0.0), axis=-1, keepdims=True) * (1.0 / DV)
    y = o * lax.rsqrt(ms + EPS) * g_ref[...] * out_scale
    o_ref[...] = jnp.where(valid, y, 0.0).astype(BF16)


def _attention(q, k, v, lam, gpad, out_scale):
    n = q.shape[0]
    nkeys = k.shape[0]
    tq = _row_tile(n, 512)
    tk = 1280 if nkeys % 1280 == 0 else 256
    kern = functools.partial(_attn_kernel, tk=tk, nk=nkeys // tk, out_scale=out_scale)
    return pl.pallas_call(
        kern,
        grid=(HEADS, n // tq),
        in_specs=[pl.BlockSpec(memory_space=pltpu.SMEM),
                  pl.BlockSpec((tq, 256), lambda h, i: (i, 0)),
                  pl.BlockSpec((nkeys, 256), lambda h, i: (0, 0)),
                  pl.BlockSpec((nkeys, LANES), lambda h, i: (0, h)),
                  pl.BlockSpec((1, LANES), lambda h, i: (0, 0))],
        out_specs=pl.BlockSpec((tq, LANES), lambda h, i: (i, h)),
        out_shape=jax.ShapeDtypeStruct((n, HEADS * LANES), BF16),
        compiler_params=_cparams("parallel", "parallel"),
    )(lam, q, k, v, gpad)


def _ret_kernel(ld_ref, q_ref, k_ref, v_ref, s0_ref, o_ref, sf_ref, state, dmask, qdec, kdec, cdec):
    d = pl.program_id(0)
    n = pl.program_id(1)
    c = CHUNK
    lane_q = lax.broadcasted_iota(I32, (1, HEADS * DK), 1)
    lane_v = lax.broadcasted_iota(I32, (1, HEADS * DV), 1)

    @pl.when(n == 0)
    def _():
        state[...] = s0_ref[0]
        pi = lax.broadcasted_iota(I32, (c, 1), 0).astype(F32)
        pj = lax.broadcasted_iota(I32, (1, c), 1).astype(F32)
        pi = jnp.where(d == 0, pi, c - 1.0 - pi)
        pj = jnp.where(d == 0, pj, c - 1.0 - pj)
        diff = pi - pj
        lg_q = jnp.zeros((1, HEADS * DK), F32)
        row_h = lax.broadcasted_iota(I32, (HEADS * DK, 1), 0) // DK
        lg_r = jnp.zeros((HEADS * DK, 1), F32)
        for h in range(HEADS):
            lg = ld_ref[d, h]
            dmask[h] = jnp.exp(jnp.where(diff >= 0, diff * lg, -jnp.inf))
            lg_q = jnp.where(lane_q // DK == h, lg, lg_q)
            lg_r = jnp.where(row_h == h, lg, lg_r)
        qdec[...] = jnp.exp((pi + 1.0) * lg_q)
        kdec[...] = jnp.exp((c - 1.0 - pi) * lg_q)
        cdec[...] = jnp.where(row_h == lane_v // DV, jnp.exp(c * lg_r), 0.0)

    q = q_ref[...]
    k = k_ref[...]
    kb = k.astype(BF16)
    vb = v_ref[...].astype(BF16)
    o = jnp.dot((q * qdec[...]).astype(BF16), state[...].astype(BF16), preferred_element_type=F32)
    for h in range(HEADS):
        qm = jnp.where(lane_q // DK == h, q, 0.0).astype(BF16)
        sc = lax.dot_general(qm, kb, (((1,), (1,)), ((), ())), preferred_element_type=F32) * dmask[h]
        oh = jnp.dot(sc.astype(BF16), vb, preferred_element_type=F32)
        o = o + jnp.where(lane_v // DV == h, oh, 0.0)
    o_ref[0] = o
    kv = lax.dot_general((k * kdec[...]).astype(BF16), vb, (((0,), (0,)), ((), ())),
                         preferred_element_type=F32)
    cd = cdec[...]
    state[...] = cd * state[...] + jnp.where(cd != 0.0, kv, 0.0)

    @pl.when(n == pl.num_programs(1) - 1)
    def _():
        sf_ref[0] = state[...]


def _retention(rq, rk, rv, ld, s0):
    n = rq.shape[0]
    nc = n // CHUNK
    cmap = lambda d, i: (i + d * (nc - 1 - 2 * i), 0)
    sshape = (2, HEADS * DK, HEADS * DV)
    return pl.pallas_call(
        _ret_kernel,
        grid=(2, nc),
        in_specs=[pl.BlockSpec(memory_space=pltpu.SMEM),
                  pl.BlockSpec((CHUNK, HEADS * DK), cmap),
                  pl.BlockSpec((CHUNK, HEADS * DK), cmap),
                  pl.BlockSpec((CHUNK, HEADS * DV), cmap),
                  pl.BlockSpec((1,) + sshape[1:], lambda d, i: (d, 0, 0))],
        out_specs=[pl.BlockSpec((1, CHUNK, HEADS * DV), lambda d, i: (d,) + cmap(d, i)),
                   pl.BlockSpec((1,) + sshape[1:], lambda d, i: (d, 0, 0))],
        out_shape=[jax.ShapeDtypeStruct((2, n, HEADS * DV), F32),
                   jax.ShapeDtypeStruct(sshape, F32)],
        scratch_shapes=[pltpu.VMEM(sshape[1:], F32),
                        pltpu.VMEM((HEADS, CHUNK, CHUNK), F32),
                        pltpu.VMEM((CHUNK, HEADS * DK), F32),
                        pltpu.VMEM((CHUNK, HEADS * DK), F32),
                        pltpu.VMEM(sshape[1:], F32)],
        compiler_params=_cparams("arbitrary", "arbitrary"),
    )(ld, rq, rk, rv, s0)


def _first_max(vals):
    idx = jnp.zeros(vals[0].shape, I32)
    best = vals[0]
    for j in range(1, len(vals)):
        upd = vals[j] > best
        idx = jnp.where(upd, j, idx)
        best = jnp.where(upd, vals[j], best)
    return idx, best


def _pick(vals, idx):
    out = vals[-1]
    for j in range(len(vals) - 2, -1, -1):
        out = jnp.where(idx == j, vals[j], out)
    return out


def _merge_kernel(h_ref, y0_ref, y1_ref, y2_ref, ro_ref, rg_ref, x_ref, g1_ref, gn_ref, sh_ref, sc_ref,
                  wg_ref, bgate_ref, wb0_ref, wb1_ref, wb2_ref, wb3_ref, wo_ref, wr_ref, br_ref,
                  xo_ref, h2_ref, e_ref, w_ref, rk_ref, cnt_ref, base):
    i = pl.program_id(0)
    tm = x_ref.shape[0]

    @pl.when(i == 0)
    def _():
        base[...] = jnp.zeros_like(base)

    ro = ro_ref[0] + ro_ref[1]
    gr = lax.broadcasted_iota(I32, (HEADS * DV, HEADS * DV), 0) // DV
    gc = lax.broadcasted_iota(I32, (HEADS * DV, HEADS * DV), 1) // DV
    avg = jnp.where(gr == gc, 1.0 / DV, 0.0).astype(F32)
    mu = jnp.dot(ro, avg, preferred_element_type=F32, precision=HI)
    xc = ro - mu
    var = jnp.dot(xc * xc, avg, preferred_element_type=F32, precision=HI)
    rg = rg_ref[...]
    y3 = (rg * jax.nn.sigmoid(rg) * (xc * lax.rsqrt(var + EPS))).astype(BF16)

    hb = h_ref[...]
    m = None
    for b, (y, wb_ref) in enumerate(((y0_ref[...], wb0_ref), (y1_ref[...], wb1_ref),
                                     (y2_ref[...], wb2_ref), (y3, wb3_ref))):
        gl = jnp.dot(hb, wg_ref[:, b * D:(b + 1) * D], preferred_element_type=F32)
        gate = jax.nn.sigmoid(gl + bgate_ref[:, b * D:(b + 1) * D])
        t = gate * jnp.dot(y, wb_ref[...], preferred_element_type=F32)
        m = t if m is None else m + t
    yo = jnp.dot(m.astype(BF16), wo_ref[...], preferred_element_type=F32)
    xn = x_ref[...] + g1_ref[...] * yo
    xo_ref[...] = xn
    yn = xn * lax.rsqrt(jnp.mean(xn * xn, axis=-1, keepdims=True) + EPS) * gn_ref[...]
    h2 = yn * (1.0 + sc_ref[...]) + sh_ref[...]
    h2_ref[...] = h2

    lt = lax.dot_general(wr_ref[...], h2, (((1,), (1,)), ((), ())),
                         preferred_element_type=F32, precision=HI)
    s = jax.nn.sigmoid(lt)
    sb = s + br_ref[...]
    r = [sb[e:e + 1, :] for e in range(N_EXPERTS)]
    sr = [s[e:e + 1, :] for e in range(N_EXPERTS)]
    gscore = []
    for g in range(N_GROUPS):
        a, b, c, d = r[EPG * g:EPG * (g + 1)]
        hi1, lo1, hi2, lo2 = jnp.maximum(a, b), jnp.minimum(a, b), jnp.maximum(c, d), jnp.minimum(c, d)
        gscore.append(jnp.maximum(hi1, hi2) + jnp.maximum(jnp.minimum(hi1, hi2), jnp.maximum(lo1, lo2)))
    gsel, _ = _first_max(gscore)
    v = [_pick([r[EPG * g + j] for g in range(N_GROUPS)], gsel) for j in range(EPG)]
    sv = [_pick([sr[EPG * g + j] for g in range(N_GROUPS)], gsel) for j in range(EPG)]
    i1, _ = _first_max(v)
    i2, _ = _first_max([jnp.where(i1 == j, -jnp.inf, v[j]) for j in range(EPG)])
    w1 = _pick(sv, i1)
    w2 = _pick(sv, i2)
    den = w1 + w2
    e1 = gsel * EPG + i1
    e2 = gsel * EPG + i2
    e_ref[0:1, :] = e1
    e_ref[1:2, :] = e2
    w_ref[0:1, :] = w1 / den
    w_ref[1:2, :] = w2 / den

    eio = lax.broadcasted_iota(I32, (N_EXPERTS, tm), 0)
    oh1 = eio == e1
    oh2 = eio == e2
    cnt = oh1.astype(F32) + oh2.astype(F32)
    ti = lax.broadcasted_iota(I32, (tm, tm), 0)
    tj = lax.broadcasted_iota(I32, (tm, tm), 1)
    before = jnp.where(ti < tj, 1.0, 0.0).astype(BF16)
    tot = base[...] + jnp.dot(cnt.astype(BF16), before, preferred_element_type=F32)
    rk_ref[0:1, :] = jnp.sum(jnp.where(oh1, tot, 0.0), axis=0, keepdims=True).astype(I32)
    rk_ref[1:2, :] = jnp.sum(jnp.where(oh2, tot, 0.0), axis=0, keepdims=True).astype(I32)
    newbase = base[...] + jnp.sum(cnt, axis=1, keepdims=True)
    base[...] = newbase
    cnt_ref[...] = jnp.broadcast_to(newbase, (N_EXPERTS, LANES)).astype(I32)


def _merge(h, y0, y1, y2, ro, rg, x, g1, gn, sh, sc, wts):
    n = x.shape[0]
    tm = _row_tile(n, 256)
    row = lambda c: pl.BlockSpec((tm, c), lambda i: (i, 0))
    full = lambda a: pl.BlockSpec(a.shape, lambda i: (0,) * a.ndim)
    tok = pl.BlockSpec((2, tm), lambda i: (0, i))
    return pl.pallas_call(
        _merge_kernel,
        grid=(n // tm,),
        in_specs=[row(D), row(BW), row(BW), row(HEADS * LANES),
                  pl.BlockSpec((2, tm, HEADS * DV), lambda i: (0, i, 0)), row(BW), row(D),
                  full(g1), full(gn), full(sh), full(sc)] + [full(a) for a in wts],
        out_specs=[row(D), row(D), tok, tok, tok,
                   pl.BlockSpec((N_EXPERTS, LANES), lambda i: (0, 0))],
        out_shape=[jax.ShapeDtypeStruct((n, D), F32), jax.ShapeDtypeStruct((n, D), F32),
                   jax.ShapeDtypeStruct((2, n), I32), jax.ShapeDtypeStruct((2, n), F32),
                   jax.ShapeDtypeStruct((2, n), I32),
                   jax.ShapeDtypeStruct((N_EXPERTS, LANES), I32)],
        scratch_shapes=[pltpu.VMEM((N_EXPERTS, 1), F32)],
        compiler_params=_cparams("arbitrary"),
    )(h, y0, y1, y2, ro, rg, x, g1, gn, sh, sc, *wts)


def _row_copy(src, s, dst, t, sem):
    return pltpu.make_async_copy(src.at[pl.ds(s, 1)], dst.at[pl.ds(t, 1)], sem)


def _dispatch_kernel(pos_ref, x_ref, xs_in_ref, xs_ref, sem):
    del xs_in_ref
    tm = x_ref.shape[0]

    def issue(r, carry):
        for k in range(2):
            _row_copy(x_ref, r, xs_ref, pos_ref[k, r], sem).start()
        return carry

    def drain(r, carry):
        for k in range(2):
            _row_copy(x_ref, r, xs_ref, pos_ref[k, r], sem).wait()
        return carry

    lax.fori_loop(0, tm, issue, 0)
    lax.fori_loop(0, tm, drain, 0)


def _dispatch(h2, pos, cap):
    n = h2.shape[0]
    tm = _row_tile(n, 256)
    return pl.pallas_call(
        _dispatch_kernel,
        grid=(n // tm,),
        in_specs=[pl.BlockSpec((2, tm), lambda i: (0, i), memory_space=pltpu.SMEM),
                  pl.BlockSpec((tm, D), lambda i: (i, 0)),
                  pl.BlockSpec(memory_space=pl.ANY)],
        out_specs=pl.BlockSpec(memory_space=pl.ANY),
        out_shape=jax.ShapeDtypeStruct((cap, D), F32),
        scratch_shapes=[pltpu.SemaphoreType.DMA(())],
        input_output_aliases={2: 0},
        compiler_params=_cparams("arbitrary"),
    )(pos, h2, jnp.zeros((cap, D), F32))


def _expert_kernel(be_ref, nu_ref, x_ref, w1_ref, w3_ref, w2_ref, y_ref):
    del be_ref
    b = pl.program_id(0)

    @pl.when(b < nu_ref[0])
    def _():
        x = x_ref[...].astype(BF16)
        a = jnp.dot(x, w1_ref[0], preferred_element_type=F32)
        c = jnp.dot(x, w3_ref[0], preferred_element_type=F32)
        u = (a * jax.nn.sigmoid(a) * c).astype(BF16)
        y_ref[...] = jnp.dot(u, w2_ref[0], preferred_element_type=F32)

    @pl.when(b >= nu_ref[0])
    def _():
        y_ref[...] = jnp.zeros_like(y_ref)


def _experts(xs, blk_e, nused, w1, w3, w2):
    cap = xs.shape[0]
    nb = cap // MOE_BLOCK
    xmap = lambda b, be, nu: (jnp.minimum(b, nu[0] - 1), 0)
    wmap = lambda b, be, nu: (be[b], 0, 0)
    return pl.pallas_call(
        _expert_kernel,
        grid_spec=pltpu.PrefetchScalarGridSpec(
            num_scalar_prefetch=2,
            grid=(nb,),
            in_specs=[pl.BlockSpec((MOE_BLOCK, D), xmap),
                      pl.BlockSpec((1, D, D_FF), wmap),
                      pl.BlockSpec((1, D, D_FF), wmap),
                      pl.BlockSpec((1, D_FF, D), wmap)],
            out_specs=pl.BlockSpec((MOE_BLOCK, D), lambda b, be, nu: (b, 0))),
        out_shape=jax.ShapeDtypeStruct((cap, D), F32),
        compiler_params=_cparams("arbitrary"),
    )(blk_e, nused, xs, w1, w3, w2)


def _combine_kernel(pos_ref, x_ref, g2_ref, w_ref, gf_ref, ys_ref, o_ref, buf, sem, *, final_norm):
    tm = x_ref.shape[0]

    def issue(r, carry):
        for k in range(2):
            _row_copy(ys_ref, pos_ref[k, r], buf.at[k], r, sem).start()
        return carry

    def drain(r, carry):
        for k in range(2):
            _row_copy(ys_ref, pos_ref[k, r], buf.at[k], r, sem).wait()
        return carry

    lax.fori_loop(0, tm, issue, 0)
    lax.fori_loop(0, tm, drain, 0)
    w = w_ref[...]
    ml = buf[0] * w[:, 0:1] + buf[1] * w[:, 1:2]
    xo = x_ref[...] + g2_ref[...] * ml
    if final_norm:
        xo = xo * lax.rsqrt(jnp.mean(xo * xo, axis=-1, keepdims=True) + EPS) * gf_ref[...]
    o_ref[...] = xo


def _combine(x, g2, pos, wtok, ys, gf, final_norm):
    n = x.shape[0]
    tm = _row_tile(n, 256)
    vec = pl.BlockSpec((1, D), lambda i: (0, 0))
    return pl.pallas_call(
        functools.partial(_combine_kernel, final_norm=final_norm),
        grid=(n // tm,),
        in_specs=[pl.BlockSpec((2, tm), lambda i: (0, i), memory_space=pltpu.SMEM),
                  pl.BlockSpec((tm, D), lambda i: (i, 0)), vec,
                  pl.BlockSpec((tm, 2), lambda i: (i, 0)), vec,
                  pl.BlockSpec(memory_space=pl.ANY)],
        out_specs=pl.BlockSpec((tm, D), lambda i: (i, 0)),
        out_shape=jax.ShapeDtypeStruct((n, D), F32),
        scratch_shapes=[pltpu.VMEM((2, tm, D), F32), pltpu.SemaphoreType.DMA(())],
        compiler_params=_cparams("arbitrary"),
    )(pos, x, g2, wtok, gf, ys)


def _moe(x, h2, e, wsel, rank, counts, g2, w1, w3, w2, gf, final_norm):
    n = x.shape[0]
    cap = (2 * n + N_EXPERTS * MOE_BLOCK + MOE_BLOCK - 1) // MOE_BLOCK * MOE_BLOCK
    nb = cap // MOE_BLOCK
    padded = (counts + MOE_BLOCK - 1) // MOE_BLOCK * MOE_BLOCK
    pend = jnp.cumsum(padded)
    pstart = pend - padded
    pos = (pstart[e] + rank).astype(I32)
    blk_e = jnp.minimum(jnp.searchsorted(pend, jnp.arange(nb, dtype=I32) * MOE_BLOCK, side='right'),
                        N_EXPERTS - 1).astype(I32)
    nused = (pend[-1:] // MOE_BLOCK).astype(I32)
    xs = _dispatch(h2, pos, cap)
    ys = _experts(xs, blk_e, nused, w1, w3, w2)
    return _combine(x, g2, pos, wsel.T, ys, gf, final_norm)


def _rope_tables(n):
    rows = n // GRID_W
    row = jnp.repeat(jnp.arange(rows, dtype=F32), GRID_W)
    col = jnp.tile(jnp.arange(GRID_W, dtype=F32), rows)
    nf = DK // 4
    inv = ROPE_BASE ** (-jnp.arange(nf, dtype=F32) / nf)
    ang = jnp.concatenate([row[:, None] * inv, col[:, None] * inv], axis=-1)
    cos, sin = jnp.cos(ang), jnp.sin(ang)
    reps = LANES // DK
    return (jnp.tile(jnp.concatenate([cos, cos], -1), (1, reps)),
            jnp.tile(jnp.concatenate([-sin, sin], -1), (1, reps)))


def _layer_weights(l, w_in, w_gate, b_gate, w_branch, w_o, w_router, b_router):
    wi = w_in[l]
    wv = wi[:, 1792:2048].reshape(D, HEADS, DV)
    wv = jnp.concatenate([wv, jnp.zeros((D, HEADS, LANES - DV), F32)], axis=-1).reshape(D, HEADS * LANES)
    w = jnp.concatenate([wi[:, :1792], wv, wi[:, 2048:]], axis=-1).astype(BF16)
    wb = w_branch[l]
    wb2 = wb[2].reshape(HEADS, DV, D)
    wb2 = jnp.concatenate([wb2, jnp.zeros((HEADS, LANES - DV, D), F32)], axis=1).reshape(HEADS * LANES, D)
    merge_w = (w_gate[l].astype(BF16), b_gate[l].reshape(1, 4 * D), wb[0].astype(BF16), wb[1].astype(BF16),
               wb2.astype(BF16), wb[3].astype(BF16), w_o[l].astype(BF16),
               w_router.T, b_router.reshape(N_EXPERTS, 1))
    return w, merge_w


def kernel(x, c, ctx, c_ctx, w_mod, b_mod, g_norm1, g_norm2, w_in, conv_a_w, conv_a_b, conv_a_g, conv_a_beta, conv_b_w, lam_q1, lam_k1, lam_q2, lam_k2, diff_g, ret_ld_f, ret_ld_b, w_gate, b_gate, w_branch, w_o, w_router, b_router, w1_e, w3_e, w2_e, g_final):
    assert x.shape[0] == 1 and ctx.shape[0] == 1
    xl, xc = x[0], ctx[0]
    n_lat, n_ctx = xl.shape[0], xc.shape[0]
    mods = _adaln(c, c_ctx, w_mod, b_mod)
    cos, sinp = _rope_tables(n_lat)
    zc = jnp.zeros((n_ctx, LANES), F32)
    vone = jnp.zeros((HEADS, LANES), F32).at[:, DV].set(1.0).reshape(1, HEADS * LANES)
    gf = g_final.reshape(1, D)
    vrow = lambda a: a.reshape(1, -1)

    for l in range(DEPTH):
        last = l == DEPTH - 1
        lam_init = 0.8 - 0.6 * math.exp(-0.3 * l)
        ml = [mods[l, 0:1, j * D:(j + 1) * D] for j in range(6)]
        mc = [mods[l, 1:2, j * D:(j + 1) * D] for j in range(6)]
        w, merge_w = _layer_weights(l, w_in, w_gate, b_gate, w_branch, w_o, w_router, b_router)
        g1 = vrow(g_norm1[l])
        g2n = vrow(g_norm2[l])
        lam = (jnp.exp(jnp.sum(lam_q1[l] * lam_k1[l])) - jnp.exp(jnp.sum(lam_q2[l] * lam_k2[l]))
               + lam_init).reshape(1, 1).astype(F32)
        gpad = jnp.concatenate([diff_g[l], jnp.zeros((LANES - DV,), F32)]).reshape(1, LANES)
        ld = jnp.stack([ret_ld_f[l], ret_ld_b[l]]).astype(F32)
        conv_w = (conv_a_w[l], vrow(conv_a_b[l]), vrow(conv_a_g[l]), vrow(conv_a_beta[l]), conv_b_w[l])
        w1, w3, w2 = w1_e[l].astype(BF16), w3_e[l].astype(BF16), w2_e[l].astype(BF16)

        (hc, uc, bgc, cxc, qc, kc, vc, rqc, rkc, rvc, rgc) = _inproj(
            xc, g1, mc[0], mc[1], w, zc, zc, vone, rope=False)
        (hl, ul, bgl, cxl, ql, kl, vl, rql, rkl, rvl, rgl) = _inproj(
            xl, g1, ml[0], ml[1], w, cos, sinp, vone, rope=True)

        kall = jnp.concatenate([kc, kl], axis=0)
        vall = jnp.concatenate([vc, vl], axis=0)
        out_scale = 1.0 - lam_init
        y2l = _attention(ql, kall, vall, lam, gpad, out_scale)
        s0 = jnp.zeros((2, HEADS * DK, HEADS * DV), F32)
        roc, sfin = _retention(rqc, rkc, rvc, ld, s0)
        rol, _ = _retention(rql, rkl, rvl, ld, sfin)
        y0l, y1l = _convs(ul, cxl, bgl, *conv_w)
        xl, h2l, el, wl, rl, cl = _merge(hl, y0l, y1l, y2l, rol, rgl, xl, ml[2], g2n, ml[3], ml[4], merge_w)
        xl = _moe(xl, h2l, el, wl, rl, cl[:, 0], ml[5], w1, w3, w2, gf, last)
        if not last:
            y2c = _attention(qc, kc, vc, lam, gpad, out_scale)
            y0c, y1c = _convs(uc, cxc, bgc, *conv_w)
            xc, h2c, ec, wc, rc, cc = _merge(hc, y0c, y1c, y2c, roc, rgc, xc, mc[2], g2n, mc[3], mc[4], merge_w)
            xc = _moe(xc, h2c, ec, wc, rc, cc[:, 0], mc[5], w1, w3, w2, gf, False)
    return xl[None]
```

```python
import functools
import math

import numpy as np
import jax
import jax.numpy as jnp
from jax import lax
from jax.experimental import pallas as pl
from jax.experimental.pallas import tpu as pltpu

F32 = jnp.float32
BF16 = jnp.bfloat16
I32 = jnp.int32
HI = lax.Precision.HIGHEST

D = 1024
DEPTH = 2
GRID_W = 64
BW = 256
CONF_K = 31
SCONV_K = 3
HEADS = 4
DK = 32
DV = 64
CHUNK = 128
ROPE_BASE = 10000.0
N_EXPERTS = 16
N_GROUPS = 4
EPG = N_EXPERTS // N_GROUPS
D_FF = 512
MOE_BLOCK = 256
EPS = 1e-6
LOG2E = math.log2(math.e)
BOUND_SLACK = 1.0 + 2.0 ** -10
L_MIN = 2.0 ** -64
HALO = 16
LANES = 128
W_COLS = 3072
VMEM_LIMIT = 56 * 1024 * 1024


def _cparams(*sem):
    return pltpu.CompilerParams(dimension_semantics=sem, vmem_limit_bytes=VMEM_LIMIT)


def _row_tile(n, pref):
    return pref if n % pref == 0 else n


def _mod_kernel(c_ref, w_ref, b_ref, o_ref):
    a = c_ref[...]
    a = a * jax.nn.sigmoid(a)
    o_ref[0] = jnp.dot(a, w_ref[0], preferred_element_type=F32, precision=HI) + b_ref[0]


def _adaln(c, c_ctx, w_mod, b_mod):
    cs = jnp.zeros((8, D), F32).at[0].set(c[0]).at[1].set(c_ctx)
    return pl.pallas_call(
        _mod_kernel,
        grid=(DEPTH, 6),
        in_specs=[pl.BlockSpec((8, D), lambda l, j: (0, 0)),
                  pl.BlockSpec((1, D, D), lambda l, j: (l, 0, j)),
                  pl.BlockSpec((1, 1, D), lambda l, j: (l, 0, j))],
        out_specs=pl.BlockSpec((1, 8, D), lambda l, j: (l, 0, j)),
        out_shape=jax.ShapeDtypeStruct((DEPTH, 8, 6 * D), F32),
        compiler_params=_cparams("parallel", "parallel"),
    )(cs, w_mod, b_mod.reshape(DEPTH, 1, 6 * D))


def _rope(x, cos, sinp):
    lane = lax.broadcasted_iota(I32, (1, LANES), 1)
    first = (lane % 32) < 16
    outs = []
    for c in range(x.shape[1] // LANES):
        xc = x[:, c * LANES:(c + 1) * LANES]
        sw = jnp.where(first, pltpu.roll(xc, LANES - 16, 1), pltpu.roll(xc, 16, 1))
        outs.append(xc * cos + sw * sinp)
    return outs[0] if len(outs) == 1 else jnp.concatenate(outs, axis=-1)


def _inproj_kernel(x_ref, g_ref, sh_ref, sc_ref, w_ref, cos_ref, sin_ref, vone_ref,
                   h_ref, u_ref, bg_ref, cx_ref, q_ref, k_ref, v_ref,
                   rq_ref, rk_ref, rv_ref, rg_ref, *, rope):
    x = x_ref[...]
    y = x * lax.rsqrt(jnp.mean(x * x, axis=-1, keepdims=True) + EPS) * g_ref[...]
    hb = (y * (1.0 + sc_ref[...]) + sh_ref[...]).astype(BF16)
    h_ref[...] = hb

    def seg(a, b):
        return jnp.dot(hb, w_ref[:, a:b], preferred_element_type=F32)

    z = seg(0, 512)
    u_ref[...] = z[:, :BW] * jax.nn.sigmoid(z[:, BW:])
    z = seg(512, 1280)
    bg_ref[...] = z[:, :BW]
    cx_ref[...] = z[:, BW:2 * BW] * z[:, 2 * BW:]
    zq = seg(1280, 1536)
    zk = seg(1536, 1792)
    zrq = seg(2304, 2432)
    zrk = seg(2432, 2560)
    if rope:
        cos = cos_ref[...]
        sinp = sin_ref[...]
        zq, zk = _rope(zq, cos, sinp), _rope(zk, cos, sinp)
        zrq, zrk = _rope(zrq, cos, sinp), _rope(zrk, cos, sinp)
    q_ref[...] = (zq * (DK ** -0.5 * LOG2E)).astype(BF16)
    k_ref[...] = zk.astype(BF16)
    v_ref[...] = (seg(1792, 2304) + vone_ref[...]).astype(BF16)
    rq_ref[...] = zrq
    rk_ref[...] = zrk * (DK ** -0.5)
    rv_ref[...] = seg(2560, 2816)
    rg_ref[...] = seg(2816, 3072)


def _inproj(x, g, shift, scale, w, cos, sinp, vone, rope):
    n = x.shape[0]
    tm = _row_tile(n, 512)
    row = lambda c: pl.BlockSpec((tm, c), lambda i: (i, 0))
    vec = lambda c: pl.BlockSpec((1, c), lambda i: (0, 0))
    widths = [(D, BF16), (BW, F32), (BW, F32), (BW, F32), (256, BF16), (256, BF16), (512, BF16),
              (128, F32), (128, F32), (256, F32), (256, F32)]
    return pl.pallas_call(
        functools.partial(_inproj_kernel, rope=rope),
        grid=(n // tm,),
        in_specs=[row(D), vec(D), vec(D), vec(D),
                  pl.BlockSpec((D, W_COLS), lambda i: (0, 0)),
                  row(LANES), row(LANES), vec(512)],
        out_specs=[row(c) for c, _ in widths],
        out_shape=[jax.ShapeDtypeStruct((n, c), dt) for c, dt in widths],
        compiler_params=_cparams("parallel"),
    )(x, g, shift, scale, w, cos, sinp, vone)


def _conv_kernel(up_ref, um_ref, un_ref, cp_ref, cm_ref, cn_ref, bg_ref,
                 wa_ref, ba_ref, ga_ref, bta_ref, wb_ref, y0_ref, y1_ref, eu, ec):
    i = pl.program_id(0)
    last = pl.num_programs(0) - 1
    tm = um_ref.shape[0]
    for ext, p_ref, m_ref, n_ref in ((eu, up_ref, um_ref, un_ref), (ec, cp_ref, cm_ref, cn_ref)):
        ext[0:HALO, :] = jnp.where(i > 0, p_ref[...], 0.0)
        ext[HALO:HALO + tm, :] = m_ref[...]
        ext[HALO + tm:, :] = jnp.where(i < last, n_ref[...], 0.0)
    rc = 32
    pad_a = (CONF_K - 1) // 2
    pad_b = (SCONV_K - 1) // 2
    for r0 in range(0, tm, rc):
        acc = jnp.zeros((rc, BW), F32)
        for k in range(CONF_K):
            acc = acc + eu[pl.ds(HALO + r0 + k - pad_a, rc), :] * wa_ref[k:k + 1, :]
        acc = acc + ba_ref[...]
        mu = jnp.mean(acc, axis=-1, keepdims=True)
        xc = acc - mu
        var = jnp.mean(xc * xc, axis=-1, keepdims=True)
        yn = xc * lax.rsqrt(var + EPS) * ga_ref[...] + bta_ref[...]
        y0_ref[pl.ds(r0, rc), :] = (yn * jax.nn.sigmoid(yn)).astype(BF16)
        accb = jnp.zeros((rc, BW), F32)
        for k in range(SCONV_K):
            accb = accb + ec[pl.ds(HALO + r0 + k - pad_b, rc), :] * wb_ref[k:k + 1, :]
        y1_ref[pl.ds(r0, rc), :] = (bg_ref[pl.ds(r0, rc), :] * accb).astype(BF16)


def _convs(u, cx, bg, wa, ba, ga, bta, wb):
    n = u.shape[0]
    tm = _row_tile(n, 256)
    hb = tm // HALO
    nh = n // HALO
    prev = pl.BlockSpec((HALO, BW), lambda i: (jnp.maximum(i * hb - 1, 0), 0))
    main = pl.BlockSpec((tm, BW), lambda i: (i, 0))
    nxt = pl.BlockSpec((HALO, BW), lambda i: (jnp.minimum((i + 1) * hb, nh - 1), 0))
    vec = lambda r: pl.BlockSpec((r, BW), lambda i: (0, 0))
    return pl.pallas_call(
        _conv_kernel,
        grid=(n // tm,),
        in_specs=[prev, main, nxt, prev, main, nxt, main,
                  vec(CONF_K), vec(1), vec(1), vec(1), vec(SCONV_K)],
        out_specs=[main, main],
        out_shape=[jax.ShapeDtypeStruct((n, BW), BF16)] * 2,
        scratch_shapes=[pltpu.VMEM((tm + 2 * HALO, BW), F32)] * 2,
        compiler_params=_cparams("parallel"),
    )(u, u, u, cx, cx, cx, bg, wa, ba, ga, bta, wb)


def _attn_kernel(lam_ref, q_ref, k_ref, v_ref, g_ref, o_ref, kmax_sc, acc_sc, *, tk, nk, out_scale):
    h = pl.program_id(0)
    kc = HEADS * 2 * DK

    @pl.when((h == 0) & (pl.program_id(1) == 0))
    def _():
        def colmax(j, mx):
            blk = k_ref[pl.ds(pl.multiple_of(j * kc, kc), kc), :].astype(F32)
            return jnp.maximum(mx, jnp.max(jnp.abs(blk), axis=0, keepdims=True))
        kmax_sc[...] = lax.fori_loop(0, nk * tk // kc, colmax, jnp.zeros((1, kc), F32))

    q = q_ref[...]
    tq = q.shape[0]
    lane_q = lax.broadcasted_iota(I32, (1, kc), 1)
    lane_o = lax.broadcasted_iota(I32, (1, LANES), 1)
    qbound = jnp.abs(q.astype(F32)) * kmax_sc[...]
    qms, shifts = [], []
    for m in range(2):
        lo = (h * 2 + m) * DK
        sel = (lane_q >= lo) & (lane_q < lo + DK)
        qms.append(jnp.where(sel, q, jnp.zeros_like(q)))
        ub = jnp.sum(jnp.where(sel, qbound, 0.0), axis=-1, keepdims=True) * BOUND_SLACK
        shifts.append(jnp.tile(jnp.broadcast_to(ub, (tq, LANES)), (1, tk // LANES)))

    def tiles(j):
        start = pl.multiple_of(j * tk, tk)
        return k_ref[pl.ds(start, tk), :], v_ref[pl.ds(start, tk), :]

    def scores(m, kj):
        return lax.dot_general(qms[m], kj, (((1,), (1,)), ((), ())), preferred_element_type=F32)

    acc_sc[...] = jnp.zeros(acc_sc.shape, F32)

    def shifted(j, carry):
        kj, vj = tiles(j)
        for m in range(2):
            p = jnp.exp2(scores(m, kj) - shifts[m]).astype(BF16)
            acc_sc[m] += jnp.dot(p, vj, preferred_element_type=F32)
        return carry

    lax.fori_loop(0, nk, shifted, 0)
    lmin = jnp.minimum(jnp.min(acc_sc[0][:, DV:DV + 1]), jnp.min(acc_sc[1][:, DV:DV + 1]))

    @pl.when(jnp.logical_not(lmin >= L_MIN))
    def _():
        def online(j, carry):
            kj, vj = tiles(j)
            new = []
            for m in range(2):
                mx, acc = carry[m]
                s = scores(m, kj)
                mnew = jnp.maximum(mx, jnp.max(s, axis=-1, keepdims=True))
                p = jnp.exp2(s - mnew).astype(BF16)
                new.append((mnew, jnp.exp2(mx - mnew) * acc + jnp.dot(p, vj, preferred_element_type=F32)))
            return tuple(new)

        init = (jnp.full((tq, 1), -jnp.inf, F32), jnp.zeros((tq, LANES), F32))
        res = lax.fori_loop(0, nk, online, (init, init))
        for m in range(2):
            acc_sc[m] = res[m][1]

    outs = [acc_sc[m] / acc_sc[m][:, DV:DV + 1] for m in range(2)]
    o = outs[0] - lam_ref[0, 0] * outs[1]
    valid = lane_o < DV
    ms = jnp.sum(jnp.where(valid, o * o, 0.0), axis=-1, keepdims=True) * (1.0 / DV)
    y = o * lax.rsqrt(ms + EPS) * g_ref[...] * out_scale
    o_ref[...] = jnp.where(valid, y, 0.0).astype(BF16)


def _attention(q, k, v, lam, gpad, out_scale):
    n = q.shape[0]
    nkeys = k.shape[0]
    tq = _row_tile(n, 512)
    tk = 1280 if nkeys % 1280 == 0 else 256
    kern = functools.partial(_attn_kernel, tk=tk, nk=nkeys // tk, out_scale=out_scale)
    return pl.pallas_call(
        kern,
        grid=(HEADS, n // tq),
        in_specs=[pl.BlockSpec(memory_space=pltpu.SMEM),
                  pl.BlockSpec((tq, 256), lambda h, i: (i, 0)),
                  pl.BlockSpec((nkeys, 256), lambda h, i: (0, 0)),
                  pl.BlockSpec((nkeys, LANES), lambda h, i: (0, h)),
                  pl.BlockSpec((1, LANES), lambda h, i: (0, 0))],
        out_specs=pl.BlockSpec((tq, LANES), lambda h, i: (i, h)),
        out_shape=jax.ShapeDtypeStruct((n, HEADS * LANES), BF16),
        scratch_shapes=[pltpu.VMEM((1, HEADS * 2 * DK), F32), pltpu.VMEM((2, tq, LANES), F32)],
        compiler_params=_cparams("arbitrary", "arbitrary"),
    )(lam, q, k, v, gpad)


def _ret_kernel(ld_ref, q_ref, k_ref, v_ref, s0_ref, o_ref, sf_ref, state, dmask, qdec, kdec, cdec):
    d = pl.program_id(0)
    n = pl.program_id(1)
    c = CHUNK
    lane_q = lax.broadcasted_iota(I32, (1, HEADS * DK), 1)
    lane_v = lax.broadcasted_iota(I32, (1, HEADS * DV), 1)

    @pl.when(n == 0)
    def _():
        state[...] = s0_ref[0]
        pi = lax.broadcasted_iota(I32, (c, 1), 0).astype(F32)
        pj = lax.broadcasted_iota(I32, (1, c), 1).astype(F32)
        pi = jnp.where(d == 0, pi, c - 1.0 - pi)
        pj = jnp.where(d == 0, pj, c - 1.0 - pj)
        diff = pi - pj
        lg_q = jnp.zeros((1, HEADS * DK), F32)
        row_h = lax.broadcasted_iota(I32, (HEADS * DK, 1), 0) // DK
        lg_r = jnp.zeros((HEADS * DK, 1), F32)
        for h in range(HEADS):
            lg = ld_ref[d, h]
            dmask[h] = jnp.exp(jnp.where(diff >= 0, diff * lg, -jnp.inf))
            lg_q = jnp.where(lane_q // DK == h, lg, lg_q)
            lg_r = jnp.where(row_h == h, lg, lg_r)
        qdec[...] = jnp.exp((pi + 1.0) * lg_q)
        kdec[...] = jnp.exp((c - 1.0 - pi) * lg_q)
        cdec[...] = jnp.where(row_h == lane_v // DV, jnp.exp(c * lg_r), 0.0)

    q = q_ref[...]
    k = k_ref[...]
    kb = k.astype(BF16)
    vb = v_ref[...].astype(BF16)
    o = jnp.dot((q * qdec[...]).astype(BF16), state[...].astype(BF16), preferred_element_type=F32)
    for h in range(HEADS):
        qm = jnp.where(lane_q // DK == h, q, 0.0).astype(BF16)
        sc = lax.dot_general(qm, kb, (((1,), (1,)), ((), ())), preferred_element_type=F32) * dmask[h]
        oh = jnp.dot(sc.astype(BF16), vb, preferred_element_type=F32)
        o = o + jnp.where(lane_v // DV == h, oh, 0.0)
    o_ref[0] = o
    kv = lax.dot_general((k * kdec[...]).astype(BF16), vb, (((0,), (0,)), ((), ())),
                         preferred_element_type=F32)
    cd = cdec[...]
    state[...] = cd * state[...] + jnp.where(cd != 0.0, kv, 0.0)

    @pl.when(n == pl.num_programs(1) - 1)
    def _():
        sf_ref[0] = state[...]


def _retention(rq, rk, rv, ld, s0):
    n = rq.shape[0]
    nc = n // CHUNK
    cmap = lambda d, i: (i + d * (nc - 1 - 2 * i), 0)
    sshape = (2, HEADS * DK, HEADS * DV)
    return pl.pallas_call(
        _ret_kernel,
        grid=(2, nc),
        in_specs=[pl.BlockSpec(memory_space=pltpu.SMEM),
                  pl.BlockSpec((CHUNK, HEADS * DK), cmap),
                  pl.BlockSpec((CHUNK, HEADS * DK), cmap),
                  pl.BlockSpec((CHUNK, HEADS * DV), cmap),
                  pl.BlockSpec((1,) + sshape[1:], lambda d, i: (d, 0, 0))],
        out_specs=[pl.BlockSpec((1, CHUNK, HEADS * DV), lambda d, i: (d,) + cmap(d, i)),
                   pl.BlockSpec((1,) + sshape[1:], lambda d, i: (d, 0, 0))],
        out_shape=[jax.ShapeDtypeStruct((2, n, HEADS * DV), F32),
                   jax.ShapeDtypeStruct(sshape, F32)],
        scratch_shapes=[pltpu.VMEM(sshape[1:], F32),
                        pltpu.VMEM((HEADS, CHUNK, CHUNK), F32),
                        pltpu.VMEM((CHUNK, HEADS * DK), F32),
                        pltpu.VMEM((CHUNK, HEADS * DK), F32),
                        pltpu.VMEM(sshape[1:], F32)],
        compiler_params=_cparams("arbitrary", "arbitrary"),
    )(ld, rq, rk, rv, s0)


def _first_max(vals):
    idx = jnp.zeros(vals[0].shape, I32)
    best = vals[0]
    for j in range(1, len(vals)):
        upd = vals[j] > best
        idx = jnp.where(upd, j, idx)
        best = jnp.where(upd, vals[j], best)
    return idx, best


def _pick(vals, idx):
    out = vals[-1]
    for j in range(len(vals) - 2, -1, -1):
        out = jnp.where(idx == j, vals[j], out)
    return out


def _merge_kernel(h_ref, y0_ref, y1_ref, y2_ref, ro_ref, rg_ref, x_ref, g1_ref, gn_ref, sh_ref, sc_ref,
                  wg_ref, bgate_ref, wb0_ref, wb1_ref, wb2_ref, wb3_ref, wo_ref, wr_ref, br_ref,
                  xo_ref, h2_ref, e_ref, w_ref, rk_ref, cnt_ref, base):
    i = pl.program_id(0)
    tm = x_ref.shape[0]

    @pl.when(i == 0)
    def _():
        base[...] = jnp.zeros_like(base)

    ro = ro_ref[0] + ro_ref[1]
    gr = lax.broadcasted_iota(I32, (HEADS * DV, HEADS * DV), 0) // DV
    gc = lax.broadcasted_iota(I32, (HEADS * DV, HEADS * DV), 1) // DV
    avg = jnp.where(gr == gc, 1.0 / DV, 0.0).astype(F32)
    mu = jnp.dot(ro, avg, preferred_element_type=F32, precision=HI)
    xc = ro - mu
    var = jnp.dot(xc * xc, avg, preferred_element_type=F32, precision=HI)
    rg = rg_ref[...]
    y3 = (rg * jax.nn.sigmoid(rg) * (xc * lax.rsqrt(var + EPS))).astype(BF16)

    hb = h_ref[...]
    m = None
    for b, (y, wb_ref) in enumerate(((y0_ref[...], wb0_ref), (y1_ref[...], wb1_ref),
                                     (y2_ref[...], wb2_ref), (y3, wb3_ref))):
        gl = jnp.dot(hb, wg_ref[:, b * D:(b + 1) * D], preferred_element_type=F32)
        gate = jax.nn.sigmoid(gl + bgate_ref[:, b * D:(b + 1) * D])
        t = gate * jnp.dot(y, wb_ref[...], preferred_element_type=F32)
        m = t if m is None else m + t
    yo = jnp.dot(m.astype(BF16), wo_ref[...], preferred_element_type=F32)
    xn = x_ref[...] + g1_ref[...] * yo
    xo_ref[...] = xn
    yn = xn * lax.rsqrt(jnp.mean(xn * xn, axis=-1, keepdims=True) + EPS) * gn_ref[...]
    h2 = yn * (1.0 + sc_ref[...]) + sh_ref[...]
    h2_ref[...] = h2

    lt = lax.dot_general(wr_ref[...], h2, (((1,), (1,)), ((), ())),
                         preferred_element_type=F32, precision=HI)
    s = jax.nn.sigmoid(lt)
    sb = s + br_ref[...]
    r = [sb[e:e + 1, :] for e in range(N_EXPERTS)]
    sr = [s[e:e + 1, :] for e in range(N_EXPERTS)]
    gscore = []
    for g in range(N_GROUPS):
        a, b, c, d = r[EPG * g:EPG * (g + 1)]
        hi1, lo1, hi2, lo2 = jnp.maximum(a, b), jnp.minimum(a, b), jnp.maximum(c, d), jnp.minimum(c, d)
        gscore.append(jnp.maximum(hi1, hi2) + jnp.maximum(jnp.minimum(hi1, hi2), jnp.maximum(lo1, lo2)))
    gsel, _ = _first_max(gscore)
    v = [_pick([r[EPG * g + j] for g in range(N_GROUPS)], gsel) for j in range(EPG)]
    sv = [_pick([sr[EPG * g + j] for g in range(N_GROUPS)], gsel) for j in range(EPG)]
    i1, _ = _first_max(v)
    i2, _ = _first_max([jnp.where(i1 == j, -jnp.inf, v[j]) for j in range(EPG)])
    w1 = _pick(sv, i1)
    w2 = _pick(sv, i2)
    den = w1 + w2
    e1 = gsel * EPG + i1
    e2 = gsel * EPG + i2
    e_ref[0:1, :] = e1
    e_ref[1:2, :] = e2
    w_ref[0:1, :] = w1 / den
    w_ref[1:2, :] = w2 / den

    eio = lax.broadcasted_iota(I32, (N_EXPERTS, tm), 0)
    oh1 = eio == e1
    oh2 = eio == e2
    cnt = oh1.astype(F32) + oh2.astype(F32)
    ti = lax.broadcasted_iota(I32, (tm, tm), 0)
    tj = lax.broadcasted_iota(I32, (tm, tm), 1)
    before = jnp.where(ti < tj, 1.0, 0.0).astype(BF16)
    tot = base[...] + jnp.dot(cnt.astype(BF16), before, preferred_element_type=F32)
    rk_ref[0:1, :] = jnp.sum(jnp.where(oh1, tot, 0.0), axis=0, keepdims=True).astype(I32)
    rk_ref[1:2, :] = jnp.sum(jnp.where(oh2, tot, 0.0), axis=0, keepdims=True).astype(I32)
    newbase = base[...] + jnp.sum(cnt, axis=1, keepdims=True)
    base[...] = newbase
    cnt_ref[...] = jnp.broadcast_to(newbase, (N_EXPERTS, LANES)).astype(I32)


def _merge(h, y0, y1, y2, ro, rg, x, g1, gn, sh, sc, wts):
    n = x.shape[0]
    tm = _row_tile(n, 256)
    row = lambda c: pl.BlockSpec((tm, c), lambda i: (i, 0))
    full = lambda a: pl.BlockSpec(a.shape, lambda i: (0,) * a.ndim)
    tok = pl.BlockSpec((2, tm), lambda i: (0, i))
    return pl.pallas_call(
        _merge_kernel,
        grid=(n // tm,),
        in_specs=[row(D), row(BW), row(BW), row(HEADS * LANES),
                  pl.BlockSpec((2, tm, HEADS * DV), lambda i: (0, i, 0)), row(BW), row(D),
                  full(g1), full(gn), full(sh), full(sc)] + [full(a) for a in wts],
        out_specs=[row(D), row(D), tok, tok, tok,
                   pl.BlockSpec((N_EXPERTS, LANES), lambda i: (0, 0))],
        out_shape=[jax.ShapeDtypeStruct((n, D), F32), jax.ShapeDtypeStruct((n, D), F32),
                   jax.ShapeDtypeStruct((2, n), I32), jax.ShapeDtypeStruct((2, n), F32),
                   jax.ShapeDtypeStruct((2, n), I32),
                   jax.ShapeDtypeStruct((N_EXPERTS, LANES), I32)],
        scratch_shapes=[pltpu.VMEM((N_EXPERTS, 1), F32)],
        compiler_params=_cparams("arbitrary"),
    )(h, y0, y1, y2, ro, rg, x, g1, gn, sh, sc, *wts)


def _row_copy(src, s, dst, t, sem):
    return pltpu.make_async_copy(src.at[pl.ds(s, 1)], dst.at[pl.ds(t, 1)], sem)


def _dispatch_kernel(pos_ref, x_ref, xs_in_ref, xs_ref, sem):
    del xs_in_ref
    tm = x_ref.shape[0]

    def issue(r, carry):
        for k in range(2):
            _row_copy(x_ref, r, xs_ref, pos_ref[k, r], sem).start()
        return carry

    def drain(r, carry):
        for k in range(2):
            _row_copy(x_ref, r, xs_ref, pos_ref[k, r], sem).wait()
        return carry

    lax.fori_loop(0, tm, issue, 0)
    lax.fori_loop(0, tm, drain, 0)


def _dispatch(h2, pos, cap):
    n = h2.shape[0]
    tm = _row_tile(n, 256)
    return pl.pallas_call(
        _dispatch_kernel,
        grid=(n // tm,),
        in_specs=[pl.BlockSpec((2, tm), lambda i: (0, i), memory_space=pltpu.SMEM),
                  pl.BlockSpec((tm, D), lambda i: (i, 0)),
                  pl.BlockSpec(memory_space=pl.ANY)],
        out_specs=pl.BlockSpec(memory_space=pl.ANY),
        out_shape=jax.ShapeDtypeStruct((cap, D), F32),
        scratch_shapes=[pltpu.SemaphoreType.DMA(())],
        input_output_aliases={2: 0},
        compiler_params=_cparams("arbitrary"),
    )(pos, h2, jnp.zeros((cap, D), F32))


def _expert_kernel(be_ref, nu_ref, x_ref, w1_ref, w3_ref, w2_ref, y_ref):
    del be_ref
    b = pl.program_id(0)

    @pl.when(b < nu_ref[0])
    def _():
        x = x_ref[...].astype(BF16)
        a = jnp.dot(x, w1_ref[0], preferred_element_type=F32)
        c = jnp.dot(x, w3_ref[0], preferred_element_type=F32)
        u = (a * jax.nn.sigmoid(a) * c).astype(BF16)
        y_ref[...] = jnp.dot(u, w2_ref[0], preferred_element_type=F32)

    @pl.when(b >= nu_ref[0])
    def _():
        y_ref[...] = jnp.zeros_like(y_ref)


def _experts(xs, blk_e, nused, w1, w3, w2):
    cap = xs.shape[0]
    nb = cap // MOE_BLOCK
    xmap = lambda b, be, nu: (jnp.minimum(b, nu[0] - 1), 0)
    wmap = lambda b, be, nu: (be[b], 0, 0)
    return pl.pallas_call(
        _expert_kernel,
        grid_spec=pltpu.PrefetchScalarGridSpec(
            num_scalar_prefetch=2,
            grid=(nb,),
            in_specs=[pl.BlockSpec((MOE_BLOCK, D), xmap),
                      pl.BlockSpec((1, D, D_FF), wmap),
                      pl.BlockSpec((1, D, D_FF), wmap),
                      pl.BlockSpec((1, D_FF, D), wmap)],
            out_specs=pl.BlockSpec((MOE_BLOCK, D), lambda b, be, nu: (b, 0))),
        out_shape=jax.ShapeDtypeStruct((cap, D), F32),
        compiler_params=_cparams("arbitrary"),
    )(blk_e, nused, xs, w1, w3, w2)


def _combine_kernel(pos_ref, x_ref, g2_ref, w_ref, gf_ref, ys_ref, o_ref, buf, sem, *, final_norm):
    tm = x_ref.shape[0]

    def issue(r, carry):
        for k in range(2):
            _row_copy(ys_ref, pos_ref[k, r], buf.at[k], r, sem).start()
        return carry

    def drain(r, carry):
        for k in range(2):
            _row_copy(ys_ref, pos_ref[k, r], buf.at[k], r, sem).wait()
        return carry

    lax.fori_loop(0, tm, issue, 0)
    lax.fori_loop(0, tm, drain, 0)
    w = w_ref[...]
    ml = buf[0] * w[:, 0:1] + buf[1] * w[:, 1:2]
    xo = x_ref[...] + g2_ref[...] * ml
    if final_norm:
        xo = xo * lax.rsqrt(jnp.mean(xo * xo, axis=-1, keepdims=True) + EPS) * gf_ref[...]
    o_ref[...] = xo


def _combine(x, g2, pos, wtok, ys, gf, final_norm):
    n = x.shape[0]
    tm = _row_tile(n, 256)
    vec = pl.BlockSpec((1, D), lambda i: (0, 0))
    return pl.pallas_call(
        functools.partial(_combine_kernel, final_norm=final_norm),
        grid=(n // tm,),
        in_specs=[pl.BlockSpec((2, tm), lambda i: (0, i), memory_space=pltpu.SMEM),
                  pl.BlockSpec((tm, D), lambda i: (i, 0)), vec,
                  pl.BlockSpec((tm, 2), lambda i: (i, 0)), vec,
                  pl.BlockSpec(memory_space=pl.ANY)],
        out_specs=pl.BlockSpec((tm, D), lambda i: (i, 0)),
        out_shape=jax.ShapeDtypeStruct((n, D), F32),
        scratch_shapes=[pltpu.VMEM((2, tm, D), F32), pltpu.SemaphoreType.DMA(())],
        compiler_params=_cparams("arbitrary"),
    )(pos, x, g2, wtok, gf, ys)


def _moe(x, h2, e, wsel, rank, counts, g2, w1, w3, w2, gf, final_norm):
    n = x.shape[0]
    cap = (2 * n + N_EXPERTS * MOE_BLOCK + MOE_BLOCK - 1) // MOE_BLOCK * MOE_BLOCK
    nb = cap // MOE_BLOCK
    padded = (counts + MOE_BLOCK - 1) // MOE_BLOCK * MOE_BLOCK
    pend = jnp.cumsum(padded)
    pstart = pend - padded
    pos = (pstart[e] + rank).astype(I32)
    blk_e = jnp.minimum(jnp.searchsorted(pend, jnp.arange(nb, dtype=I32) * MOE_BLOCK, side='right'),
                        N_EXPERTS - 1).astype(I32)
    nused = (pend[-1:] // MOE_BLOCK).astype(I32)
    xs = _dispatch(h2, pos, cap)
    ys = _experts(xs, blk_e, nused, w1, w3, w2)
    return _combine(x, g2, pos, wsel.T, ys, gf, final_norm)


def _rope_tables(n):
    rows = n // GRID_W
    row = jnp.repeat(jnp.arange(rows, dtype=F32), GRID_W)
    col = jnp.tile(jnp.arange(GRID_W, dtype=F32), rows)
    nf = DK // 4
    inv = ROPE_BASE ** (-jnp.arange(nf, dtype=F32) / nf)
    ang = jnp.concatenate([row[:, None] * inv, col[:, None] * inv], axis=-1)
    cos, sin = jnp.cos(ang), jnp.sin(ang)
    reps = LANES // DK
    return (jnp.tile(jnp.concatenate([cos, cos], -1), (1, reps)),
            jnp.tile(jnp.concatenate([-sin, sin], -1), (1, reps)))


def _layer_weights(l, w_in, w_gate, b_gate, w_branch, w_o, w_router, b_router):
    wi = w_in[l]
    wv = wi[:, 1792:2048].reshape(D, HEADS, DV)
    wv = jnp.concatenate([wv, jnp.zeros((D, HEADS, LANES - DV), F32)], axis=-1).reshape(D, HEADS * LANES)
    w = jnp.concatenate([wi[:, :1792], wv, wi[:, 2048:]], axis=-1).astype(BF16)
    wb = w_branch[l]
    wb2 = wb[2].reshape(HEADS, DV, D)
    wb2 = jnp.concatenate([wb2, jnp.zeros((HEADS, LANES - DV, D), F32)], axis=1).reshape(HEADS * LANES, D)
    merge_w = (w_gate[l].astype(BF16), b_gate[l].reshape(1, 4 * D), wb[0].astype(BF16), wb[1].astype(BF16),
               wb2.astype(BF16), wb[3].astype(BF16), w_o[l].astype(BF16),
               w_router.T, b_router.reshape(N_EXPERTS, 1))
    return w, merge_w


def kernel(x, c, ctx, c_ctx, w_mod, b_mod, g_norm1, g_norm2, w_in, conv_a_w, conv_a_b, conv_a_g, conv_a_beta, conv_b_w, lam_q1, lam_k1, lam_q2, lam_k2, diff_g, ret_ld_f, ret_ld_b, w_gate, b_gate, w_branch, w_o, w_router, b_router, w1_e, w3_e, w2_e, g_final):
    assert x.shape[0] == 1 and ctx.shape[0] == 1
    xl, xc = x[0], ctx[0]
    n_lat, n_ctx = xl.shape[0], xc.shape[0]
    mods = _adaln(c, c_ctx, w_mod, b_mod)
    cos, sinp = _rope_tables(n_lat)
    zc = jnp.zeros((n_ctx, LANES), F32)
    vone = jnp.zeros((HEADS, LANES), F32).at[:, DV].set(1.0).reshape(1, HEADS * LANES)
    gf = g_final.reshape(1, D)
    vrow = lambda a: a.reshape(1, -1)

    for l in range(DEPTH):
        last = l == DEPTH - 1
        lam_init = 0.8 - 0.6 * math.exp(-0.3 * l)
        ml = [mods[l, 0:1, j * D:(j + 1) * D] for j in range(6)]
        mc = [mods[l, 1:2, j * D:(j + 1) * D] for j in range(6)]
        w, merge_w = _layer_weights(l, w_in, w_gate, b_gate, w_branch, w_o, w_router, b_router)
        g1 = vrow(g_norm1[l])
        g2n = vrow(g_norm2[l])
        lam = (jnp.exp(jnp.sum(lam_q1[l] * lam_k1[l])) - jnp.exp(jnp.sum(lam_q2[l] * lam_k2[l]))
               + lam_init).reshape(1, 1).astype(F32)
        gpad = jnp.concatenate([diff_g[l], jnp.zeros((LANES - DV,), F32)]).reshape(1, LANES)
        ld = jnp.stack([ret_ld_f[l], ret_ld_b[l]]).astype(F32)
        conv_w = (conv_a_w[l], vrow(conv_a_b[l]), vrow(conv_a_g[l]), vrow(conv_a_beta[l]), conv_b_w[l])
        w1, w3, w2 = w1_e[l].astype(BF16), w3_e[l].astype(BF16), w2_e[l].astype(BF16)

        (hc, uc, bgc, cxc, qc, kc, vc, rqc, rkc, rvc, rgc) = _inproj(
            xc, g1, mc[0], mc[1], w, zc, zc, vone, rope=False)
        (hl, ul, bgl, cxl, ql, kl, vl, rql, rkl, rvl, rgl) = _inproj(
            xl, g1, ml[0], ml[1], w, cos, sinp, vone, rope=True)

        kall = jnp.concatenate([kc, kl], axis=0)
        vall = jnp.concatenate([vc, vl], axis=0)
        out_scale = 1.0 - lam_init
        y2l = _attention(ql, kall, vall, lam, gpad, out_scale)
        s0 = jnp.zeros((2, HEADS * DK, HEADS * DV), F32)
        roc, sfin = _retention(rqc, rkc, rvc, ld, s0)
        rol, _ = _retention(rql, rkl, rvl, ld, sfin)
        y0l, y1l = _convs(ul, cxl, bgl, *conv_w)
        xl, h2l, el, wl, rl, cl = _merge(hl, y0l, y1l, y2l, rol, rgl, xl, ml[2], g2n, ml[3], ml[4], merge_w)
        xl = _moe(xl, h2l, el, wl, rl, cl[:, 0], ml[5], w1, w3, w2, gf, last)
        if not last:
            y2c = _attention(qc, kc, vc, lam, gpad, out_scale)
            y0c, y1c = _convs(uc, cxc, bgc, *conv_w)
            xc, h2c, ec, wc, rc, cc = _merge(hc, y0c, y1c, y2c, roc, rgc, xc, mc[2], g2n, mc[3], mc[4], merge_w)
            xc = _moe(xc, h2c, ec, wc, rc, cc[:, 0], mc[5], w1, w3, w2, gf, False)
    return xl[None]
```

```python
import functools
import math

import numpy as np
import jax
import jax.numpy as jnp
from jax import lax
from jax.experimental import pallas as pl
from jax.experimental.pallas import tpu as pltpu

F32 = jnp.float32
BF16 = jnp.bfloat16
I32 = jnp.int32
HI = lax.Precision.HIGHEST

D = 1024
DEPTH = 2
GRID_W = 64
BW = 256
CONF_K = 31
SCONV_K = 3
HEADS = 4
DK = 32
DV = 64
CHUNK = 128
ROPE_BASE = 10000.0
N_EXPERTS = 16
N_GROUPS = 4
EPG = N_EXPERTS // N_GROUPS
D_FF = 512
MOE_BLOCK = 256
EPS = 1e-6
LOG2E = math.log2(math.e)
BOUND_SLACK = 1.0 + 2.0 ** -10
L_MIN = 2.0 ** -64
ROW_UNROLL = 8
HALO = 16
LANES = 128
SUBLANES = 8
W_COLS = 3072
VMEM_LIMIT = 56 * 1024 * 1024


def _cparams(*sem):
    return pltpu.CompilerParams(dimension_semantics=sem, vmem_limit_bytes=VMEM_LIMIT)


def _row_tile(n, pref):
    return pref if n % pref == 0 else n


def _mod_kernel(c_ref, w_ref, b_ref, o_ref):
    a = c_ref[...]
    a = a * jax.nn.sigmoid(a)
    o_ref[0] = jnp.dot(a, w_ref[0], preferred_element_type=F32, precision=HI) + b_ref[0]


def _adaln(c, c_ctx, w_mod, b_mod):
    cs = jnp.zeros((8, D), F32).at[0].set(c[0]).at[1].set(c_ctx)
    return pl.pallas_call(
        _mod_kernel,
        grid=(DEPTH, 6),
        in_specs=[pl.BlockSpec((8, D), lambda l, j: (0, 0)),
                  pl.BlockSpec((1, D, D), lambda l, j: (l, 0, j)),
                  pl.BlockSpec((1, 1, D), lambda l, j: (l, 0, j))],
        out_specs=pl.BlockSpec((1, 8, D), lambda l, j: (l, 0, j)),
        out_shape=jax.ShapeDtypeStruct((DEPTH, 8, 6 * D), F32),
        compiler_params=_cparams("parallel", "parallel"),
    )(cs, w_mod, b_mod.reshape(DEPTH, 1, 6 * D))


def _rope(x, cos, sinp):
    lane = lax.broadcasted_iota(I32, (1, LANES), 1)
    first = (lane % 32) < 16
    outs = []
    for c in range(x.shape[1] // LANES):
        xc = x[:, c * LANES:(c + 1) * LANES]
        sw = jnp.where(first, pltpu.roll(xc, LANES - 16, 1), pltpu.roll(xc, 16, 1))
        outs.append(xc * cos + sw * sinp)
    return outs[0] if len(outs) == 1 else jnp.concatenate(outs, axis=-1)


def _inproj_kernel(x_ref, g_ref, sh_ref, sc_ref, w_ref, cos_ref, sin_ref, vone_ref,
                   h_ref, u_ref, bg_ref, cx_ref, q_ref, k_ref, v_ref,
                   rq_ref, rk_ref, rv_ref, rg_ref, *, rope):
    x = x_ref[...]
    y = x * lax.rsqrt(jnp.mean(x * x, axis=-1, keepdims=True) + EPS) * g_ref[...]
    hb = (y * (1.0 + sc_ref[...]) + sh_ref[...]).astype(BF16)
    h_ref[...] = hb

    def seg(a, b):
        return jnp.dot(hb, w_ref[:, a:b], preferred_element_type=F32)

    z = seg(0, 512)
    u_ref[...] = z[:, :BW] * jax.nn.sigmoid(z[:, BW:])
    z = seg(512, 1280)
    bg_ref[...] = z[:, :BW]
    cx_ref[...] = z[:, BW:2 * BW] * z[:, 2 * BW:]
    zq = seg(1280, 1536)
    zk = seg(1536, 1792)
    zrq = seg(2304, 2432)
    zrk = seg(2432, 2560)
    if rope:
        cos = cos_ref[...]
        sinp = sin_ref[...]
        zq, zk = _rope(zq, cos, sinp), _rope(zk, cos, sinp)
        zrq, zrk = _rope(zrq, cos, sinp), _rope(zrk, cos, sinp)
    q_ref[...] = (zq * (DK ** -0.5 * LOG2E)).astype(BF16)
    k_ref[...] = zk.astype(BF16)
    v_ref[...] = (seg(1792, 2304) + vone_ref[...]).astype(BF16)
    rq_ref[...] = zrq
    rk_ref[...] = zrk * (DK ** -0.5)
    rv_ref[...] = seg(2560, 2816)
    rg_ref[...] = seg(2816, 3072)


def _inproj(x, g, shift, scale, w, cos, sinp, vone, rope):
    n = x.shape[0]
    tm = _row_tile(n, 512)
    row = lambda c: pl.BlockSpec((tm, c), lambda i: (i, 0))
    vec = lambda c: pl.BlockSpec((1, c), lambda i: (0, 0))
    widths = [(D, BF16), (BW, F32), (BW, F32), (BW, F32), (256, BF16), (256, BF16), (512, BF16),
              (128, F32), (128, F32), (256, F32), (256, F32)]
    return pl.pallas_call(
        functools.partial(_inproj_kernel, rope=rope),
        grid=(n // tm,),
        in_specs=[row(D), vec(D), vec(D), vec(D),
                  pl.BlockSpec((D, W_COLS), lambda i: (0, 0)),
                  row(LANES), row(LANES), vec(512)],
        out_specs=[row(c) for c, _ in widths],
        out_shape=[jax.ShapeDtypeStruct((n, c), dt) for c, dt in widths],
        compiler_params=_cparams("parallel"),
    )(x, g, shift, scale, w, cos, sinp, vone)


def _conv_kernel(up_ref, um_ref, un_ref, cp_ref, cm_ref, cn_ref, bg_ref,
                 wa_ref, ba_ref, ga_ref, bta_ref, wb_ref, y0_ref, y1_ref, eu, ec):
    i = pl.program_id(0)
    last = pl.num_programs(0) - 1
    tm = um_ref.shape[0]
    pad_a = (CONF_K - 1) // 2
    pad_b = (SCONV_K - 1) // 2
    span = tm + 2 * HALO - SUBLANES
    for ext, p_ref, m_ref, n_ref, shifts in ((eu, up_ref, um_ref, un_ref, range(1, SUBLANES)),
                                             (ec, cp_ref, cm_ref, cn_ref, (1, SUBLANES - 1))):
        ext[0, 0:HALO, :] = jnp.where(i > 0, p_ref[...], 0.0)
        ext[0, HALO:HALO + tm, :] = m_ref[...]
        ext[0, HALO + tm:, :] = jnp.where(i < last, n_ref[...], 0.0)
        for s in shifts:
            ext[s, 0:span, :] = ext[0, pl.ds(s, span), :]

    def tap(ext, r0, off):
        return ext[off % SUBLANES, pl.ds(r0 + off - off % SUBLANES, rc), :]

    rc = 32
    for r0 in range(0, tm, rc):
        acc = jnp.zeros((rc, BW), F32)
        for k in range(CONF_K):
            acc = acc + tap(eu, r0, HALO + k - pad_a) * wa_ref[k:k + 1, :]
        acc = acc + ba_ref[...]
        mu = jnp.mean(acc, axis=-1, keepdims=True)
        xc = acc - mu
        var = jnp.mean(xc * xc, axis=-1, keepdims=True)
        yn = xc * lax.rsqrt(var + EPS) * ga_ref[...] + bta_ref[...]
        y0_ref[pl.ds(r0, rc), :] = (yn * jax.nn.sigmoid(yn)).astype(BF16)
        accb = jnp.zeros((rc, BW), F32)
        for k in range(SCONV_K):
            accb = accb + tap(ec, r0, HALO + k - pad_b) * wb_ref[k:k + 1, :]
        y1_ref[pl.ds(r0, rc), :] = (bg_ref[pl.ds(r0, rc), :] * accb).astype(BF16)


def _convs(u, cx, bg, wa, ba, ga, bta, wb):
    n = u.shape[0]
    tm = _row_tile(n, 256)
    hb = tm // HALO
    nh = n // HALO
    prev = pl.BlockSpec((HALO, BW), lambda i: (jnp.maximum(i * hb - 1, 0), 0))
    main = pl.BlockSpec((tm, BW), lambda i: (i, 0))
    nxt = pl.BlockSpec((HALO, BW), lambda i: (jnp.minimum((i + 1) * hb, nh - 1), 0))
    vec = lambda r: pl.BlockSpec((r, BW), lambda i: (0, 0))
    return pl.pallas_call(
        _conv_kernel,
        grid=(n // tm,),
        in_specs=[prev, main, nxt, prev, main, nxt, main,
                  vec(CONF_K), vec(1), vec(1), vec(1), vec(SCONV_K)],
        out_specs=[main, main],
        out_shape=[jax.ShapeDtypeStruct((n, BW), BF16)] * 2,
        scratch_shapes=[pltpu.VMEM((SUBLANES, tm + 2 * HALO, BW), F32)] * 2,
        compiler_params=_cparams("parallel"),
    )(u, u, u, cx, cx, cx, bg, wa, ba, ga, bta, wb)


def _attn_kernel(lam_ref, q_ref, k_ref, v_ref, g_ref, o_ref, kmax_sc, acc_sc, *, tk, nk, out_scale):
    h = pl.program_id(0)
    kc = HEADS * 2 * DK

    @pl.when((h == 0) & (pl.program_id(1) == 0))
    def _():
        def colmax(j, mx):
            blk = k_ref[pl.ds(pl.multiple_of(j * kc, kc), kc), :].astype(F32)
            return jnp.maximum(mx, jnp.max(jnp.abs(blk), axis=0, keepdims=True))
        kmax_sc[...] = lax.fori_loop(0, nk * tk // kc, colmax, jnp.zeros((1, kc), F32))

    q = q_ref[...]
    tq = q.shape[0]
    lane_q = lax.broadcasted_iota(I32, (1, kc), 1)
    lane_o = lax.broadcasted_iota(I32, (1, LANES), 1)
    qbound = jnp.abs(q.astype(F32)) * kmax_sc[...]
    qms, shifts = [], []
    for m in range(2):
        lo = (h * 2 + m) * DK
        sel = (lane_q >= lo) & (lane_q < lo + DK)
        qms.append(jnp.where(sel, q, jnp.zeros_like(q)))
        ub = jnp.sum(jnp.where(sel, qbound, 0.0), axis=-1, keepdims=True) * BOUND_SLACK
        shifts.append(jnp.tile(jnp.broadcast_to(ub, (tq, LANES)), (1, tk // LANES)))

    def tiles(j):
        start = pl.multiple_of(j * tk, tk)
        return k_ref[pl.ds(start, tk), :], v_ref[pl.ds(start, tk), :]

    def scores(m, kj):
        return lax.dot_general(qms[m], kj, (((1,), (1,)), ((), ())), preferred_element_type=F32)

    acc_sc[...] = jnp.zeros(acc_sc.shape, F32)

    def shifted(j, carry):
        kj, vj = tiles(j)
        for m in range(2):
            p = jnp.exp2(scores(m, kj) - shifts[m]).astype(BF16)
            acc_sc[m] += jnp.dot(p, vj, preferred_element_type=F32)
        return carry

    lax.fori_loop(0, nk, shifted, 0)
    lmin = jnp.minimum(jnp.min(acc_sc[0][:, DV:DV + 1]), jnp.min(acc_sc[1][:, DV:DV + 1]))

    @pl.when(jnp.logical_not(lmin >= L_MIN))
    def _():
        def online(j, carry):
            kj, vj = tiles(j)
            new = []
            for m in range(2):
                mx, acc = carry[m]
                s = scores(m, kj)
                mnew = jnp.maximum(mx, jnp.max(s, axis=-1, keepdims=True))
                p = jnp.exp2(s - mnew).astype(BF16)
                new.append((mnew, jnp.exp2(mx - mnew) * acc + jnp.dot(p, vj, preferred_element_type=F32)))
            return tuple(new)

        init = (jnp.full((tq, 1), -jnp.inf, F32), jnp.zeros((tq, LANES), F32))
        res = lax.fori_loop(0, nk, online, (init, init))
        for m in range(2):
            acc_sc[m] = res[m][1]

    outs = [acc_sc[m] / acc_sc[m][:, DV:DV + 1] for m in range(2)]
    o = outs[0] - lam_ref[0, 0] * outs[1]
    valid = lane_o < DV
    ms = jnp.sum(jnp.where(valid, o * o, 0.0), axis=-1, keepdims=True) * (1.0 / DV)
    y = o * lax.rsqrt(ms + EPS) * g_ref[...] * out_scale
    o_ref[...] = jnp.where(valid, y, 0.0).astype(BF16)


def _attention(q, k, v, lam, gpad, out_scale):
    n = q.shape[0]
    nkeys = k.shape[0]
    tq = _row_tile(n, 512)
    tk = 1280 if nkeys % 1280 == 0 else 256
    kern = functools.partial(_attn_kernel, tk=tk, nk=nkeys // tk, out_scale=out_scale)
    return pl.pallas_call(
        kern,
        grid=(HEADS, n // tq),
        in_specs=[pl.BlockSpec(memory_space=pltpu.SMEM),
                  pl.BlockSpec((tq, 256), lambda h, i: (i, 0)),
                  pl.BlockSpec((nkeys, 256), lambda h, i: (0, 0)),
                  pl.BlockSpec((nkeys, LANES), lambda h, i: (0, h)),
                  pl.BlockSpec((1, LANES), lambda h, i: (0, 0))],
        out_specs=pl.BlockSpec((tq, LANES), lambda h, i: (i, h)),
        out_shape=jax.ShapeDtypeStruct((n, HEADS * LANES), BF16),
        scratch_shapes=[pltpu.VMEM((1, HEADS * 2 * DK), F32), pltpu.VMEM((2, tq, LANES), F32)],
        compiler_params=_cparams("arbitrary", "arbitrary"),
    )(lam, q, k, v, gpad)


def _ret_kernel(ld_ref, qf_ref, kf_ref, vf_ref, qb_ref, kb_ref, vb_ref, s0_ref, of_ref, ob_ref, sf_ref,
                state, dmask, qdec, kdec, cdec):
    n = pl.program_id(0)
    c = CHUNK
    lane_q = lax.broadcasted_iota(I32, (1, HEADS * DK), 1)
    lane_v = lax.broadcasted_iota(I32, (1, HEADS * DV), 1)

    @pl.when(n == 0)
    def _():
        state[...] = s0_ref[...]
        row_h = lax.broadcasted_iota(I32, (HEADS * DK, 1), 0) // DK
        for d in range(2):
            pi = lax.broadcasted_iota(I32, (c, 1), 0).astype(F32)
            pj = lax.broadcasted_iota(I32, (1, c), 1).astype(F32)
            if d == 1:
                pi, pj = c - 1.0 - pi, c - 1.0 - pj
            diff = pi - pj
            lg_q = jnp.zeros((1, HEADS * DK), F32)
            lg_r = jnp.zeros((HEADS * DK, 1), F32)
            for h in range(HEADS):
                lg = ld_ref[d, h]
                dmask[d, h] = jnp.exp(jnp.where(diff >= 0, diff * lg, -jnp.inf))
                lg_q = jnp.where(lane_q // DK == h, lg, lg_q)
                lg_r = jnp.where(row_h == h, lg, lg_r)
            qdec[d] = jnp.exp((pi + 1.0) * lg_q)
            kdec[d] = jnp.exp((c - 1.0 - pi) * lg_q)
            cdec[d] = jnp.where(row_h == lane_v // DV, jnp.exp(c * lg_r), 0.0)

    for d, (q_ref, k_ref, v_ref, o_ref) in enumerate(((qf_ref, kf_ref, vf_ref, of_ref),
                                                      (qb_ref, kb_ref, vb_ref, ob_ref))):
        q = q_ref[...]
        k = k_ref[...]
        kb = k.astype(BF16)
        vb = v_ref[...].astype(BF16)
        o = jnp.dot((q * qdec[d]).astype(BF16), state[d].astype(BF16), preferred_element_type=F32)
        for h in range(HEADS):
            qm = jnp.where(lane_q // DK == h, q, 0.0).astype(BF16)
            sc = lax.dot_general(qm, kb, (((1,), (1,)), ((), ())), preferred_element_type=F32) * dmask[d, h]
            oh = jnp.dot(sc.astype(BF16), vb, preferred_element_type=F32)
            o = o + jnp.where(lane_v // DV == h, oh, 0.0)
        o_ref[...] = o
        kv = lax.dot_general((k * kdec[d]).astype(BF16), vb, (((0,), (0,)), ((), ())),
                             preferred_element_type=F32)
        cd = cdec[d]
        state[d] = cd * state[d] + jnp.where(cd != 0.0, kv, 0.0)

    @pl.when(n == pl.num_programs(0) - 1)
    def _():
        sf_ref[...] = state[...]


def _retention(rq, rk, rv, ld, s0):
    n = rq.shape[0]
    nc = n // CHUNK
    fwd = lambda w: pl.BlockSpec((CHUNK, w), lambda i: (i, 0))
    bwd = lambda w: pl.BlockSpec((CHUNK, w), lambda i: (nc - 1 - i, 0))
    sshape = (2, HEADS * DK, HEADS * DV)
    whole = pl.BlockSpec(sshape, lambda i: (0, 0, 0))
    dq, dv = HEADS * DK, HEADS * DV
    return pl.pallas_call(
        _ret_kernel,
        grid=(nc,),
        in_specs=[pl.BlockSpec(memory_space=pltpu.SMEM),
                  fwd(dq), fwd(dq), fwd(dv), bwd(dq), bwd(dq), bwd(dv), whole],
        out_specs=[fwd(dv), bwd(dv), whole],
        out_shape=[jax.ShapeDtypeStruct((n, dv), F32), jax.ShapeDtypeStruct((n, dv), F32),
                   jax.ShapeDtypeStruct(sshape, F32)],
        scratch_shapes=[pltpu.VMEM(sshape, F32),
                        pltpu.VMEM((2, HEADS, CHUNK, CHUNK), F32),
                        pltpu.VMEM((2, CHUNK, dq), F32),
                        pltpu.VMEM((2, CHUNK, dq), F32),
                        pltpu.VMEM(sshape, F32)],
        compiler_params=_cparams("arbitrary"),
    )(ld, rq, rk, rv, rq, rk, rv, s0)


def _first_max(vals):
    idx = jnp.zeros(vals[0].shape, I32)
    best = vals[0]
    for j in range(1, len(vals)):
        upd = vals[j] > best
        idx = jnp.where(upd, j, idx)
        best = jnp.where(upd, vals[j], best)
    return idx, best


def _pick(vals, idx):
    out = vals[-1]
    for j in range(len(vals) - 2, -1, -1):
        out = jnp.where(idx == j, vals[j], out)
    return out


def _merge_kernel(h_ref, y0_ref, y1_ref, y2_ref, rof_ref, rob_ref, rg_ref, x_ref, g1_ref, gn_ref, sh_ref, sc_ref,
                  wg_ref, bgate_ref, wb0_ref, wb1_ref, wb2_ref, wb3_ref, wo_ref, wr_ref, br_ref,
                  xo_ref, h2_ref, e_ref, w_ref, rk_ref, cnt_ref, base):
    i = pl.program_id(0)
    tm = x_ref.shape[0]

    @pl.when(i == 0)
    def _():
        base[...] = jnp.zeros_like(base)

    ro = rof_ref[...] + rob_ref[...]
    gr = lax.broadcasted_iota(I32, (HEADS * DV, HEADS * DV), 0) // DV
    gc = lax.broadcasted_iota(I32, (HEADS * DV, HEADS * DV), 1) // DV
    avg = jnp.where(gr == gc, 1.0 / DV, 0.0).astype(BF16)

    def head_mean(a):
        hi = a.astype(BF16)
        lo = (a - hi.astype(F32)).astype(BF16)
        return (jnp.dot(hi, avg, preferred_element_type=F32) + jnp.dot(lo, avg, preferred_element_type=F32))

    mu = head_mean(ro)
    xc = ro - mu
    var = head_mean(xc * xc)
    rg = rg_ref[...]
    y3 = (rg * jax.nn.sigmoid(rg) * (xc * lax.rsqrt(var + EPS))).astype(BF16)

    hb = h_ref[...]
    m = None
    for b, (y, wb_ref) in enumerate(((y0_ref[...], wb0_ref), (y1_ref[...], wb1_ref),
                                     (y2_ref[...], wb2_ref), (y3, wb3_ref))):
        gl = jnp.dot(hb, wg_ref[:, b * D:(b + 1) * D], preferred_element_type=F32)
        gate = jax.nn.sigmoid(gl + bgate_ref[:, b * D:(b + 1) * D])
        t = gate * jnp.dot(y, wb_ref[...], preferred_element_type=F32)
        m = t if m is None else m + t
    yo = jnp.dot(m.astype(BF16), wo_ref[...], preferred_element_type=F32)
    xn = x_ref[...] + g1_ref[...] * yo
    xo_ref[...] = xn
    yn = xn * lax.rsqrt(jnp.mean(xn * xn, axis=-1, keepdims=True) + EPS) * gn_ref[...]
    h2 = yn * (1.0 + sc_ref[...]) + sh_ref[...]
    h2_ref[...] = h2

    lt = lax.dot_general(wr_ref[...], h2, (((1,), (1,)), ((), ())),
                         preferred_element_type=F32, precision=HI)
    s = jax.nn.sigmoid(lt)
    sb = s + br_ref[...]
    r = [sb[e:e + 1, :] for e in range(N_EXPERTS)]
    sr = [s[e:e + 1, :] for e in range(N_EXPERTS)]
    gscore = []
    for g in range(N_GROUPS):
        a, b, c, d = r[EPG * g:EPG * (g + 1)]
        hi1, lo1, hi2, lo2 = jnp.maximum(a, b), jnp.minimum(a, b), jnp.maximum(c, d), jnp.minimum(c, d)
        gscore.append(jnp.maximum(hi1, hi2) + jnp.maximum(jnp.minimum(hi1, hi2), jnp.maximum(lo1, lo2)))
    gsel, _ = _first_max(gscore)
    v = [_pick([r[EPG * g + j] for g in range(N_GROUPS)], gsel) for j in range(EPG)]
    sv = [_pick([sr[EPG * g + j] for g in range(N_GROUPS)], gsel) for j in range(EPG)]
    i1, _ = _first_max(v)
    i2, _ = _first_max([jnp.where(i1 == j, -jnp.inf, v[j]) for j in range(EPG)])
    w1 = _pick(sv, i1)
    w2 = _pick(sv, i2)
    den = w1 + w2
    e1 = gsel * EPG + i1
    e2 = gsel * EPG + i2
    e_ref[0:1, :] = e1
    e_ref[1:2, :] = e2
    w_ref[0:1, :] = w1 / den
    w_ref[1:2, :] = w2 / den

    eio = lax.broadcasted_iota(I32, (N_EXPERTS, tm), 0)
    oh1 = eio == e1
    oh2 = eio == e2
    cnt = oh1.astype(F32) + oh2.astype(F32)
    ti = lax.broadcasted_iota(I32, (tm, tm), 0)
    tj = lax.broadcasted_iota(I32, (tm, tm), 1)
    before = jnp.where(ti < tj, 1.0, 0.0).astype(BF16)
    tot = base[...] + jnp.dot(cnt.astype(BF16), before, preferred_element_type=F32)
    rk_ref[0:1, :] = jnp.sum(jnp.where(oh1, tot, 0.0), axis=0, keepdims=True).astype(I32)
    rk_ref[1:2, :] = jnp.sum(jnp.where(oh2, tot, 0.0), axis=0, keepdims=True).astype(I32)
    newbase = base[...] + jnp.sum(cnt, axis=1, keepdims=True)
    base[...] = newbase
    cnt_ref[...] = jnp.broadcast_to(newbase, (N_EXPERTS, LANES)).astype(I32)


def _merge(h, y0, y1, y2, rof, rob, rg, x, g1, gn, sh, sc, wts):
    n = x.shape[0]
    tm = _row_tile(n, 512)
    row = lambda c: pl.BlockSpec((tm, c), lambda i: (i, 0))
    full = lambda a: pl.BlockSpec(a.shape, lambda i: (0,) * a.ndim)
    tok = pl.BlockSpec((2, tm), lambda i: (0, i))
    return pl.pallas_call(
        _merge_kernel,
        grid=(n // tm,),
        in_specs=[row(D), row(BW), row(BW), row(HEADS * LANES),
                  row(HEADS * DV), row(HEADS * DV), row(BW), row(D),
                  full(g1), full(gn), full(sh), full(sc)] + [full(a) for a in wts],
        out_specs=[row(D), row(D), tok, tok, tok,
                   pl.BlockSpec((N_EXPERTS, LANES), lambda i: (0, 0))],
        out_shape=[jax.ShapeDtypeStruct((n, D), F32), jax.ShapeDtypeStruct((n, D), F32),
                   jax.ShapeDtypeStruct((2, n), I32), jax.ShapeDtypeStruct((2, n), F32),
                   jax.ShapeDtypeStruct((2, n), I32),
                   jax.ShapeDtypeStruct((N_EXPERTS, LANES), I32)],
        scratch_shapes=[pltpu.VMEM((N_EXPERTS, 1), F32)],
        compiler_params=_cparams("arbitrary"),
    )(h, y0, y1, y2, rof, rob, rg, x, g1, gn, sh, sc, *wts)


def _row_copy(src, s, dst, t, sem):
    return pltpu.make_async_copy(src.at[pl.ds(s, 1)], dst.at[pl.ds(t, 1)], sem)


def _for_rows(n, fn):
    def step(i, carry):
        for u in range(ROW_UNROLL):
            fn(i * ROW_UNROLL + u)
        return carry
    lax.fori_loop(0, n // ROW_UNROLL, step, 0)


def _dispatch_kernel(pos_ref, x_ref, xs_in_ref, xs_ref, sem):
    del xs_in_ref
    tm = x_ref.shape[0]

    def copies(r):
        return [_row_copy(x_ref, r, xs_ref, pos_ref[k, r], sem) for k in range(2)]

    _for_rows(tm, lambda r: [c.start() for c in copies(r)])
    _for_rows(tm, lambda r: [c.wait() for c in copies(r)])


def _dispatch(h2, pos, cap):
    n = h2.shape[0]
    tm = _row_tile(n, 256)
    return pl.pallas_call(
        _dispatch_kernel,
        grid=(n // tm,),
        in_specs=[pl.BlockSpec((2, tm), lambda i: (0, i), memory_space=pltpu.SMEM),
                  pl.BlockSpec((tm, D), lambda i: (i, 0)),
                  pl.BlockSpec(memory_space=pl.ANY)],
        out_specs=pl.BlockSpec(memory_space=pl.ANY),
        out_shape=jax.ShapeDtypeStruct((cap, D), F32),
        scratch_shapes=[pltpu.SemaphoreType.DMA(())],
        input_output_aliases={2: 0},
        compiler_params=_cparams("arbitrary"),
    )(pos, h2, jnp.zeros((cap, D), F32))


def _expert_kernel(be_ref, nu_ref, x_ref, w1_ref, w3_ref, w2_ref, y_ref):
    del be_ref
    b = pl.program_id(0)

    @pl.when(b < nu_ref[0])
    def _():
        x = x_ref[...].astype(BF16)
        a = jnp.dot(x, w1_ref[0], preferred_element_type=F32)
        c = jnp.dot(x, w3_ref[0], preferred_element_type=F32)
        u = (a * jax.nn.sigmoid(a) * c).astype(BF16)
        y_ref[...] = jnp.dot(u, w2_ref[0], preferred_element_type=F32)

    @pl.when(b >= nu_ref[0])
    def _():
        y_ref[...] = jnp.zeros_like(y_ref)


def _experts(xs, blk_e, nused, w1, w3, w2):
    cap = xs.shape[0]
    nb = cap // MOE_BLOCK
    xmap = lambda b, be, nu: (jnp.minimum(b, jnp.maximum(nu[0] - 1, 0)), 0)
    wmap = lambda b, be, nu: (be[b], 0, 0)
    return pl.pallas_call(
        _expert_kernel,
        grid_spec=pltpu.PrefetchScalarGridSpec(
            num_scalar_prefetch=2,
            grid=(nb,),
            in_specs=[pl.BlockSpec((MOE_BLOCK, D), xmap),
                      pl.BlockSpec((1, D, D_FF), wmap),
                      pl.BlockSpec((1, D, D_FF), wmap),
                      pl.BlockSpec((1, D_FF, D), wmap)],
            out_specs=pl.BlockSpec((MOE_BLOCK, D), lambda b, be, nu: (b, 0))),
        out_shape=jax.ShapeDtypeStruct((cap, D), F32),
        compiler_params=_cparams("arbitrary"),
    )(blk_e, nused, xs, w1, w3, w2)


def _combine_kernel(pos_ref, x_ref, g2_ref, w_ref, gf_ref, ys_ref, o_ref, buf, sem, *, final_norm):
    tm = x_ref.shape[0]

    def copies(r):
        return [_row_copy(ys_ref, pos_ref[k, r], buf.at[k], r, sem) for k in range(2)]

    _for_rows(tm, lambda r: [c.start() for c in copies(r)])
    _for_rows(tm, lambda r: [c.wait() for c in copies(r)])
    w = w_ref[...]
    ml = buf[0] * w[:, 0:1] + buf[1] * w[:, 1:2]
    xo = x_ref[...] + g2_ref[...] * ml
    if final_norm:
        xo = xo * lax.rsqrt(jnp.mean(xo * xo, axis=-1, keepdims=True) + EPS) * gf_ref[...]
    o_ref[...] = xo


def _combine(x, g2, pos, wtok, ys, gf, final_norm):
    n = x.shape[0]
    tm = _row_tile(n, 256)
    vec = pl.BlockSpec((1, D), lambda i: (0, 0))
    return pl.pallas_call(
        functools.partial(_combine_kernel, final_norm=final_norm),
        grid=(n // tm,),
        in_specs=[pl.BlockSpec((2, tm), lambda i: (0, i), memory_space=pltpu.SMEM),
                  pl.BlockSpec((tm, D), lambda i: (i, 0)), vec,
                  pl.BlockSpec((tm, 2), lambda i: (i, 0)), vec,
                  pl.BlockSpec(memory_space=pl.ANY)],
        out_specs=pl.BlockSpec((tm, D), lambda i: (i, 0)),
        out_shape=jax.ShapeDtypeStruct((n, D), F32),
        scratch_shapes=[pltpu.VMEM((2, tm, D), F32), pltpu.SemaphoreType.DMA(())],
        compiler_params=_cparams("arbitrary"),
    )(pos, x, g2, wtok, gf, ys)


def _moe(x, h2, e, wsel, rank, counts, g2, w1, w3, w2, gf, final_norm):
    n = x.shape[0]
    cap = (2 * n + N_EXPERTS * MOE_BLOCK + MOE_BLOCK - 1) // MOE_BLOCK * MOE_BLOCK
    nb = cap // MOE_BLOCK
    padded = (counts + MOE_BLOCK - 1) // MOE_BLOCK * MOE_BLOCK
    pend = jnp.cumsum(padded)
    pstart = pend - padded
    eids = jnp.arange(N_EXPERTS, dtype=I32)
    pos = (rank + jnp.sum(jnp.where(e[..., None] == eids, pstart, 0), axis=-1)).astype(I32)
    blk_start = jnp.arange(nb, dtype=I32) * MOE_BLOCK
    blk_e = jnp.minimum(jnp.sum(pend[None, :] <= blk_start[:, None], axis=-1), N_EXPERTS - 1).astype(I32)
    nused = (pend[-1:] // MOE_BLOCK).astype(I32)
    xs = _dispatch(h2, pos, cap)
    ys = _experts(xs, blk_e, nused, w1, w3, w2)
    return _combine(x, g2, pos, wsel.T, ys, gf, final_norm)


def _rope_tables(n):
    rows = n // GRID_W
    row = jnp.repeat(jnp.arange(rows, dtype=F32), GRID_W)
    col = jnp.tile(jnp.arange(GRID_W, dtype=F32), rows)
    nf = DK // 4
    inv = ROPE_BASE ** (-jnp.arange(nf, dtype=F32) / nf)
    ang = jnp.concatenate([row[:, None] * inv, col[:, None] * inv], axis=-1)
    cos, sin = jnp.cos(ang), jnp.sin(ang)
    reps = LANES // DK
    return (jnp.tile(jnp.concatenate([cos, cos], -1), (1, reps)),
            jnp.tile(jnp.concatenate([-sin, sin], -1), (1, reps)))


def _layer_weights(l, w_in, w_gate, b_gate, w_branch, w_o, w_router, b_router):
    wi = w_in[l]
    wv = wi[:, 1792:2048].reshape(D, HEADS, DV)
    wv = jnp.concatenate([wv, jnp.zeros((D, HEADS, LANES - DV), F32)], axis=-1).reshape(D, HEADS * LANES)
    w = jnp.concatenate([wi[:, :1792], wv, wi[:, 2048:]], axis=-1).astype(BF16)
    wb = w_branch[l]
    wb2 = wb[2].reshape(HEADS, DV, D)
    wb2 = jnp.concatenate([wb2, jnp.zeros((HEADS, LANES - DV, D), F32)], axis=1).reshape(HEADS * LANES, D)
    merge_w = (w_gate[l].astype(BF16), b_gate[l].reshape(1, 4 * D), wb[0].astype(BF16), wb[1].astype(BF16),
               wb2.astype(BF16), wb[3].astype(BF16), w_o[l].astype(BF16),
               w_router.T, b_router.reshape(N_EXPERTS, 1))
    return w, merge_w


def kernel(x, c, ctx, c_ctx, w_mod, b_mod, g_norm1, g_norm2, w_in, conv_a_w, conv_a_b, conv_a_g, conv_a_beta, conv_b_w, lam_q1, lam_k1, lam_q2, lam_k2, diff_g, ret_ld_f, ret_ld_b, w_gate, b_gate, w_branch, w_o, w_router, b_router, w1_e, w3_e, w2_e, g_final):
    assert x.shape[0] == 1 and ctx.shape[0] == 1
    xl, xc = x[0], ctx[0]
    n_lat, n_ctx = xl.shape[0], xc.shape[0]
    mods = _adaln(c, c_ctx, w_mod, b_mod)
    cos, sinp = _rope_tables(n_lat)
    zc = jnp.zeros((n_ctx, LANES), F32)
    vone = jnp.zeros((HEADS, LANES), F32).at[:, DV].set(1.0).reshape(1, HEADS * LANES)
    gf = g_final.reshape(1, D)
    vrow = lambda a: a.reshape(1, -1)

    for l in range(DEPTH):
        last = l == DEPTH - 1
        lam_init = 0.8 - 0.6 * math.exp(-0.3 * l)
        ml = [mods[l, 0:1, j * D:(j + 1) * D] for j in range(6)]
        mc = [mods[l, 1:2, j * D:(j + 1) * D] for j in range(6)]
        w, merge_w = _layer_weights(l, w_in, w_gate, b_gate, w_branch, w_o, w_router, b_router)
        g1 = vrow(g_norm1[l])
        g2n = vrow(g_norm2[l])
        lam = (jnp.exp(jnp.sum(lam_q1[l] * lam_k1[l])) - jnp.exp(jnp.sum(lam_q2[l] * lam_k2[l]))
               + lam_init).reshape(1, 1).astype(F32)
        gpad = jnp.concatenate([diff_g[l], jnp.zeros((LANES - DV,), F32)]).reshape(1, LANES)
        ld = jnp.stack([ret_ld_f[l], ret_ld_b[l]]).astype(F32)
        conv_w = (conv_a_w[l], vrow(conv_a_b[l]), vrow(conv_a_g[l]), vrow(conv_a_beta[l]), conv_b_w[l])
        w1, w3, w2 = w1_e[l].astype(BF16), w3_e[l].astype(BF16), w2_e[l].astype(BF16)

        (hc, uc, bgc, cxc, qc, kc, vc, rqc, rkc, rvc, rgc) = _inproj(
            xc, g1, mc[0], mc[1], w, zc, zc, vone, rope=False)
        (hl, ul, bgl, cxl, ql, kl, vl, rql, rkl, rvl, rgl) = _inproj(
            xl, g1, ml[0], ml[1], w, cos, sinp, vone, rope=True)

        kall = jnp.concatenate([kc, kl], axis=0)
        vall = jnp.concatenate([vc, vl], axis=0)
        out_scale = 1.0 - lam_init
        y2l = _attention(ql, kall, vall, lam, gpad, out_scale)
        s0 = jnp.zeros((2, HEADS * DK, HEADS * DV), F32)
        rofc, robc, sfin = _retention(rqc, rkc, rvc, ld, s0)
        rofl, robl, _ = _retention(rql, rkl, rvl, ld, sfin)
        y0l, y1l = _convs(ul, cxl, bgl, *conv_w)
        xl, h2l, el, wl, rl, cl = _merge(hl, y0l, y1l, y2l, rofl, robl, rgl, xl, ml[2], g2n, ml[3], ml[4], merge_w)
        xl = _moe(xl, h2l, el, wl, rl, cl[:, 0], ml[5], w1, w3, w2, gf, last)
        if not last:
            y2c = _attention(qc, kc, vc, lam, gpad, out_scale)
            y0c, y1c = _convs(uc, cxc, bgc, *conv_w)
            xc, h2c, ec, wc, rc, cc = _merge(hc, y0c, y1c, y2c, rofc, robc, rgc, xc, mc[2], g2n, mc[3], mc[4], merge_w)
            xc = _moe(xc, h2c, ec, wc, rc, cc[:, 0], mc[5], w1, w3, w2, gf, False)
    return xl[None]
```

```python
import functools
import math

import numpy as np
import jax
import jax.numpy as jnp
from jax import lax
from jax.experimental import pallas as pl
from jax.experimental.pallas import tpu as pltpu

F32 = jnp.float32
BF16 = jnp.bfloat16
I32 = jnp.int32
HI = lax.Precision.HIGHEST

D = 1024
DEPTH = 2
GRID_W = 64
BW = 256
CONF_K = 31
SCONV_K = 3
HEADS = 4
DK = 32
DV = 64
CHUNK = 128
ROPE_BASE = 10000.0
N_EXPERTS = 16
N_GROUPS = 4
EPG = N_EXPERTS // N_GROUPS
D_FF = 512
MOE_BLOCK = 256
EPS = 1e-6
LOG2E = math.log2(math.e)
BOUND_SLACK = 1.0 + 2.0 ** -10
L_MIN = 2.0 ** -64
MOE_TILE = 512
RUN = 16
HALO = 16
LANES = 128
SUBLANES = 8
W_COLS = 3072
VMEM_LIMIT = 56 * 1024 * 1024


def _cparams(*sem):
    return pltpu.CompilerParams(dimension_semantics=sem, vmem_limit_bytes=VMEM_LIMIT)


def _row_tile(n, pref):
    return pref if n % pref == 0 else n


def _mod_kernel(c_ref, w_ref, b_ref, o_ref):
    a = c_ref[...]
    a = a * jax.nn.sigmoid(a)
    o_ref[0] = jnp.dot(a, w_ref[0], preferred_element_type=F32, precision=HI) + b_ref[0]


def _adaln(c, c_ctx, w_mod, b_mod):
    cs = jnp.zeros((8, D), F32).at[0].set(c[0]).at[1].set(c_ctx)
    return pl.pallas_call(
        _mod_kernel,
        grid=(DEPTH, 6),
        in_specs=[pl.BlockSpec((8, D), lambda l, j: (0, 0)),
                  pl.BlockSpec((1, D, D), lambda l, j: (l, 0, j)),
                  pl.BlockSpec((1, 1, D), lambda l, j: (l, 0, j))],
        out_specs=pl.BlockSpec((1, 8, D), lambda l, j: (l, 0, j)),
        out_shape=jax.ShapeDtypeStruct((DEPTH, 8, 6 * D), F32),
        compiler_params=_cparams("parallel", "parallel"),
    )(cs, w_mod, b_mod.reshape(DEPTH, 1, 6 * D))


def _rope(x, cos, sinp):
    lane = lax.broadcasted_iota(I32, (1, LANES), 1)
    first = (lane % 32) < 16
    outs = []
    for c in range(x.shape[1] // LANES):
        xc = x[:, c * LANES:(c + 1) * LANES]
        sw = jnp.where(first, pltpu.roll(xc, LANES - 16, 1), pltpu.roll(xc, 16, 1))
        outs.append(xc * cos + sw * sinp)
    return outs[0] if len(outs) == 1 else jnp.concatenate(outs, axis=-1)


def _inproj_kernel(x_ref, g_ref, sh_ref, sc_ref, w_ref, cos_ref, sin_ref, vone_ref,
                   h_ref, u_ref, bg_ref, cx_ref, q_ref, k_ref, v_ref,
                   rq_ref, rk_ref, rv_ref, rg_ref, *, rope):
    x = x_ref[...]
    y = x * lax.rsqrt(jnp.mean(x * x, axis=-1, keepdims=True) + EPS) * g_ref[...]
    hb = (y * (1.0 + sc_ref[...]) + sh_ref[...]).astype(BF16)
    h_ref[...] = hb

    def seg(a, b):
        return jnp.dot(hb, w_ref[:, a:b], preferred_element_type=F32)

    z = seg(0, 512)
    u_ref[...] = z[:, :BW] * jax.nn.sigmoid(z[:, BW:])
    z = seg(512, 1280)
    bg_ref[...] = z[:, :BW]
    cx_ref[...] = z[:, BW:2 * BW] * z[:, 2 * BW:]
    zq = seg(1280, 1536)
    zk = seg(1536, 1792)
    zrq = seg(2304, 2432)
    zrk = seg(2432, 2560)
    if rope:
        cos = cos_ref[...]
        sinp = sin_ref[...]
        zq, zk = _rope(zq, cos, sinp), _rope(zk, cos, sinp)
        zrq, zrk = _rope(zrq, cos, sinp), _rope(zrk, cos, sinp)
    q_ref[...] = (zq * (DK ** -0.5 * LOG2E)).astype(BF16)
    k_ref[...] = zk.astype(BF16)
    v_ref[...] = (seg(1792, 2304) + vone_ref[...]).astype(BF16)
    rq_ref[...] = zrq
    rk_ref[...] = zrk * (DK ** -0.5)
    rv_ref[...] = seg(2560, 2816)
    rg_ref[...] = seg(2816, 3072)


def _inproj(x, g, shift, scale, w, cos, sinp, vone, rope):
    n = x.shape[0]
    tm = _row_tile(n, 512)
    row = lambda c: pl.BlockSpec((tm, c), lambda i: (i, 0))
    vec = lambda c: pl.BlockSpec((1, c), lambda i: (0, 0))
    widths = [(D, BF16), (BW, F32), (BW, F32), (BW, F32), (256, BF16), (256, BF16), (512, BF16),
              (128, F32), (128, F32), (256, F32), (256, F32)]
    return pl.pallas_call(
        functools.partial(_inproj_kernel, rope=rope),
        grid=(n // tm,),
        in_specs=[row(D), vec(D), vec(D), vec(D),
                  pl.BlockSpec((D, W_COLS), lambda i: (0, 0)),
                  row(LANES), row(LANES), vec(512)],
        out_specs=[row(c) for c, _ in widths],
        out_shape=[jax.ShapeDtypeStruct((n, c), dt) for c, dt in widths],
        compiler_params=_cparams("parallel"),
    )(x, g, shift, scale, w, cos, sinp, vone)


def _conv_kernel(up_ref, um_ref, un_ref, cp_ref, cm_ref, cn_ref, bg_ref,
                 wa_ref, ba_ref, ga_ref, bta_ref, wb_ref, y0_ref, y1_ref, eu, ec):
    i = pl.program_id(0)
    last = pl.num_programs(0) - 1
    tm = um_ref.shape[0]
    pad_a = (CONF_K - 1) // 2
    pad_b = (SCONV_K - 1) // 2
    span = tm + 2 * HALO - SUBLANES
    for ext, p_ref, m_ref, n_ref, shifts in ((eu, up_ref, um_ref, un_ref, range(1, SUBLANES)),
                                             (ec, cp_ref, cm_ref, cn_ref, (1, SUBLANES - 1))):
        ext[0, 0:HALO, :] = jnp.where(i > 0, p_ref[...], 0.0)
        ext[0, HALO:HALO + tm, :] = m_ref[...]
        ext[0, HALO + tm:, :] = jnp.where(i < last, n_ref[...], 0.0)
        for s in shifts:
            ext[s, 0:span, :] = ext[0, pl.ds(s, span), :]

    def tap(ext, r0, off):
        return ext[off % SUBLANES, pl.ds(r0 + off - off % SUBLANES, rc), :]

    rc = 32
    for r0 in range(0, tm, rc):
        acc = jnp.zeros((rc, BW), F32)
        for k in range(CONF_K):
            acc = acc + tap(eu, r0, HALO + k - pad_a) * wa_ref[k:k + 1, :]
        acc = acc + ba_ref[...]
        mu = jnp.mean(acc, axis=-1, keepdims=True)
        xc = acc - mu
        var = jnp.mean(xc * xc, axis=-1, keepdims=True)
        yn = xc * lax.rsqrt(var + EPS) * ga_ref[...] + bta_ref[...]
        y0_ref[pl.ds(r0, rc), :] = (yn * jax.nn.sigmoid(yn)).astype(BF16)
        accb = jnp.zeros((rc, BW), F32)
        for k in range(SCONV_K):
            accb = accb + tap(ec, r0, HALO + k - pad_b) * wb_ref[k:k + 1, :]
        y1_ref[pl.ds(r0, rc), :] = (bg_ref[pl.ds(r0, rc), :] * accb).astype(BF16)


def _convs(u, cx, bg, wa, ba, ga, bta, wb):
    n = u.shape[0]
    tm = _row_tile(n, 256)
    hb = tm // HALO
    nh = n // HALO
    prev = pl.BlockSpec((HALO, BW), lambda i: (jnp.maximum(i * hb - 1, 0), 0))
    main = pl.BlockSpec((tm, BW), lambda i: (i, 0))
    nxt = pl.BlockSpec((HALO, BW), lambda i: (jnp.minimum((i + 1) * hb, nh - 1), 0))
    vec = lambda r: pl.BlockSpec((r, BW), lambda i: (0, 0))
    return pl.pallas_call(
        _conv_kernel,
        grid=(n // tm,),
        in_specs=[prev, main, nxt, prev, main, nxt, main,
                  vec(CONF_K), vec(1), vec(1), vec(1), vec(SCONV_K)],
        out_specs=[main, main],
        out_shape=[jax.ShapeDtypeStruct((n, BW), BF16)] * 2,
        scratch_shapes=[pltpu.VMEM((SUBLANES, tm + 2 * HALO, BW), F32)] * 2,
        compiler_params=_cparams("parallel"),
    )(u, u, u, cx, cx, cx, bg, wa, ba, ga, bta, wb)


def _attn_kernel(lam_ref, q_ref, k_ref, v_ref, g_ref, o_ref, kmax_sc, acc_sc, *, tk, nk, out_scale):
    h = pl.program_id(0)
    kc = HEADS * 2 * DK

    @pl.when((h == 0) & (pl.program_id(1) == 0))
    def _():
        def colmax(j, mx):
            blk = k_ref[pl.ds(pl.multiple_of(j * kc, kc), kc), :].astype(F32)
            return jnp.maximum(mx, jnp.max(jnp.abs(blk), axis=0, keepdims=True))
        kmax_sc[...] = lax.fori_loop(0, nk * tk // kc, colmax, jnp.zeros((1, kc), F32))

    q = q_ref[...]
    tq = q.shape[0]
    lane_q = lax.broadcasted_iota(I32, (1, kc), 1)
    lane_o = lax.broadcasted_iota(I32, (1, LANES), 1)
    qbound = jnp.abs(q.astype(F32)) * kmax_sc[...]
    qms, shifts = [], []
    for m in range(2):
        lo = (h * 2 + m) * DK
        sel = (lane_q >= lo) & (lane_q < lo + DK)
        qms.append(jnp.where(sel, q, jnp.zeros_like(q)))
        ub = jnp.sum(jnp.where(sel, qbound, 0.0), axis=-1, keepdims=True) * BOUND_SLACK
        shifts.append(jnp.tile(jnp.broadcast_to(ub, (tq, LANES)), (1, tk // LANES)))

    def tiles(j):
        start = pl.multiple_of(j * tk, tk)
        return k_ref[pl.ds(start, tk), :], v_ref[pl.ds(start, tk), :]

    def scores(m, kj):
        return lax.dot_general(qms[m], kj, (((1,), (1,)), ((), ())), preferred_element_type=F32)

    acc_sc[...] = jnp.zeros(acc_sc.shape, F32)

    def shifted(j, carry):
        kj, vj = tiles(j)
        for m in range(2):
            p = jnp.exp2(scores(m, kj) - shifts[m]).astype(BF16)
            acc_sc[m] += jnp.dot(p, vj, preferred_element_type=F32)
        return carry

    lax.fori_loop(0, nk, shifted, 0)
    lmin = jnp.minimum(jnp.min(acc_sc[0][:, DV:DV + 1]), jnp.min(acc_sc[1][:, DV:DV + 1]))

    @pl.when(jnp.logical_not(lmin >= L_MIN))
    def _():
        def online(j, carry):
            kj, vj = tiles(j)
            new = []
            for m in range(2):
                mx, acc = carry[m]
                s = scores(m, kj)
                mnew = jnp.maximum(mx, jnp.max(s, axis=-1, keepdims=True))
                p = jnp.exp2(s - mnew).astype(BF16)
                new.append((mnew, jnp.exp2(mx - mnew) * acc + jnp.dot(p, vj, preferred_element_type=F32)))
            return tuple(new)

        init = (jnp.full((tq, 1), -jnp.inf, F32), jnp.zeros((tq, LANES), F32))
        res = lax.fori_loop(0, nk, online, (init, init))
        for m in range(2):
            acc_sc[m] = res[m][1]

    outs = [acc_sc[m] / acc_sc[m][:, DV:DV + 1] for m in range(2)]
    o = outs[0] - lam_ref[0, 0] * outs[1]
    valid = lane_o < DV
    ms = jnp.sum(jnp.where(valid, o * o, 0.0), axis=-1, keepdims=True) * (1.0 / DV)
    y = o * lax.rsqrt(ms + EPS) * g_ref[...] * out_scale
    o_ref[...] = jnp.where(valid, y, 0.0).astype(BF16)


def _attention(q, k, v, lam, gpad, out_scale):
    n = q.shape[0]
    nkeys = k.shape[0]
    tq = _row_tile(n, 512)
    tk = next(t for t in (3328, 1280, 256) if nkeys % t == 0)
    kern = functools.partial(_attn_kernel, tk=tk, nk=nkeys // tk, out_scale=out_scale)
    return pl.pallas_call(
        kern,
        grid=(HEADS, n // tq),
        in_specs=[pl.BlockSpec(memory_space=pltpu.SMEM),
                  pl.BlockSpec((tq, 256), lambda h, i: (i, 0)),
                  pl.BlockSpec((nkeys, 256), lambda h, i: (0, 0)),
                  pl.BlockSpec((nkeys, LANES), lambda h, i: (0, h)),
                  pl.BlockSpec((1, LANES), lambda h, i: (0, 0))],
        out_specs=pl.BlockSpec((tq, LANES), lambda h, i: (i, h)),
        out_shape=jax.ShapeDtypeStruct((n, HEADS * LANES), BF16),
        scratch_shapes=[pltpu.VMEM((1, HEADS * 2 * DK), F32), pltpu.VMEM((2, tq, LANES), F32)],
        compiler_params=_cparams("arbitrary", "arbitrary"),
    )(lam, q, k, v, gpad)


def _ret_kernel(ld_ref, qf_ref, kf_ref, vf_ref, qb_ref, kb_ref, vb_ref, s0_ref, of_ref, ob_ref, sf_ref,
                state, dmask, qdec, kdec, cdec):
    n = pl.program_id(0)
    c = CHUNK
    lane_q = lax.broadcasted_iota(I32, (1, HEADS * DK), 1)
    lane_v = lax.broadcasted_iota(I32, (1, HEADS * DV), 1)

    @pl.when(n == 0)
    def _():
        state[...] = s0_ref[...]
        row_h = lax.broadcasted_iota(I32, (HEADS * DK, 1), 0) // DK
        for d in range(2):
            pi = lax.broadcasted_iota(I32, (c, 1), 0).astype(F32)
            pj = lax.broadcasted_iota(I32, (1, c), 1).astype(F32)
            if d == 1:
                pi, pj = c - 1.0 - pi, c - 1.0 - pj
            diff = pi - pj
            lg_q = jnp.zeros((1, HEADS * DK), F32)
            lg_r = jnp.zeros((HEADS * DK, 1), F32)
            for h in range(HEADS):
                lg = ld_ref[d, h]
                dmask[d, h] = jnp.exp(jnp.where(diff >= 0, diff * lg, -jnp.inf))
                lg_q = jnp.where(lane_q // DK == h, lg, lg_q)
                lg_r = jnp.where(row_h == h, lg, lg_r)
            qdec[d] = jnp.exp((pi + 1.0) * lg_q)
            kdec[d] = jnp.exp((c - 1.0 - pi) * lg_q)
            cdec[d] = jnp.where(row_h == lane_v // DV, jnp.exp(c * lg_r), 0.0)

    for d, (q_ref, k_ref, v_ref, o_ref) in enumerate(((qf_ref, kf_ref, vf_ref, of_ref),
                                                      (qb_ref, kb_ref, vb_ref, ob_ref))):
        q = q_ref[...]
        k = k_ref[...]
        kb = k.astype(BF16)
        vb = v_ref[...].astype(BF16)
        o = jnp.dot((q * qdec[d]).astype(BF16), state[d].astype(BF16), preferred_element_type=F32)
        for h in range(HEADS):
            qm = jnp.where(lane_q // DK == h, q, 0.0).astype(BF16)
            sc = lax.dot_general(qm, kb, (((1,), (1,)), ((), ())), preferred_element_type=F32) * dmask[d, h]
            oh = jnp.dot(sc.astype(BF16), vb, preferred_element_type=F32)
            o = o + jnp.where(lane_v // DV == h, oh, 0.0)
        o_ref[...] = o
        kv = lax.dot_general((k * kdec[d]).astype(BF16), vb, (((0,), (0,)), ((), ())),
                             preferred_element_type=F32)
        cd = cdec[d]
        state[d] = cd * state[d] + jnp.where(cd != 0.0, kv, 0.0)

    @pl.when(n == pl.num_programs(0) - 1)
    def _():
        sf_ref[...] = state[...]


def _retention(rq, rk, rv, ld, s0):
    n = rq.shape[0]
    nc = n // CHUNK
    fwd = lambda w: pl.BlockSpec((CHUNK, w), lambda i: (i, 0))
    bwd = lambda w: pl.BlockSpec((CHUNK, w), lambda i: (nc - 1 - i, 0))
    sshape = (2, HEADS * DK, HEADS * DV)
    whole = pl.BlockSpec(sshape, lambda i: (0, 0, 0))
    dq, dv = HEADS * DK, HEADS * DV
    return pl.pallas_call(
        _ret_kernel,
        grid=(nc,),
        in_specs=[pl.BlockSpec(memory_space=pltpu.SMEM),
                  fwd(dq), fwd(dq), fwd(dv), bwd(dq), bwd(dq), bwd(dv), whole],
        out_specs=[fwd(dv), bwd(dv), whole],
        out_shape=[jax.ShapeDtypeStruct((n, dv), F32), jax.ShapeDtypeStruct((n, dv), F32),
                   jax.ShapeDtypeStruct(sshape, F32)],
        scratch_shapes=[pltpu.VMEM(sshape, F32),
                        pltpu.VMEM((2, HEADS, CHUNK, CHUNK), F32),
                        pltpu.VMEM((2, CHUNK, dq), F32),
                        pltpu.VMEM((2, CHUNK, dq), F32),
                        pltpu.VMEM(sshape, F32)],
        compiler_params=_cparams("arbitrary"),
    )(ld, rq, rk, rv, rq, rk, rv, s0)


def _first_max(vals):
    idx = jnp.zeros(vals[0].shape, I32)
    best = vals[0]
    for j in range(1, len(vals)):
        upd = vals[j] > best
        idx = jnp.where(upd, j, idx)
        best = jnp.where(upd, vals[j], best)
    return idx, best


def _pick(vals, idx):
    out = vals[-1]
    for j in range(len(vals) - 2, -1, -1):
        out = jnp.where(idx == j, vals[j], out)
    return out


def _merge_kernel(h_ref, y0_ref, y1_ref, y2_ref, rof_ref, rob_ref, rg_ref, x_ref, g1_ref, gn_ref, sh_ref, sc_ref,
                  wg_ref, bgate_ref, wb0_ref, wb1_ref, wb2_ref, wb3_ref, wo_ref, wr_ref, br_ref,
                  xo_ref, h2_ref, lp_ref, w_ref, tab_ref, cnt_ref, base):
    i = pl.program_id(0)
    tm = x_ref.shape[0]

    @pl.when(i == 0)
    def _():
        base[...] = jnp.zeros_like(base)

    ro = rof_ref[...] + rob_ref[...]
    gr = lax.broadcasted_iota(I32, (HEADS * DV, HEADS * DV), 0) // DV
    gc = lax.broadcasted_iota(I32, (HEADS * DV, HEADS * DV), 1) // DV
    avg = jnp.where(gr == gc, 1.0 / DV, 0.0).astype(BF16)

    def head_mean(a):
        hi = a.astype(BF16)
        lo = (a - hi.astype(F32)).astype(BF16)
        return (jnp.dot(hi, avg, preferred_element_type=F32) + jnp.dot(lo, avg, preferred_element_type=F32))

    mu = head_mean(ro)
    xc = ro - mu
    var = head_mean(xc * xc)
    rg = rg_ref[...]
    y3 = (rg * jax.nn.sigmoid(rg) * (xc * lax.rsqrt(var + EPS))).astype(BF16)

    hb = h_ref[...]
    m = None
    for b, (y, wb_ref) in enumerate(((y0_ref[...], wb0_ref), (y1_ref[...], wb1_ref),
                                     (y2_ref[...], wb2_ref), (y3, wb3_ref))):
        gl = jnp.dot(hb, wg_ref[:, b * D:(b + 1) * D], preferred_element_type=F32)
        gate = jax.nn.sigmoid(gl + bgate_ref[:, b * D:(b + 1) * D])
        t = gate * jnp.dot(y, wb_ref[...], preferred_element_type=F32)
        m = t if m is None else m + t
    yo = jnp.dot(m.astype(BF16), wo_ref[...], preferred_element_type=F32)
    xn = x_ref[...] + g1_ref[...] * yo
    xo_ref[...] = xn
    yn = xn * lax.rsqrt(jnp.mean(xn * xn, axis=-1, keepdims=True) + EPS) * gn_ref[...]
    h2 = yn * (1.0 + sc_ref[...]) + sh_ref[...]
    h2_ref[...] = h2.astype(BF16)

    lt = lax.dot_general(wr_ref[...], h2, (((1,), (1,)), ((), ())),
                         preferred_element_type=F32, precision=HI)
    s = jax.nn.sigmoid(lt)
    sb = s + br_ref[...]
    r = [sb[e:e + 1, :] for e in range(N_EXPERTS)]
    sr = [s[e:e + 1, :] for e in range(N_EXPERTS)]
    gscore = []
    for g in range(N_GROUPS):
        a, b, c, d = r[EPG * g:EPG * (g + 1)]
        hi1, lo1, hi2, lo2 = jnp.maximum(a, b), jnp.minimum(a, b), jnp.maximum(c, d), jnp.minimum(c, d)
        gscore.append(jnp.maximum(hi1, hi2) + jnp.maximum(jnp.minimum(hi1, hi2), jnp.maximum(lo1, lo2)))
    gsel, _ = _first_max(gscore)
    v = [_pick([r[EPG * g + j] for g in range(N_GROUPS)], gsel) for j in range(EPG)]
    sv = [_pick([sr[EPG * g + j] for g in range(N_GROUPS)], gsel) for j in range(EPG)]
    i1, _ = _first_max(v)
    i2, _ = _first_max([jnp.where(i1 == j, -jnp.inf, v[j]) for j in range(EPG)])
    w1 = _pick(sv, i1)
    w2 = _pick(sv, i2)
    den = w1 + w2
    e1 = gsel * EPG + i1
    e2 = gsel * EPG + i2
    w_ref[0:1, :] = w1 / den
    w_ref[1:2, :] = w2 / den

    eio = lax.broadcasted_iota(I32, (N_EXPERTS, tm), 0)
    oh1 = eio == e1
    oh2 = eio == e2
    cnt = oh1.astype(F32) + oh2.astype(F32)
    ti = lax.broadcasted_iota(I32, (tm, tm), 0)
    tj = lax.broadcasted_iota(I32, (tm, tm), 1)
    before = jnp.where(ti < tj, 1.0, 0.0).astype(BF16)
    pref = jnp.dot(cnt.astype(BF16), before, preferred_element_type=F32)
    plen = jnp.floor((jnp.sum(cnt, axis=1, keepdims=True) + (RUN - 1.0)) * (1.0 / RUN)) * RUN
    eio1 = lax.broadcasted_iota(I32, (N_EXPERTS, 1), 0)
    loff = jnp.zeros((N_EXPERTS, 1), F32)
    run_start = jnp.zeros((1, 1), F32)
    for ex in range(N_EXPERTS):
        loff = jnp.where(eio1 == ex, run_start, loff)
        run_start = run_start + plen[ex:ex + 1, :]
    start = loff + pref
    lp_ref[0:1, :] = jnp.sum(jnp.where(oh1, start, 0.0), axis=0, keepdims=True).astype(I32)
    lp_ref[1:2, :] = jnp.sum(jnp.where(oh2, start, 0.0), axis=0, keepdims=True).astype(I32)
    goff = base[...]
    lane = lax.broadcasted_iota(I32, (1, LANES), 1)
    tab_ref[...] = jnp.where(lane == 0, plen, jnp.where(lane == 1, goff, 0.0)).astype(I32)
    base[...] = goff + plen
    cnt_ref[...] = jnp.broadcast_to(goff + plen, (N_EXPERTS, LANES)).astype(I32)


def _merge(h, y0, y1, y2, rof, rob, rg, x, g1, gn, sh, sc, wts):
    n = x.shape[0]
    tm = _row_tile(n, MOE_TILE)
    row = lambda c: pl.BlockSpec((tm, c), lambda i: (i, 0))
    full = lambda a: pl.BlockSpec(a.shape, lambda i: (0,) * a.ndim)
    tok = pl.BlockSpec((2, tm), lambda i: (0, i))
    return pl.pallas_call(
        _merge_kernel,
        grid=(n // tm,),
        in_specs=[row(D), row(BW), row(BW), row(HEADS * LANES),
                  row(HEADS * DV), row(HEADS * DV), row(BW), row(D),
                  full(g1), full(gn), full(sh), full(sc)] + [full(a) for a in wts],
        out_specs=[row(D), row(D), tok, tok,
                   pl.BlockSpec((N_EXPERTS, LANES), lambda i: (i, 0)),
                   pl.BlockSpec((N_EXPERTS, LANES), lambda i: (0, 0))],
        out_shape=[jax.ShapeDtypeStruct((n, D), F32), jax.ShapeDtypeStruct((n, D), BF16),
                   jax.ShapeDtypeStruct((2, n), I32), jax.ShapeDtypeStruct((2, n), F32),
                   jax.ShapeDtypeStruct((n // tm * N_EXPERTS, LANES), I32),
                   jax.ShapeDtypeStruct((N_EXPERTS, LANES), I32)],
        scratch_shapes=[pltpu.VMEM((N_EXPERTS, 1), F32)],
        compiler_params=_cparams("arbitrary"),
    )(h, y0, y1, y2, rof, rob, rg, x, g1, gn, sh, sc, *wts)


def _for_run_pieces(tab_ref, ps_ref, fn):
    loff = 0
    for ex in range(N_EXPERTS):
        plen = tab_ref[ex, 0]
        gbase = ps_ref[ex] + tab_ref[ex, 1]

        def piece(c, carry, loff=loff, gbase=gbase):
            fn(pl.multiple_of(loff + c * RUN, RUN), pl.multiple_of(gbase + c * RUN, RUN))
            return carry

        lax.fori_loop(0, plen // RUN, piece, 0)
        loff = loff + plen


def _dispatch_kernel(ps_ref, tab_ref, lp_ref, x_ref, xs_in_ref, xs_ref, buf, sem):
    del xs_in_ref
    rmax, tm = buf.shape[0], x_ref.shape[0]
    lp = lp_ref[...]
    r = lax.broadcasted_iota(I32, (rmax, tm), 0)
    sel = jnp.where((r == lp[0:1, :]) | (r == lp[1:2, :]), 1.0, 0.0).astype(BF16)
    buf[...] = jnp.dot(sel, x_ref[...], preferred_element_type=F32).astype(BF16)

    def copy(s, t):
        return pltpu.make_async_copy(buf.at[pl.ds(s, RUN)], xs_ref.at[pl.ds(t, RUN)], sem)

    _for_run_pieces(tab_ref, ps_ref, lambda s, t: copy(s, t).start())
    _for_run_pieces(tab_ref, ps_ref, lambda s, t: copy(s, t).wait())


def _tile_tables(tm):
    return [pl.BlockSpec(memory_space=pltpu.SMEM),
            pl.BlockSpec((N_EXPERTS, LANES), lambda i: (i, 0), memory_space=pltpu.SMEM)]


def _dispatch(h2, lp, tab, pstart, cap, rmax):
    n = h2.shape[0]
    tm = _row_tile(n, MOE_TILE)
    return pl.pallas_call(
        _dispatch_kernel,
        grid=(n // tm,),
        in_specs=_tile_tables(tm) + [pl.BlockSpec((2, tm), lambda i: (0, i)),
                                     pl.BlockSpec((tm, D), lambda i: (i, 0)),
                                     pl.BlockSpec(memory_space=pl.ANY)],
        out_specs=pl.BlockSpec(memory_space=pl.ANY),
        out_shape=jax.ShapeDtypeStruct((cap, D), BF16),
        scratch_shapes=[pltpu.VMEM((rmax, D), BF16), pltpu.SemaphoreType.DMA(())],
        input_output_aliases={4: 0},
        compiler_params=_cparams("arbitrary"),
    )(pstart, tab, lp, h2, jnp.zeros((cap, D), BF16))


def _expert_kernel(be_ref, nu_ref, x_ref, w1_ref, w3_ref, w2_ref, y_ref):
    del be_ref
    b = pl.program_id(0)

    @pl.when(b < nu_ref[0])
    def _():
        x = x_ref[...]
        a = jnp.dot(x, w1_ref[0], preferred_element_type=F32)
        c = jnp.dot(x, w3_ref[0], preferred_element_type=F32)
        u = (a * jax.nn.sigmoid(a) * c).astype(BF16)
        y_ref[...] = jnp.dot(u, w2_ref[0], preferred_element_type=F32).astype(BF16)

    @pl.when(b >= nu_ref[0])
    def _():
        y_ref[...] = jnp.zeros_like(y_ref)


def _experts(xs, blk_e, nused, w1, w3, w2):
    cap = xs.shape[0]
    nb = cap // MOE_BLOCK
    xmap = lambda b, be, nu: (jnp.minimum(b, jnp.maximum(nu[0] - 1, 0)), 0)
    wmap = lambda b, be, nu: (be[b], 0, 0)
    return pl.pallas_call(
        _expert_kernel,
        grid_spec=pltpu.PrefetchScalarGridSpec(
            num_scalar_prefetch=2,
            grid=(nb,),
            in_specs=[pl.BlockSpec((MOE_BLOCK, D), xmap),
                      pl.BlockSpec((1, D, D_FF), wmap),
                      pl.BlockSpec((1, D, D_FF), wmap),
                      pl.BlockSpec((1, D_FF, D), wmap)],
            out_specs=pl.BlockSpec((MOE_BLOCK, D), lambda b, be, nu: (b, 0))),
        out_shape=jax.ShapeDtypeStruct((cap, D), BF16),
        compiler_params=_cparams("arbitrary"),
    )(blk_e, nused, xs, w1, w3, w2)


def _combine_kernel(ps_ref, tab_ref, lp_ref, w_ref, x_ref, g2_ref, gf_ref, ys_ref, o_ref, buf, sem, *,
                    final_norm):
    rmax, tm = buf.shape[0], x_ref.shape[0]

    @pl.when(pl.program_id(0) == 0)
    def _():
        buf[...] = jnp.zeros_like(buf)

    def copy(s, t):
        return pltpu.make_async_copy(ys_ref.at[pl.ds(t, RUN)], buf.at[pl.ds(s, RUN)], sem)

    _for_run_pieces(tab_ref, ps_ref, lambda s, t: copy(s, t).start())
    _for_run_pieces(tab_ref, ps_ref, lambda s, t: copy(s, t).wait())
    lp = lp_ref[...]
    w = w_ref[...]
    r = lax.broadcasted_iota(I32, (tm, rmax), 1)
    mix = jnp.where(r == lp[:, 0:1], w[:, 0:1], 0.0) + jnp.where(r == lp[:, 1:2], w[:, 1:2], 0.0)
    ml = jnp.dot(mix.astype(BF16), buf[...], preferred_element_type=F32)
    xo = x_ref[...] + g2_ref[...] * ml
    if final_norm:
        xo = xo * lax.rsqrt(jnp.mean(xo * xo, axis=-1, keepdims=True) + EPS) * gf_ref[...]
    o_ref[...] = xo


def _combine(x, g2, lpt, wtok, tab, pstart, ys, gf, final_norm, rmax):
    n = x.shape[0]
    tm = _row_tile(n, MOE_TILE)
    vec = pl.BlockSpec((1, D), lambda i: (0, 0))
    tok = pl.BlockSpec((tm, 2), lambda i: (i, 0))
    return pl.pallas_call(
        functools.partial(_combine_kernel, final_norm=final_norm),
        grid=(n // tm,),
        in_specs=_tile_tables(tm) + [tok, tok, pl.BlockSpec((tm, D), lambda i: (i, 0)), vec, vec,
                                     pl.BlockSpec(memory_space=pl.ANY)],
        out_specs=pl.BlockSpec((tm, D), lambda i: (i, 0)),
        out_shape=jax.ShapeDtypeStruct((n, D), F32),
        scratch_shapes=[pltpu.VMEM((rmax, D), BF16), pltpu.SemaphoreType.DMA(())],
        compiler_params=_cparams("arbitrary"),
    )(pstart, tab, lpt, wtok, x, g2, gf, ys)


def _moe(x, h2, lp, wsel, tab, filled, g2, w1, w3, w2, gf, final_norm):
    n = x.shape[0]
    tm = _row_tile(n, MOE_TILE)
    rmax = 2 * tm + N_EXPERTS * RUN
    worst = n // tm * (2 * tm + N_EXPERTS * (RUN - 1)) + N_EXPERTS * (MOE_BLOCK - RUN)
    cap = (worst + MOE_BLOCK - 1) // MOE_BLOCK * MOE_BLOCK
    nb = cap // MOE_BLOCK
    region = (filled + MOE_BLOCK - 1) // MOE_BLOCK * MOE_BLOCK
    pend = jnp.cumsum(region)
    pstart = (pend - region).astype(I32)
    blk_start = jnp.arange(nb, dtype=I32) * MOE_BLOCK
    blk_e = jnp.minimum(jnp.sum(pend[None, :] <= blk_start[:, None], axis=-1), N_EXPERTS - 1).astype(I32)
    nused = (pend[-1:] // MOE_BLOCK).astype(I32)
    xs = _dispatch(h2, lp, tab, pstart, cap, rmax)
    ys = _experts(xs, blk_e, nused, w1, w3, w2)
    return _combine(x, g2, lp.T, wsel.T, tab, pstart, ys, gf, final_norm, rmax)


def _rope_tables(n):
    rows = n // GRID_W
    row = jnp.repeat(jnp.arange(rows, dtype=F32), GRID_W)
    col = jnp.tile(jnp.arange(GRID_W, dtype=F32), rows)
    nf = DK // 4
    inv = ROPE_BASE ** (-jnp.arange(nf, dtype=F32) / nf)
    ang = jnp.concatenate([row[:, None] * inv, col[:, None] * inv], axis=-1)
    cos, sin = jnp.cos(ang), jnp.sin(ang)
    reps = LANES // DK
    return (jnp.tile(jnp.concatenate([cos, cos], -1), (1, reps)),
            jnp.tile(jnp.concatenate([-sin, sin], -1), (1, reps)))


def _layer_weights(l, w_in, w_gate, b_gate, w_branch, w_o, w_router, b_router):
    wi = w_in[l]
    wv = wi[:, 1792:2048].reshape(D, HEADS, DV)
    wv = jnp.concatenate([wv, jnp.zeros((D, HEADS, LANES - DV), F32)], axis=-1).reshape(D, HEADS * LANES)
    w = jnp.concatenate([wi[:, :1792], wv, wi[:, 2048:]], axis=-1).astype(BF16)
    wb = w_branch[l]
    wb2 = wb[2].reshape(HEADS, DV, D)
    wb2 = jnp.concatenate([wb2, jnp.zeros((HEADS, LANES - DV, D), F32)], axis=1).reshape(HEADS * LANES, D)
    merge_w = (w_gate[l].astype(BF16), b_gate[l].reshape(1, 4 * D), wb[0].astype(BF16), wb[1].astype(BF16),
               wb2.astype(BF16), wb[3].astype(BF16), w_o[l].astype(BF16),
               w_router.T, b_router.reshape(N_EXPERTS, 1))
    return w, merge_w


def kernel(x, c, ctx, c_ctx, w_mod, b_mod, g_norm1, g_norm2, w_in, conv_a_w, conv_a_b, conv_a_g, conv_a_beta, conv_b_w, lam_q1, lam_k1, lam_q2, lam_k2, diff_g, ret_ld_f, ret_ld_b, w_gate, b_gate, w_branch, w_o, w_router, b_router, w1_e, w3_e, w2_e, g_final):
    assert x.shape[0] == 1 and ctx.shape[0] == 1
    xl, xc = x[0], ctx[0]
    n_lat, n_ctx = xl.shape[0], xc.shape[0]
    mods = _adaln(c, c_ctx, w_mod, b_mod)
    cos, sinp = _rope_tables(n_lat)
    zc = jnp.zeros((n_ctx, LANES), F32)
    vone = jnp.zeros((HEADS, LANES), F32).at[:, DV].set(1.0).reshape(1, HEADS * LANES)
    gf = g_final.reshape(1, D)
    vrow = lambda a: a.reshape(1, -1)

    for l in range(DEPTH):
        last = l == DEPTH - 1
        lam_init = 0.8 - 0.6 * math.exp(-0.3 * l)
        ml = [mods[l, 0:1, j * D:(j + 1) * D] for j in range(6)]
        mc = [mods[l, 1:2, j * D:(j + 1) * D] for j in range(6)]
        w, merge_w = _layer_weights(l, w_in, w_gate, b_gate, w_branch, w_o, w_router, b_router)
        g1 = vrow(g_norm1[l])
        g2n = vrow(g_norm2[l])
        lam = (jnp.exp(jnp.sum(lam_q1[l] * lam_k1[l])) - jnp.exp(jnp.sum(lam_q2[l] * lam_k2[l]))
               + lam_init).reshape(1, 1).astype(F32)
        gpad = jnp.concatenate([diff_g[l], jnp.zeros((LANES - DV,), F32)]).reshape(1, LANES)
        ld = jnp.stack([ret_ld_f[l], ret_ld_b[l]]).astype(F32)
        conv_w = (conv_a_w[l], vrow(conv_a_b[l]), vrow(conv_a_g[l]), vrow(conv_a_beta[l]), conv_b_w[l])
        w1, w3, w2 = w1_e[l].astype(BF16), w3_e[l].astype(BF16), w2_e[l].astype(BF16)

        (hc, uc, bgc, cxc, qc, kc, vc, rqc, rkc, rvc, rgc) = _inproj(
            xc, g1, mc[0], mc[1], w, zc, zc, vone, rope=False)
        (hl, ul, bgl, cxl, ql, kl, vl, rql, rkl, rvl, rgl) = _inproj(
            xl, g1, ml[0], ml[1], w, cos, sinp, vone, rope=True)

        kall = jnp.concatenate([kc, kl], axis=0)
        vall = jnp.concatenate([vc, vl], axis=0)
        out_scale = 1.0 - lam_init
        y2l = _attention(ql, kall, vall, lam, gpad, out_scale)
        s0 = jnp.zeros((2, HEADS * DK, HEADS * DV), F32)
        rofc, robc, sfin = _retention(rqc, rkc, rvc, ld, s0)
        rofl, robl, _ = _retention(rql, rkl, rvl, ld, sfin)
        y0l, y1l = _convs(ul, cxl, bgl, *conv_w)
        xl, h2l, lpl, wl, tabl, cl = _merge(hl, y0l, y1l, y2l, rofl, robl, rgl, xl, ml[2], g2n, ml[3], ml[4], merge_w)
        xl = _moe(xl, h2l, lpl, wl, tabl, cl[:, 0], ml[5], w1, w3, w2, gf, last)
        if not last:
            y2c = _attention(qc, kc, vc, lam, gpad, out_scale)
            y0c, y1c = _convs(uc, cxc, bgc, *conv_w)
            xc, h2c, lpc, wc, tabc, cc = _merge(hc, y0c, y1c, y2c, rofc, robc, rgc, xc, mc[2], g2n, mc[3], mc[4], merge_w)
            xc = _moe(xc, h2c, lpc, wc, tabc, cc[:, 0], mc[5], w1, w3, w2, gf, False)
    return xl[None]
```

```python
import functools
import math

import numpy as np
import jax
import jax.numpy as jnp
from jax import lax
from jax.experimental import pallas as pl
from jax.experimental.pallas import tpu as pltpu

F32 = jnp.float32
BF16 = jnp.bfloat16
I32 = jnp.int32
HI = lax.Precision.HIGHEST

D = 1024
DEPTH = 2
GRID_W = 64
BW = 256
CONF_K = 31
SCONV_K = 3
HEADS = 4
DK = 32
DV = 64
CHUNK = 128
ROPE_BASE = 10000.0
N_EXPERTS = 16
N_GROUPS = 4
EPG = N_EXPERTS // N_GROUPS
D_FF = 512
MOE_BLOCK = 256
EPS = 1e-6
LOG2E = math.log2(math.e)
BOUND_SLACK = 1.0 + 2.0 ** -10
L_MIN = 2.0 ** -64
RET_ROWS = 256
GATE_COLS = 512
MOE_TILE = 512
RUN = 16
HALO = 16
LANES = 128
SUBLANES = 8
W_COLS = 2816
VMEM_LIMIT = 56 * 1024 * 1024


def _cparams(*sem):
    return pltpu.CompilerParams(dimension_semantics=sem, vmem_limit_bytes=VMEM_LIMIT)


def _row_tile(n, pref):
    return pref if n % pref == 0 else n


def _mod_kernel(c_ref, w_ref, b_ref, o_ref):
    a = c_ref[...]
    a = a * jax.nn.sigmoid(a)
    o_ref[0] = jnp.dot(a, w_ref[0], preferred_element_type=F32, precision=HI) + b_ref[0]


def _adaln(c, c_ctx, w_mod, b_mod):
    cs = jnp.zeros((8, D), F32).at[0].set(c[0]).at[1].set(c_ctx)
    return pl.pallas_call(
        _mod_kernel,
        grid=(DEPTH, 6),
        in_specs=[pl.BlockSpec((8, D), lambda l, j: (0, 0)),
                  pl.BlockSpec((1, D, D), lambda l, j: (l, 0, j)),
                  pl.BlockSpec((1, 1, D), lambda l, j: (l, 0, j))],
        out_specs=pl.BlockSpec((1, 8, D), lambda l, j: (l, 0, j)),
        out_shape=jax.ShapeDtypeStruct((DEPTH, 8, 6 * D), F32),
        compiler_params=_cparams("parallel", "parallel"),
    )(cs, w_mod, b_mod.reshape(DEPTH, 1, 6 * D))


def _rope(x, cos, sinp):
    lane = lax.broadcasted_iota(I32, (1, LANES), 1)
    first = (lane % 32) < 16
    outs = []
    for c in range(x.shape[1] // LANES):
        xc = x[:, c * LANES:(c + 1) * LANES]
        sw = jnp.where(first, pltpu.roll(xc, LANES - 16, 1), pltpu.roll(xc, 16, 1))
        outs.append(xc * cos + sw * sinp)
    return outs[0] if len(outs) == 1 else jnp.concatenate(outs, axis=-1)


def _inproj_kernel(x_ref, g_ref, sh_ref, sc_ref, w_ref, cos_ref, sin_ref, vone_ref,
                   h_ref, u_ref, bg_ref, cx_ref, q_ref, k_ref, v_ref,
                   rq_ref, rk_ref, rv_ref, rg_ref, *, rope):
    x = x_ref[...]
    y = x * lax.rsqrt(jnp.mean(x * x, axis=-1, keepdims=True) + EPS) * g_ref[...]
    hb = (y * (1.0 + sc_ref[...]) + sh_ref[...]).astype(BF16)
    h_ref[...] = hb

    def seg(a, b):
        return jnp.dot(hb, w_ref[:, a:b], preferred_element_type=F32)

    z = seg(0, 512)
    u_ref[...] = z[:, :BW] * jax.nn.sigmoid(z[:, BW:])
    z = seg(512, 1280)
    bg_ref[...] = z[:, :BW]
    cx_ref[...] = z[:, BW:2 * BW] * z[:, 2 * BW:]
    zq = seg(1280, 1536)
    zk = seg(1536, 1792)
    zrq = seg(2048, 2176)
    zrk = seg(2176, 2304)
    if rope:
        cos = cos_ref[...]
        sinp = sin_ref[...]
        zq, zk = _rope(zq, cos, sinp), _rope(zk, cos, sinp)
        zrq, zrk = _rope(zrq, cos, sinp), _rope(zrk, cos, sinp)
    q_ref[...] = (zq * (DK ** -0.5 * LOG2E)).astype(BF16)
    k_ref[...] = zk.astype(BF16)
    vr = lax.broadcasted_iota(I32, (HEADS * DV, HEADS * LANES), 0)
    vc = lax.broadcasted_iota(I32, (HEADS * DV, HEADS * LANES), 1)
    spread = jnp.where((vc // LANES == vr // DV) & (vc % LANES == vr % DV), 1.0, 0.0).astype(BF16)
    zv = seg(1792, 2048).astype(BF16)
    v_ref[...] = (jnp.dot(zv, spread, preferred_element_type=F32) + vone_ref[...]).astype(BF16)
    rq_ref[...] = zrq
    rk_ref[...] = zrk * (DK ** -0.5)
    rv_ref[...] = seg(2304, 2560)
    rg_ref[...] = seg(2560, 2816)


def _inproj(x, g, shift, scale, w, cos, sinp, vone, rope):
    n = x.shape[0]
    tm = _row_tile(n, 512)
    row = lambda c: pl.BlockSpec((tm, c), lambda i: (i, 0))
    vec = lambda c: pl.BlockSpec((1, c), lambda i: (0, 0))
    widths = [(D, BF16), (BW, F32), (BW, F32), (BW, F32), (256, BF16), (256, BF16), (512, BF16),
              (128, F32), (128, F32), (256, F32), (256, F32)]
    return pl.pallas_call(
        functools.partial(_inproj_kernel, rope=rope),
        grid=(n // tm,),
        in_specs=[row(D), vec(D), vec(D), vec(D),
                  pl.BlockSpec((D, W_COLS), lambda i: (0, 0)),
                  row(LANES), row(LANES), vec(512)],
        out_specs=[row(c) for c, _ in widths],
        out_shape=[jax.ShapeDtypeStruct((n, c), dt) for c, dt in widths],
        compiler_params=_cparams("parallel"),
    )(x, g, shift, scale, w, cos, sinp, vone)


def _conv_kernel(up_ref, um_ref, un_ref, cp_ref, cm_ref, cn_ref, bg_ref,
                 wa_ref, ba_ref, ga_ref, bta_ref, wb_ref, y0_ref, y1_ref, eu, ec):
    i = pl.program_id(0)
    last = pl.num_programs(0) - 1
    tm = um_ref.shape[0]
    pad_a = (CONF_K - 1) // 2
    pad_b = (SCONV_K - 1) // 2
    span = tm + 2 * HALO - SUBLANES
    for ext, p_ref, m_ref, n_ref, shifts in ((eu, up_ref, um_ref, un_ref, range(1, SUBLANES)),
                                             (ec, cp_ref, cm_ref, cn_ref, (1, SUBLANES - 1))):
        ext[0, 0:HALO, :] = jnp.where(i > 0, p_ref[...], 0.0)
        ext[0, HALO:HALO + tm, :] = m_ref[...]
        ext[0, HALO + tm:, :] = jnp.where(i < last, n_ref[...], 0.0)
        for s in shifts:
            ext[s, 0:span, :] = ext[0, pl.ds(s, span), :]

    def tap(ext, r0, off):
        return ext[off % SUBLANES, pl.ds(r0 + off - off % SUBLANES, rc), :]

    rc = 32
    for r0 in range(0, tm, rc):
        acc = jnp.zeros((rc, BW), F32)
        for k in range(CONF_K):
            acc = acc + tap(eu, r0, HALO + k - pad_a) * wa_ref[k:k + 1, :]
        acc = acc + ba_ref[...]
        mu = jnp.mean(acc, axis=-1, keepdims=True)
        xc = acc - mu
        var = jnp.mean(xc * xc, axis=-1, keepdims=True)
        yn = xc * lax.rsqrt(var + EPS) * ga_ref[...] + bta_ref[...]
        y0_ref[pl.ds(r0, rc), :] = (yn * jax.nn.sigmoid(yn)).astype(BF16)
        accb = jnp.zeros((rc, BW), F32)
        for k in range(SCONV_K):
            accb = accb + tap(ec, r0, HALO + k - pad_b) * wb_ref[k:k + 1, :]
        y1_ref[pl.ds(r0, rc), :] = (bg_ref[pl.ds(r0, rc), :] * accb).astype(BF16)


def _convs(u, cx, bg, wa, ba, ga, bta, wb):
    n = u.shape[0]
    tm = _row_tile(n, 256)
    hb = tm // HALO
    nh = n // HALO
    prev = pl.BlockSpec((HALO, BW), lambda i: (jnp.maximum(i * hb - 1, 0), 0))
    main = pl.BlockSpec((tm, BW), lambda i: (i, 0))
    nxt = pl.BlockSpec((HALO, BW), lambda i: (jnp.minimum((i + 1) * hb, nh - 1), 0))
    vec = lambda r: pl.BlockSpec((r, BW), lambda i: (0, 0))
    return pl.pallas_call(
        _conv_kernel,
        grid=(n // tm,),
        in_specs=[prev, main, nxt, prev, main, nxt, main,
                  vec(CONF_K), vec(1), vec(1), vec(1), vec(SCONV_K)],
        out_specs=[main, main],
        out_shape=[jax.ShapeDtypeStruct((n, BW), BF16)] * 2,
        scratch_shapes=[pltpu.VMEM((SUBLANES, tm + 2 * HALO, BW), F32)] * 2,
        compiler_params=_cparams("parallel"),
    )(u, u, u, cx, cx, cx, bg, wa, ba, ga, bta, wb)


def _attn_kernel(lam_ref, q_ref, k_ref, v_ref, g_ref, o_ref, kmax_sc, acc_sc, *, tk, nk, out_scale):
    h = pl.program_id(0)
    kc = HEADS * 2 * DK

    @pl.when((h == 0) & (pl.program_id(1) == 0))
    def _():
        def colmax(j, mx):
            blk = k_ref[pl.ds(pl.multiple_of(j * kc, kc), kc), :].astype(F32)
            return jnp.maximum(mx, jnp.max(jnp.abs(blk), axis=0, keepdims=True))
        kmax_sc[...] = lax.fori_loop(0, nk * tk // kc, colmax, jnp.zeros((1, kc), F32))

    q = q_ref[...]
    tq = q.shape[0]
    lane_q = lax.broadcasted_iota(I32, (1, kc), 1)
    lane_o = lax.broadcasted_iota(I32, (1, LANES), 1)
    qbound = jnp.abs(q.astype(F32)) * kmax_sc[...]
    qms, shifts = [], []
    for m in range(2):
        lo = (h * 2 + m) * DK
        sel = (lane_q >= lo) & (lane_q < lo + DK)
        qms.append(jnp.where(sel, q, jnp.zeros_like(q)))
        ub = jnp.sum(jnp.where(sel, qbound, 0.0), axis=-1, keepdims=True) * BOUND_SLACK
        shifts.append(jnp.tile(jnp.broadcast_to(ub, (tq, LANES)), (1, tk // LANES)))

    def tiles(j):
        start = pl.multiple_of(j * tk, tk)
        return k_ref[pl.ds(start, tk), :], v_ref[pl.ds(start, tk), :]

    def scores(m, kj):
        return lax.dot_general(qms[m], kj, (((1,), (1,)), ((), ())), preferred_element_type=F32)

    acc_sc[...] = jnp.zeros(acc_sc.shape, F32)

    def shifted(j, carry):
        kj, vj = tiles(j)
        for m in range(2):
            p = jnp.exp2(scores(m, kj) - shifts[m]).astype(BF16)
            acc_sc[m] += jnp.dot(p, vj, preferred_element_type=F32)
        return carry

    lax.fori_loop(0, nk, shifted, 0)
    lmin = jnp.minimum(jnp.min(acc_sc[0][:, DV:DV + 1]), jnp.min(acc_sc[1][:, DV:DV + 1]))

    @pl.when(jnp.logical_not(lmin >= L_MIN))
    def _():
        def online(j, carry):
            kj, vj = tiles(j)
            new = []
            for m in range(2):
                mx, acc = carry[m]
                s = scores(m, kj)
                mnew = jnp.maximum(mx, jnp.max(s, axis=-1, keepdims=True))
                p = jnp.exp2(s - mnew).astype(BF16)
                new.append((mnew, jnp.exp2(mx - mnew) * acc + jnp.dot(p, vj, preferred_element_type=F32)))
            return tuple(new)

        init = (jnp.full((tq, 1), -jnp.inf, F32), jnp.zeros((tq, LANES), F32))
        res = lax.fori_loop(0, nk, online, (init, init))
        for m in range(2):
            acc_sc[m] = res[m][1]

    outs = [acc_sc[m] / acc_sc[m][:, DV:DV + 1] for m in range(2)]
    o = outs[0] - lam_ref[0, 0] * outs[1]
    valid = lane_o < DV
    ms = jnp.sum(jnp.where(valid, o * o, 0.0), axis=-1, keepdims=True) * (1.0 / DV)
    y = o * lax.rsqrt(ms + EPS) * g_ref[...] * out_scale
    o_ref[...] = jnp.where(valid, y, 0.0).astype(BF16)


def _attention(q, k, v, lam, gpad, out_scale):
    n = q.shape[0]
    nkeys = k.shape[0]
    tq = _row_tile(n, 512)
    tk = next(t for t in (3328, 1280, 256) if nkeys % t == 0)
    kern = functools.partial(_attn_kernel, tk=tk, nk=nkeys // tk, out_scale=out_scale)
    return pl.pallas_call(
        kern,
        grid=(HEADS, n // tq),
        in_specs=[pl.BlockSpec(memory_space=pltpu.SMEM),
                  pl.BlockSpec((tq, 256), lambda h, i: (i, 0)),
                  pl.BlockSpec((nkeys, 256), lambda h, i: (0, 0)),
                  pl.BlockSpec((nkeys, LANES), lambda h, i: (0, h)),
                  pl.BlockSpec((1, LANES), lambda h, i: (0, 0))],
        out_specs=pl.BlockSpec((tq, LANES), lambda h, i: (i, h)),
        out_shape=jax.ShapeDtypeStruct((n, HEADS * LANES), BF16),
        scratch_shapes=[pltpu.VMEM((1, HEADS * 2 * DK), F32), pltpu.VMEM((2, tq, LANES), F32)],
        compiler_params=_cparams("arbitrary", "arbitrary"),
    )(lam, q, k, v, gpad)


def _ret_kernel(ld_ref, qf_ref, kf_ref, vf_ref, qb_ref, kb_ref, vb_ref, s0_ref, of_ref, ob_ref, sf_ref,
                state, dmask, qdec, kdec, cdec):
    n = pl.program_id(0)
    c = CHUNK
    lane_q = lax.broadcasted_iota(I32, (1, HEADS * DK), 1)
    lane_v = lax.broadcasted_iota(I32, (1, HEADS * DV), 1)

    @pl.when(n == 0)
    def _():
        state[...] = s0_ref[...]
        row_h = lax.broadcasted_iota(I32, (HEADS * DK, 1), 0) // DK
        for d in range(2):
            pi = lax.broadcasted_iota(I32, (c, 1), 0).astype(F32)
            pj = lax.broadcasted_iota(I32, (1, c), 1).astype(F32)
            if d == 1:
                pi, pj = c - 1.0 - pi, c - 1.0 - pj
            diff = pi - pj
            lg_q = jnp.zeros((1, HEADS * DK), F32)
            lg_r = jnp.zeros((HEADS * DK, 1), F32)
            for h in range(HEADS):
                lg = ld_ref[d, h]
                dmask[d, h] = jnp.exp(jnp.where(diff >= 0, diff * lg, -jnp.inf))
                lg_q = jnp.where(lane_q // DK == h, lg, lg_q)
                lg_r = jnp.where(row_h == h, lg, lg_r)
            qdec[d] = jnp.exp((pi + 1.0) * lg_q)
            kdec[d] = jnp.exp((c - 1.0 - pi) * lg_q)
            cdec[d] = jnp.where(row_h == lane_v // DV, jnp.exp(c * lg_r), 0.0)

    for d, (q_ref, k_ref, v_ref, o_ref) in enumerate(((qf_ref, kf_ref, vf_ref, of_ref),
                                                      (qb_ref, kb_ref, vb_ref, ob_ref))):
        nsub = q_ref.shape[0] // c
        cd = cdec[d]
        s_cur = state[d]
        for sub in (range(nsub) if d == 0 else reversed(range(nsub))):
            rows = pl.ds(sub * c, c)
            q = q_ref[rows, :]
            k = k_ref[rows, :]
            kb = k.astype(BF16)
            vb = v_ref[rows, :].astype(BF16)
            o = jnp.dot((q * qdec[d]).astype(BF16), s_cur.astype(BF16), preferred_element_type=F32)
            for h in range(HEADS):
                qm = jnp.where(lane_q // DK == h, q, 0.0).astype(BF16)
                sc = lax.dot_general(qm, kb, (((1,), (1,)), ((), ())),
                                     preferred_element_type=F32) * dmask[d, h]
                oh = jnp.dot(sc.astype(BF16), vb, preferred_element_type=F32)
                o = o + jnp.where(lane_v // DV == h, oh, 0.0)
            o_ref[rows, :] = o
            kv = lax.dot_general((k * kdec[d]).astype(BF16), vb, (((0,), (0,)), ((), ())),
                                 preferred_element_type=F32)
            s_cur = cd * s_cur + jnp.where(cd != 0.0, kv, 0.0)
        state[d] = s_cur

    @pl.when(n == pl.num_programs(0) - 1)
    def _():
        sf_ref[...] = state[...]


def _retention(rq, rk, rv, ld, s0):
    n = rq.shape[0]
    rows = _row_tile(n, RET_ROWS)
    nc = n // rows
    fwd = lambda w: pl.BlockSpec((rows, w), lambda i: (i, 0))
    bwd = lambda w: pl.BlockSpec((rows, w), lambda i: (nc - 1 - i, 0))
    sshape = (2, HEADS * DK, HEADS * DV)
    whole = pl.BlockSpec(sshape, lambda i: (0, 0, 0))
    dq, dv = HEADS * DK, HEADS * DV
    return pl.pallas_call(
        _ret_kernel,
        grid=(nc,),
        in_specs=[pl.BlockSpec(memory_space=pltpu.SMEM),
                  fwd(dq), fwd(dq), fwd(dv), bwd(dq), bwd(dq), bwd(dv), whole],
        out_specs=[fwd(dv), bwd(dv), whole],
        out_shape=[jax.ShapeDtypeStruct((n, dv), F32), jax.ShapeDtypeStruct((n, dv), F32),
                   jax.ShapeDtypeStruct(sshape, F32)],
        scratch_shapes=[pltpu.VMEM(sshape, F32),
                        pltpu.VMEM((2, HEADS, CHUNK, CHUNK), F32),
                        pltpu.VMEM((2, CHUNK, dq), F32),
                        pltpu.VMEM((2, CHUNK, dq), F32),
                        pltpu.VMEM(sshape, F32)],
        compiler_params=_cparams("arbitrary"),
    )(ld, rq, rk, rv, rq, rk, rv, s0)


def _first_max(vals):
    idx = jnp.zeros(vals[0].shape, I32)
    best = vals[0]
    for j in range(1, len(vals)):
        upd = vals[j] > best
        idx = jnp.where(upd, j, idx)
        best = jnp.where(upd, vals[j], best)
    return idx, best


def _split_bf16(a):
    hi = a.astype(BF16)
    return hi, (a - hi.astype(F32)).astype(BF16)


def _pick(vals, idx):
    out = vals[-1]
    for j in range(len(vals) - 2, -1, -1):
        out = jnp.where(idx == j, vals[j], out)
    return out


def _merge_kernel(h_ref, y0_ref, y1_ref, y2_ref, rof_ref, rob_ref, rg_ref, x_ref, g1_ref, gn_ref, sh_ref, sc_ref,
                  wg_ref, bgate_ref, wb0_ref, wb1_ref, wb2_ref, wb3_ref, wo_ref, wr_ref, br_ref,
                  xo_ref, h2_ref, lp_ref, w_ref, tab_ref, cnt_ref, base):
    i = pl.program_id(0)
    tm = x_ref.shape[0]

    @pl.when(i == 0)
    def _():
        base[...] = jnp.zeros_like(base)

    ro = rof_ref[...] + rob_ref[...]
    gr = lax.broadcasted_iota(I32, (HEADS * DV, HEADS * DV), 0) // DV
    gc = lax.broadcasted_iota(I32, (HEADS * DV, HEADS * DV), 1) // DV
    avg = jnp.where(gr == gc, 1.0 / DV, 0.0).astype(BF16)

    def head_mean(a):
        hi, lo = _split_bf16(a)
        return (jnp.dot(hi, avg, preferred_element_type=F32) + jnp.dot(lo, avg, preferred_element_type=F32))

    mu = head_mean(ro)
    xc = ro - mu
    var = head_mean(xc * xc)
    rg = rg_ref[...]
    y3 = (rg * jax.nn.sigmoid(rg) * (xc * lax.rsqrt(var + EPS))).astype(BF16)

    hb = h_ref[...]
    ys = (y0_ref[...], y1_ref[...], y2_ref[...], y3)
    wbs = (wb0_ref, wb1_ref, wb2_ref, wb3_ref)
    parts = []
    for c0 in range(0, D, GATE_COLS):
        mc = None
        for b in range(4):
            lo = b * D + c0
            gl = jnp.dot(hb, wg_ref[:, lo:lo + GATE_COLS], preferred_element_type=F32)
            gate = jax.nn.sigmoid(gl + bgate_ref[:, lo:lo + GATE_COLS])
            t = gate * jnp.dot(ys[b], wbs[b][:, c0:c0 + GATE_COLS], preferred_element_type=F32)
            mc = t if mc is None else mc + t
        parts.append(mc.astype(BF16))
    yo = jnp.dot(jnp.concatenate(parts, axis=-1), wo_ref[...], preferred_element_type=F32)
    xn = x_ref[...] + g1_ref[...] * yo
    xo_ref[...] = xn
    yn = xn * lax.rsqrt(jnp.mean(xn * xn, axis=-1, keepdims=True) + EPS) * gn_ref[...]
    h2 = yn * (1.0 + sc_ref[...]) + sh_ref[...]
    h2_ref[...] = h2.astype(BF16)

    h_hi, h_lo = _split_bf16(h2)
    w_hi, w_lo = _split_bf16(wr_ref[...])
    nt = (((1,), (1,)), ((), ()))
    l2 = lax.dot_general(jnp.concatenate([w_hi, w_lo], axis=0), h_hi, nt, preferred_element_type=F32)
    lt = (l2[:N_EXPERTS] + l2[N_EXPERTS:]) + lax.dot_general(w_hi, h_lo, nt, preferred_element_type=F32)
    s = jax.nn.sigmoid(lt)
    sb = s + br_ref[...]
    r = [sb[e:e + 1, :] for e in range(N_EXPERTS)]
    sr = [s[e:e + 1, :] for e in range(N_EXPERTS)]
    gscore = []
    for g in range(N_GROUPS):
        a, b, c, d = r[EPG * g:EPG * (g + 1)]
        hi1, lo1, hi2, lo2 = jnp.maximum(a, b), jnp.minimum(a, b), jnp.maximum(c, d), jnp.minimum(c, d)
        gscore.append(jnp.maximum(hi1, hi2) + jnp.maximum(jnp.minimum(hi1, hi2), jnp.maximum(lo1, lo2)))
    gsel, _ = _first_max(gscore)
    v = [_pick([r[EPG * g + j] for g in range(N_GROUPS)], gsel) for j in range(EPG)]
    sv = [_pick([sr[EPG * g + j] for g in range(N_GROUPS)], gsel) for j in range(EPG)]
    i1, _ = _first_max(v)
    i2, _ = _first_max([jnp.where(i1 == j, -jnp.inf, v[j]) for j in range(EPG)])
    w1 = _pick(sv, i1)
    w2 = _pick(sv, i2)
    den = w1 + w2
    e1 = gsel * EPG + i1
    e2 = gsel * EPG + i2
    w_ref[0:1, :] = w1 / den
    w_ref[1:2, :] = w2 / den

    eio = lax.broadcasted_iota(I32, (N_EXPERTS, tm), 0)
    oh1 = eio == e1
    oh2 = eio == e2
    cnt = oh1.astype(F32) + oh2.astype(F32)
    ti = lax.broadcasted_iota(I32, (tm, tm), 0)
    tj = lax.broadcasted_iota(I32, (tm, tm), 1)
    before = jnp.where(ti < tj, 1.0, 0.0).astype(BF16)
    pref = jnp.dot(cnt.astype(BF16), before, preferred_element_type=F32)
    plen = jnp.floor((jnp.sum(cnt, axis=1, keepdims=True) + (RUN - 1.0)) * (1.0 / RUN)) * RUN
    eio1 = lax.broadcasted_iota(I32, (N_EXPERTS, 1), 0)
    loff = jnp.zeros((N_EXPERTS, 1), F32)
    run_start = jnp.zeros((1, 1), F32)
    for ex in range(N_EXPERTS):
        loff = jnp.where(eio1 == ex, run_start, loff)
        run_start = run_start + plen[ex:ex + 1, :]
    start = loff + pref
    lp_ref[0:1, :] = jnp.sum(jnp.where(oh1, start, 0.0), axis=0, keepdims=True).astype(I32)
    lp_ref[1:2, :] = jnp.sum(jnp.where(oh2, start, 0.0), axis=0, keepdims=True).astype(I32)
    goff = base[...]
    lane = lax.broadcasted_iota(I32, (1, LANES), 1)
    tab_ref[...] = jnp.where(lane == 0, plen, jnp.where(lane == 1, goff, 0.0)).astype(I32)
    base[...] = goff + plen
    cnt_ref[...] = jnp.broadcast_to(goff + plen, (N_EXPERTS, LANES)).astype(I32)


def _merge(h, y0, y1, y2, rof, rob, rg, x, g1, gn, sh, sc, wts):
    n = x.shape[0]
    tm = _row_tile(n, MOE_TILE)
    row = lambda c: pl.BlockSpec((tm, c), lambda i: (i, 0))
    full = lambda a: pl.BlockSpec(a.shape, lambda i: (0,) * a.ndim)
    tok = pl.BlockSpec((2, tm), lambda i: (0, i))
    return pl.pallas_call(
        _merge_kernel,
        grid=(n // tm,),
        in_specs=[row(D), row(BW), row(BW), row(HEADS * LANES),
                  row(HEADS * DV), row(HEADS * DV), row(BW), row(D),
                  full(g1), full(gn), full(sh), full(sc)] + [full(a) for a in wts],
        out_specs=[row(D), row(D), tok, tok,
                   pl.BlockSpec((N_EXPERTS, LANES), lambda i: (i, 0)),
                   pl.BlockSpec((N_EXPERTS, LANES), lambda i: (0, 0))],
        out_shape=[jax.ShapeDtypeStruct((n, D), F32), jax.ShapeDtypeStruct((n, D), BF16),
                   jax.ShapeDtypeStruct((2, n), I32), jax.ShapeDtypeStruct((2, n), F32),
                   jax.ShapeDtypeStruct((n // tm * N_EXPERTS, LANES), I32),
                   jax.ShapeDtypeStruct((N_EXPERTS, LANES), I32)],
        scratch_shapes=[pltpu.VMEM((N_EXPERTS, 1), F32)],
        compiler_params=_cparams("arbitrary"),
    )(h, y0, y1, y2, rof, rob, rg, x, g1, gn, sh, sc, *wts)


def _for_run_pieces(tab_ref, ps_ref, fn):
    loff = 0
    for ex in range(N_EXPERTS):
        plen = tab_ref[ex, 0]
        gbase = ps_ref[ex] + tab_ref[ex, 1]

        def piece(c, carry, loff=loff, gbase=gbase):
            fn(pl.multiple_of(loff + c * RUN, RUN), pl.multiple_of(gbase + c * RUN, RUN))
            return carry

        lax.fori_loop(0, plen // RUN, piece, 0)
        loff = loff + plen


def _dispatch_kernel(ps_ref, tab_ref, lp_ref, x_ref, xs_in_ref, xs_ref, buf, sem):
    del xs_in_ref
    rmax, tm = buf.shape[0], x_ref.shape[0]
    lp = lp_ref[...]
    r = lax.broadcasted_iota(I32, (rmax, tm), 0)
    sel = jnp.where((r == lp[0:1, :]) | (r == lp[1:2, :]), 1.0, 0.0).astype(BF16)
    buf[...] = jnp.dot(sel, x_ref[...], preferred_element_type=F32).astype(BF16)

    def copy(s, t):
        return pltpu.make_async_copy(buf.at[pl.ds(s, RUN)], xs_ref.at[pl.ds(t, RUN)], sem)

    _for_run_pieces(tab_ref, ps_ref, lambda s, t: copy(s, t).start())
    _for_run_pieces(tab_ref, ps_ref, lambda s, t: copy(s, t).wait())


def _tile_tables(tm):
    return [pl.BlockSpec(memory_space=pltpu.SMEM),
            pl.BlockSpec((N_EXPERTS, LANES), lambda i: (i, 0), memory_space=pltpu.SMEM)]


def _dispatch(h2, lp, tab, pstart, cap, rmax):
    n = h2.shape[0]
    tm = _row_tile(n, MOE_TILE)
    return pl.pallas_call(
        _dispatch_kernel,
        grid=(n // tm,),
        in_specs=_tile_tables(tm) + [pl.BlockSpec((2, tm), lambda i: (0, i)),
                                     pl.BlockSpec((tm, D), lambda i: (i, 0)),
                                     pl.BlockSpec(memory_space=pl.ANY)],
        out_specs=pl.BlockSpec(memory_space=pl.ANY),
        out_shape=jax.ShapeDtypeStruct((cap, D), BF16),
        scratch_shapes=[pltpu.VMEM((rmax, D), BF16), pltpu.SemaphoreType.DMA(())],
        input_output_aliases={4: 0},
        compiler_params=_cparams("arbitrary"),
    )(pstart, tab, lp, h2, jnp.zeros((cap, D), BF16))


def _expert_kernel(be_ref, nu_ref, x_ref, w1_ref, w3_ref, w2_ref, y_ref, w1b, w3b, w2b):
    b = pl.program_id(0)

    @pl.when((b == 0) | (be_ref[b] != be_ref[jnp.maximum(b - 1, 0)]))
    def _():
        w1b[...] = w1_ref[0, 0].astype(BF16)
        w3b[...] = w3_ref[0, 0].astype(BF16)
        w2b[...] = w2_ref[0, 0].astype(BF16)

    @pl.when(b < nu_ref[0])
    def _():
        x = x_ref[...]
        a = jnp.dot(x, w1b[...], preferred_element_type=F32)
        c = jnp.dot(x, w3b[...], preferred_element_type=F32)
        u = (a * jax.nn.sigmoid(a) * c).astype(BF16)
        y_ref[...] = jnp.dot(u, w2b[...], preferred_element_type=F32).astype(BF16)

    @pl.when(b >= nu_ref[0])
    def _():
        y_ref[...] = jnp.zeros_like(y_ref)


def _experts(xs, blk_e, nused, l, w1, w3, w2):
    cap = xs.shape[0]
    nb = cap // MOE_BLOCK
    xmap = lambda b, be, nu: (jnp.minimum(b, jnp.maximum(nu[0] - 1, 0)), 0)
    wmap = lambda b, be, nu: (l, be[b], 0, 0)
    return pl.pallas_call(
        _expert_kernel,
        grid_spec=pltpu.PrefetchScalarGridSpec(
            num_scalar_prefetch=2,
            grid=(nb,),
            in_specs=[pl.BlockSpec((MOE_BLOCK, D), xmap),
                      pl.BlockSpec((1, 1, D, D_FF), wmap),
                      pl.BlockSpec((1, 1, D, D_FF), wmap),
                      pl.BlockSpec((1, 1, D_FF, D), wmap)],
            out_specs=pl.BlockSpec((MOE_BLOCK, D), lambda b, be, nu: (b, 0)),
            scratch_shapes=[pltpu.VMEM((D, D_FF), BF16), pltpu.VMEM((D, D_FF), BF16),
                            pltpu.VMEM((D_FF, D), BF16)]),
        out_shape=jax.ShapeDtypeStruct((cap, D), BF16),
        compiler_params=_cparams("arbitrary"),
    )(blk_e, nused, xs, w1, w3, w2)


def _combine_kernel(ps_ref, tab_ref, lp_ref, w_ref, x_ref, g2_ref, gf_ref, ys_ref, o_ref, buf, sem, *,
                    final_norm):
    rmax, tm = buf.shape[0], x_ref.shape[0]

    @pl.when(pl.program_id(0) == 0)
    def _():
        buf[...] = jnp.zeros_like(buf)

    def copy(s, t):
        return pltpu.make_async_copy(ys_ref.at[pl.ds(t, RUN)], buf.at[pl.ds(s, RUN)], sem)

    _for_run_pieces(tab_ref, ps_ref, lambda s, t: copy(s, t).start())
    lp = lp_ref[...]
    w = w_ref[...]
    r = lax.broadcasted_iota(I32, (tm, rmax), 1)
    mix = (jnp.where(r == lp[:, 0:1], w[:, 0:1], 0.0)
           + jnp.where(r == lp[:, 1:2], w[:, 1:2], 0.0)).astype(BF16)
    _for_run_pieces(tab_ref, ps_ref, lambda s, t: copy(s, t).wait())
    ml = jnp.dot(mix, buf[...], preferred_element_type=F32)
    xo = x_ref[...] + g2_ref[...] * ml
    if final_norm:
        xo = xo * lax.rsqrt(jnp.mean(xo * xo, axis=-1, keepdims=True) + EPS) * gf_ref[...]
    o_ref[...] = xo


def _combine(x, g2, lpt, wtok, tab, pstart, ys, gf, final_norm, rmax):
    n = x.shape[0]
    tm = _row_tile(n, MOE_TILE)
    vec = pl.BlockSpec((1, D), lambda i: (0, 0))
    tok = pl.BlockSpec((tm, 2), lambda i: (i, 0))
    return pl.pallas_call(
        functools.partial(_combine_kernel, final_norm=final_norm),
        grid=(n // tm,),
        in_specs=_tile_tables(tm) + [tok, tok, pl.BlockSpec((tm, D), lambda i: (i, 0)), vec, vec,
                                     pl.BlockSpec(memory_space=pl.ANY)],
        out_specs=pl.BlockSpec((tm, D), lambda i: (i, 0)),
        out_shape=jax.ShapeDtypeStruct((n, D), F32),
        scratch_shapes=[pltpu.VMEM((rmax, D), BF16), pltpu.SemaphoreType.DMA(())],
        compiler_params=_cparams("arbitrary"),
    )(pstart, tab, lpt, wtok, x, g2, gf, ys)


def _moe(x, h2, lp, wsel, tab, filled, g2, l, w1, w3, w2, gf, final_norm):
    n = x.shape[0]
    tm = _row_tile(n, MOE_TILE)
    rmax = 2 * tm + N_EXPERTS * RUN
    worst = n // tm * (2 * tm + N_EXPERTS * (RUN - 1)) + N_EXPERTS * (MOE_BLOCK - RUN)
    cap = (worst + MOE_BLOCK - 1) // MOE_BLOCK * MOE_BLOCK
    nb = cap // MOE_BLOCK
    region = (filled + MOE_BLOCK - 1) // MOE_BLOCK * MOE_BLOCK
    pend = jnp.cumsum(region)
    pstart = (pend - region).astype(I32)
    blk_start = jnp.arange(nb, dtype=I32) * MOE_BLOCK
    blk_e = jnp.minimum(jnp.sum(pend[None, :] <= blk_start[:, None], axis=-1), N_EXPERTS - 1).astype(I32)
    nused = (pend[-1:] // MOE_BLOCK).astype(I32)
    xs = _dispatch(h2, lp, tab, pstart, cap, rmax)
    ys = _experts(xs, blk_e, nused, l, w1, w3, w2)
    return _combine(x, g2, lp.T, wsel.T, tab, pstart, ys, gf, final_norm, rmax)


def _rope_tables(n):
    rows = n // GRID_W
    row = jnp.repeat(jnp.arange(rows, dtype=F32), GRID_W)
    col = jnp.tile(jnp.arange(GRID_W, dtype=F32), rows)
    nf = DK // 4
    inv = ROPE_BASE ** (-jnp.arange(nf, dtype=F32) / nf)
    ang = jnp.concatenate([row[:, None] * inv, col[:, None] * inv], axis=-1)
    cos, sin = jnp.cos(ang), jnp.sin(ang)
    reps = LANES // DK
    return (jnp.tile(jnp.concatenate([cos, cos], -1), (1, reps)),
            jnp.tile(jnp.concatenate([-sin, sin], -1), (1, reps)))


def _layer_weights(l, w_in, w_gate, b_gate, w_branch, w_o, w_router, b_router):
    w = w_in[l].astype(BF16)
    wb = w_branch[l]
    wb2 = wb[2].reshape(HEADS, DV, D)
    wb2 = jnp.concatenate([wb2, jnp.zeros((HEADS, LANES - DV, D), F32)], axis=1).reshape(HEADS * LANES, D)
    merge_w = (w_gate[l].astype(BF16), b_gate[l].reshape(1, 4 * D), wb[0].astype(BF16), wb[1].astype(BF16),
               wb2.astype(BF16), wb[3].astype(BF16), w_o[l].astype(BF16),
               w_router.T, b_router.reshape(N_EXPERTS, 1))
    return w, merge_w


def kernel(x, c, ctx, c_ctx, w_mod, b_mod, g_norm1, g_norm2, w_in, conv_a_w, conv_a_b, conv_a_g, conv_a_beta, conv_b_w, lam_q1, lam_k1, lam_q2, lam_k2, diff_g, ret_ld_f, ret_ld_b, w_gate, b_gate, w_branch, w_o, w_router, b_router, w1_e, w3_e, w2_e, g_final):
    assert x.shape[0] == 1 and ctx.shape[0] == 1
    xl, xc = x[0], ctx[0]
    n_lat, n_ctx = xl.shape[0], xc.shape[0]
    mods = _adaln(c, c_ctx, w_mod, b_mod)
    cos, sinp = _rope_tables(n_lat)
    zc = jnp.zeros((n_ctx, LANES), F32)
    vone = jnp.zeros((HEADS, LANES), F32).at[:, DV].set(1.0).reshape(1, HEADS * LANES)
    gf = g_final.reshape(1, D)
    vrow = lambda a: a.reshape(1, -1)

    for l in range(DEPTH):
        last = l == DEPTH - 1
        lam_init = 0.8 - 0.6 * math.exp(-0.3 * l)
        ml = [mods[l, 0:1, j * D:(j + 1) * D] for j in range(6)]
        mc = [mods[l, 1:2, j * D:(j + 1) * D] for j in range(6)]
        w, merge_w = _layer_weights(l, w_in, w_gate, b_gate, w_branch, w_o, w_router, b_router)
        g1 = vrow(g_norm1[l])
        g2n = vrow(g_norm2[l])
        lam = (jnp.exp(jnp.sum(lam_q1[l] * lam_k1[l])) - jnp.exp(jnp.sum(lam_q2[l] * lam_k2[l]))
               + lam_init).reshape(1, 1).astype(F32)
        gpad = jnp.concatenate([diff_g[l], jnp.zeros((LANES - DV,), F32)]).reshape(1, LANES)
        ld = jnp.stack([ret_ld_f[l], ret_ld_b[l]]).astype(F32)
        conv_w = (conv_a_w[l], vrow(conv_a_b[l]), vrow(conv_a_g[l]), vrow(conv_a_beta[l]), conv_b_w[l])

        (hc, uc, bgc, cxc, qc, kc, vc, rqc, rkc, rvc, rgc) = _inproj(
            xc, g1, mc[0], mc[1], w, zc, zc, vone, rope=False)
        (hl, ul, bgl, cxl, ql, kl, vl, rql, rkl, rvl, rgl) = _inproj(
            xl, g1, ml[0], ml[1], w, cos, sinp, vone, rope=True)

        kall = jnp.concatenate([kc, kl], axis=0)
        vall = jnp.concatenate([vc, vl], axis=0)
        out_scale = 1.0 - lam_init
        y2l = _attention(ql, kall, vall, lam, gpad, out_scale)
        s0 = jnp.zeros((2, HEADS * DK, HEADS * DV), F32)
        rofc, robc, sfin = _retention(rqc, rkc, rvc, ld, s0)
        rofl, robl, _ = _retention(rql, rkl, rvl, ld, sfin)
        y0l, y1l = _convs(ul, cxl, bgl, *conv_w)
        xl, h2l, lpl, wl, tabl, cl = _merge(hl, y0l, y1l, y2l, rofl, robl, rgl, xl, ml[2], g2n, ml[3], ml[4], merge_w)
        xl = _moe(xl, h2l, lpl, wl, tabl, cl[:, 0], ml[5], l, w1_e, w3_e, w2_e, gf, last)
        if not last:
            y2c = _attention(qc, kc, vc, lam, gpad, out_scale)
            y0c, y1c = _convs(uc, cxc, bgc, *conv_w)
            xc, h2c, lpc, wc, tabc, cc = _merge(hc, y0c, y1c, y2c, rofc, robc, rgc, xc, mc[2], g2n, mc[3], mc[4], merge_w)
            xc = _moe(xc, h2c, lpc, wc, tabc, cc[:, 0], mc[5], l, w1_e, w3_e, w2_e, gf, False)
    return xl[None]
```

```python
import functools
import math

import numpy as np
import jax
import jax.numpy as jnp
from jax import lax
from jax.experimental import pallas as pl
from jax.experimental.pallas import tpu as pltpu

F32 = jnp.float32
BF16 = jnp.bfloat16
I32 = jnp.int32
HI = lax.Precision.HIGHEST

D = 1024
DEPTH = 2
GRID_W = 64
BW = 256
CONF_K = 31
SCONV_K = 3
HEADS = 4
DK = 32
DV = 64
CHUNK = 128
ROPE_BASE = 10000.0
N_EXPERTS = 16
N_GROUPS = 4
EPG = N_EXPERTS // N_GROUPS
D_FF = 512
MOE_BLOCK = 512
EPS = 1e-6
LOG2E = math.log2(math.e)
BOUND_SLACK = 1.0 + 2.0 ** -10
L_MIN = 2.0 ** -64
ATTN_TQ = 512
ATTN_TK = (3328, 1280, 256)
ATTN_TILE = 512 * 3328
RET_ROWS = 512
GATE_COLS = 512
MOE_TILE = 512
RUN = 16
HALO = 16
LANES = 128
SUBLANES = 8
W_COLS = 2816
VMEM_LIMIT = 56 * 1024 * 1024


def _cparams(*sem):
    return pltpu.CompilerParams(dimension_semantics=sem, vmem_limit_bytes=VMEM_LIMIT)


def _row_tile(n, pref):
    return pref if n % pref == 0 else n


def _mod_kernel(c_ref, w_ref, b_ref, o_ref):
    a = c_ref[...]
    a = a * jax.nn.sigmoid(a)
    o_ref[0] = jnp.dot(a, w_ref[0], preferred_element_type=F32, precision=HI) + b_ref[0]


def _adaln(c, c_ctx, w_mod, b_mod):
    cs = jnp.zeros((8, D), F32).at[0].set(c[0]).at[1].set(c_ctx)
    return pl.pallas_call(
        _mod_kernel,
        grid=(DEPTH, 6),
        in_specs=[pl.BlockSpec((8, D), lambda l, j: (0, 0)),
                  pl.BlockSpec((1, D, D), lambda l, j: (l, 0, j)),
                  pl.BlockSpec((1, 1, D), lambda l, j: (l, 0, j))],
        out_specs=pl.BlockSpec((1, 8, D), lambda l, j: (l, 0, j)),
        out_shape=jax.ShapeDtypeStruct((DEPTH, 8, 6 * D), F32),
        compiler_params=_cparams("parallel", "parallel"),
    )(cs, w_mod, b_mod.reshape(DEPTH, 1, 6 * D))


def _rope(x, cos, sinp):
    lane = lax.broadcasted_iota(I32, (1, LANES), 1)
    first = (lane % 32) < 16
    outs = []
    for c in range(x.shape[1] // LANES):
        xc = x[:, c * LANES:(c + 1) * LANES]
        sw = jnp.where(first, pltpu.roll(xc, LANES - 16, 1), pltpu.roll(xc, 16, 1))
        outs.append(xc * cos + sw * sinp)
    return outs[0] if len(outs) == 1 else jnp.concatenate(outs, axis=-1)


def _inproj_kernel(x_ref, g_ref, sh_ref, sc_ref, w_ref, cos_ref, sin_ref, vone_ref,
                   h_ref, u_ref, bg_ref, cx_ref, q_ref, k_ref, v_ref,
                   rq_ref, rk_ref, rv_ref, rg_ref, *, rope):
    x = x_ref[...]
    y = x * lax.rsqrt(jnp.mean(x * x, axis=-1, keepdims=True) + EPS) * g_ref[...]
    hb = (y * (1.0 + sc_ref[...]) + sh_ref[...]).astype(BF16)
    h_ref[...] = hb

    def seg(a, b):
        return jnp.dot(hb, w_ref[:, a:b], preferred_element_type=F32)

    z = seg(0, 512)
    u_ref[...] = z[:, :BW] * jax.nn.sigmoid(z[:, BW:])
    z = seg(512, 1280)
    bg_ref[...] = z[:, :BW]
    cx_ref[...] = z[:, BW:2 * BW] * z[:, 2 * BW:]
    zq = seg(1280, 1536)
    zk = seg(1536, 1792)
    zrq = seg(2048, 2176)
    zrk = seg(2176, 2304)
    if rope:
        cos = cos_ref[...]
        sinp = sin_ref[...]
        zq, zk = _rope(zq, cos, sinp), _rope(zk, cos, sinp)
        zrq, zrk = _rope(zrq, cos, sinp), _rope(zrk, cos, sinp)
    q_ref[...] = (zq * (DK ** -0.5 * LOG2E)).astype(BF16)
    k_ref[...] = zk.astype(BF16)
    vr = lax.broadcasted_iota(I32, (HEADS * DV, HEADS * LANES), 0)
    vc = lax.broadcasted_iota(I32, (HEADS * DV, HEADS * LANES), 1)
    spread = jnp.where((vc // LANES == vr // DV) & (vc % LANES == vr % DV), 1.0, 0.0).astype(BF16)
    zv = seg(1792, 2048).astype(BF16)
    v_ref[...] = (jnp.dot(zv, spread, preferred_element_type=F32) + vone_ref[...]).astype(BF16)
    rq_ref[...] = zrq
    rk_ref[...] = zrk * (DK ** -0.5)
    rv_ref[...] = seg(2304, 2560)
    rg_ref[...] = seg(2560, 2816)


def _inproj(x, g, shift, scale, w, cos, sinp, vone, rope):
    n = x.shape[0]
    tm = _row_tile(n, 512)
    row = lambda c: pl.BlockSpec((tm, c), lambda i: (i, 0))
    vec = lambda c: pl.BlockSpec((1, c), lambda i: (0, 0))
    widths = [(D, BF16), (BW, F32), (BW, F32), (BW, F32), (256, BF16), (256, BF16), (512, BF16),
              (128, F32), (128, F32), (256, F32), (256, F32)]
    return pl.pallas_call(
        functools.partial(_inproj_kernel, rope=rope),
        grid=(n // tm,),
        in_specs=[row(D), vec(D), vec(D), vec(D),
                  pl.BlockSpec((D, W_COLS), lambda i: (0, 0)),
                  row(LANES), row(LANES), vec(512)],
        out_specs=[row(c) for c, _ in widths],
        out_shape=[jax.ShapeDtypeStruct((n, c), dt) for c, dt in widths],
        compiler_params=_cparams("parallel"),
    )(x, g, shift, scale, w, cos, sinp, vone)


def _conv_kernel(up_ref, um_ref, un_ref, cp_ref, cm_ref, cn_ref, bg_ref,
                 wa_ref, ba_ref, ga_ref, bta_ref, wb_ref, y0_ref, y1_ref, eu, ec):
    i = pl.program_id(0)
    last = pl.num_programs(0) - 1
    tm = um_ref.shape[0]
    pad_a = (CONF_K - 1) // 2
    pad_b = (SCONV_K - 1) // 2
    span = tm + 2 * HALO - SUBLANES
    for ext, p_ref, m_ref, n_ref, shifts in ((eu, up_ref, um_ref, un_ref, range(1, SUBLANES)),
                                             (ec, cp_ref, cm_ref, cn_ref, (1, SUBLANES - 1))):
        ext[0, 0:HALO, :] = jnp.where(i > 0, p_ref[...], 0.0)
        ext[0, HALO:HALO + tm, :] = m_ref[...]
        ext[0, HALO + tm:, :] = jnp.where(i < last, n_ref[...], 0.0)
        for s in shifts:
            ext[s, 0:span, :] = ext[0, pl.ds(s, span), :]

    def tap(ext, r0, off):
        return ext[off % SUBLANES, pl.ds(r0 + off - off % SUBLANES, rc), :]

    rc = 32
    for r0 in range(0, tm, rc):
        acc = jnp.zeros((rc, BW), F32)
        for k in range(CONF_K):
            acc = acc + tap(eu, r0, HALO + k - pad_a) * wa_ref[k:k + 1, :]
        acc = acc + ba_ref[...]
        mu = jnp.mean(acc, axis=-1, keepdims=True)
        xc = acc - mu
        var = jnp.mean(xc * xc, axis=-1, keepdims=True)
        yn = xc * lax.rsqrt(var + EPS) * ga_ref[...] + bta_ref[...]
        y0_ref[pl.ds(r0, rc), :] = (yn * jax.nn.sigmoid(yn)).astype(BF16)
        accb = jnp.zeros((rc, BW), F32)
        for k in range(SCONV_K):
            accb = accb + tap(ec, r0, HALO + k - pad_b) * wb_ref[k:k + 1, :]
        y1_ref[pl.ds(r0, rc), :] = (bg_ref[pl.ds(r0, rc), :] * accb).astype(BF16)


def _convs(u, cx, bg, wa, ba, ga, bta, wb):
    n = u.shape[0]
    tm = _row_tile(n, 256)
    hb = tm // HALO
    nh = n // HALO
    prev = pl.BlockSpec((HALO, BW), lambda i: (jnp.maximum(i * hb - 1, 0), 0))
    main = pl.BlockSpec((tm, BW), lambda i: (i, 0))
    nxt = pl.BlockSpec((HALO, BW), lambda i: (jnp.minimum((i + 1) * hb, nh - 1), 0))
    vec = lambda r: pl.BlockSpec((r, BW), lambda i: (0, 0))
    return pl.pallas_call(
        _conv_kernel,
        grid=(n // tm,),
        in_specs=[prev, main, nxt, prev, main, nxt, main,
                  vec(CONF_K), vec(1), vec(1), vec(1), vec(SCONV_K)],
        out_specs=[main, main],
        out_shape=[jax.ShapeDtypeStruct((n, BW), BF16)] * 2,
        scratch_shapes=[pltpu.VMEM((SUBLANES, tm + 2 * HALO, BW), F32)] * 2,
        compiler_params=_cparams("parallel"),
    )(u, u, u, cx, cx, cx, bg, wa, ba, ga, bta, wb)


def _attn_kernel(lam_ref, q_ref, k_ref, v_ref, g_ref, o_ref, kmax_sc, acc_sc, *, tk, nk, out_scale):
    h = pl.program_id(0)
    kc = HEADS * 2 * DK

    @pl.when((h == 0) & (pl.program_id(1) == 0))
    def _():
        def colmax(j, mx):
            blk = k_ref[pl.ds(pl.multiple_of(j * kc, kc), kc), :].astype(F32)
            return jnp.maximum(mx, jnp.max(jnp.abs(blk), axis=0, keepdims=True))
        kmax_sc[...] = lax.fori_loop(0, nk * tk // kc, colmax, jnp.zeros((1, kc), F32))

    q = q_ref[...]
    tq = q.shape[0]
    lane_q = lax.broadcasted_iota(I32, (1, kc), 1)
    lane_o = lax.broadcasted_iota(I32, (1, LANES), 1)
    qbound = jnp.abs(q.astype(F32)) * kmax_sc[...]
    qms, shifts = [], []
    for m in range(2):
        lo = (h * 2 + m) * DK
        sel = (lane_q >= lo) & (lane_q < lo + DK)
        qms.append(jnp.where(sel, q, jnp.zeros_like(q)))
        ub = jnp.sum(jnp.where(sel, qbound, 0.0), axis=-1, keepdims=True) * BOUND_SLACK
        shifts.append(jnp.tile(jnp.broadcast_to(ub, (tq, LANES)), (1, tk // LANES)))

    def tiles(j):
        start = pl.multiple_of(j * tk, tk)
        return k_ref[pl.ds(start, tk), :], v_ref[pl.ds(start, tk), :]

    def scores(m, kj):
        return lax.dot_general(qms[m], kj, (((1,), (1,)), ((), ())), preferred_element_type=F32)

    acc_sc[...] = jnp.zeros(acc_sc.shape, F32)

    def shifted(j, carry):
        kj, vj = tiles(j)
        for m in range(2):
            p = jnp.exp2(scores(m, kj) - shifts[m]).astype(BF16)
            acc_sc[m] += jnp.dot(p, vj, preferred_element_type=F32)
        return carry

    lax.fori_loop(0, nk, shifted, 0)
    lmin = jnp.minimum(jnp.min(acc_sc[0][:, DV:DV + 1]), jnp.min(acc_sc[1][:, DV:DV + 1]))

    @pl.when(jnp.logical_not(lmin >= L_MIN))
    def _():
        def online(j, carry):
            kj, vj = tiles(j)
            new = []
            for m in range(2):
                mx, acc = carry[m]
                s = scores(m, kj)
                mnew = jnp.maximum(mx, jnp.max(s, axis=-1, keepdims=True))
                p = jnp.exp2(s - mnew).astype(BF16)
                new.append((mnew, jnp.exp2(mx - mnew) * acc + jnp.dot(p, vj, preferred_element_type=F32)))
            return tuple(new)

        init = (jnp.full((tq, 1), -jnp.inf, F32), jnp.zeros((tq, LANES), F32))
        res = lax.fori_loop(0, nk, online, (init, init))
        for m in range(2):
            acc_sc[m] = res[m][1]

    outs = [acc_sc[m] / acc_sc[m][:, DV:DV + 1] for m in range(2)]
    o = outs[0] - lam_ref[0, 0] * outs[1]
    valid = lane_o < DV
    ms = jnp.sum(jnp.where(valid, o * o, 0.0), axis=-1, keepdims=True) * (1.0 / DV)
    y = o * lax.rsqrt(ms + EPS) * g_ref[...] * out_scale
    o_ref[...] = jnp.where(valid, y, 0.0).astype(BF16)


def _attention(q, k, v, lam, gpad, out_scale):
    n = q.shape[0]
    nkeys = k.shape[0]
    tq = _row_tile(n, ATTN_TQ)
    tk = next(t for t in ATTN_TK if nkeys % t == 0 and t * tq <= ATTN_TILE)
    kern = functools.partial(_attn_kernel, tk=tk, nk=nkeys // tk, out_scale=out_scale)
    return pl.pallas_call(
        kern,
        grid=(HEADS, n // tq),
        in_specs=[pl.BlockSpec(memory_space=pltpu.SMEM),
                  pl.BlockSpec((tq, 256), lambda h, i: (i, 0)),
                  pl.BlockSpec((nkeys, 256), lambda h, i: (0, 0)),
                  pl.BlockSpec((nkeys, LANES), lambda h, i: (0, h)),
                  pl.BlockSpec((1, LANES), lambda h, i: (0, 0))],
        out_specs=pl.BlockSpec((tq, LANES), lambda h, i: (i, h)),
        out_shape=jax.ShapeDtypeStruct((n, HEADS * LANES), BF16),
        scratch_shapes=[pltpu.VMEM((1, HEADS * 2 * DK), F32), pltpu.VMEM((2, tq, LANES), F32)],
        compiler_params=_cparams("arbitrary", "arbitrary"),
    )(lam, q, k, v, gpad)


def _ret_kernel(ld_ref, qf_ref, kf_ref, vf_ref, qb_ref, kb_ref, vb_ref, s0_ref, of_ref, ob_ref, sf_ref,
                state, dmask, qdec, kdec, cdec):
    n = pl.program_id(0)
    c = CHUNK
    lane_q = lax.broadcasted_iota(I32, (1, HEADS * DK), 1)
    lane_v = lax.broadcasted_iota(I32, (1, HEADS * DV), 1)

    @pl.when(n == 0)
    def _():
        state[...] = s0_ref[...]
        row_h = lax.broadcasted_iota(I32, (HEADS * DK, 1), 0) // DK
        for d in range(2):
            pi = lax.broadcasted_iota(I32, (c, 1), 0).astype(F32)
            pj = lax.broadcasted_iota(I32, (1, c), 1).astype(F32)
            if d == 1:
                pi, pj = c - 1.0 - pi, c - 1.0 - pj
            diff = pi - pj
            lg_q = jnp.zeros((1, HEADS * DK), F32)
            lg_r = jnp.zeros((HEADS * DK, 1), F32)
            for h in range(HEADS):
                lg = ld_ref[d, h]
                dmask[d, h] = jnp.exp(jnp.where(diff >= 0, diff * lg, -jnp.inf))
                lg_q = jnp.where(lane_q // DK == h, lg, lg_q)
                lg_r = jnp.where(row_h == h, lg, lg_r)
            qdec[d] = jnp.exp((pi + 1.0) * lg_q)
            kdec[d] = jnp.exp((c - 1.0 - pi) * lg_q)
            cdec[d] = jnp.where(row_h == lane_v // DV, jnp.exp(c * lg_r), 0.0)

    for d, (q_ref, k_ref, v_ref, o_ref) in enumerate(((qf_ref, kf_ref, vf_ref, of_ref),
                                                      (qb_ref, kb_ref, vb_ref, ob_ref))):
        nsub = q_ref.shape[0] // c
        cd = cdec[d]
        s_cur = state[d]
        for sub in (range(nsub) if d == 0 else reversed(range(nsub))):
            rows = pl.ds(sub * c, c)
            q = q_ref[rows, :]
            k = k_ref[rows, :]
            kb = k.astype(BF16)
            vb = v_ref[rows, :].astype(BF16)
            o = jnp.dot((q * qdec[d]).astype(BF16), s_cur.astype(BF16), preferred_element_type=F32)
            for h in range(HEADS):
                qm = jnp.where(lane_q // DK == h, q, 0.0).astype(BF16)
                sc = lax.dot_general(qm, kb, (((1,), (1,)), ((), ())),
                                     preferred_element_type=F32) * dmask[d, h]
                oh = jnp.dot(sc.astype(BF16), vb, preferred_element_type=F32)
                o = o + jnp.where(lane_v // DV == h, oh, 0.0)
            o_ref[rows, :] = o
            kv = lax.dot_general((k * kdec[d]).astype(BF16), vb, (((0,), (0,)), ((), ())),
                                 preferred_element_type=F32)
            s_cur = cd * s_cur + jnp.where(cd != 0.0, kv, 0.0)
        state[d] = s_cur

    @pl.when(n == pl.num_programs(0) - 1)
    def _():
        sf_ref[...] = state[...]


def _retention(rq, rk, rv, ld, s0):
    n = rq.shape[0]
    rows = _row_tile(n, RET_ROWS)
    nc = n // rows
    fwd = lambda w: pl.BlockSpec((rows, w), lambda i: (i, 0))
    bwd = lambda w: pl.BlockSpec((rows, w), lambda i: (nc - 1 - i, 0))
    sshape = (2, HEADS * DK, HEADS * DV)
    whole = pl.BlockSpec(sshape, lambda i: (0, 0, 0))
    dq, dv = HEADS * DK, HEADS * DV
    return pl.pallas_call(
        _ret_kernel,
        grid=(nc,),
        in_specs=[pl.BlockSpec(memory_space=pltpu.SMEM),
                  fwd(dq), fwd(dq), fwd(dv), bwd(dq), bwd(dq), bwd(dv), whole],
        out_specs=[fwd(dv), bwd(dv), whole],
        out_shape=[jax.ShapeDtypeStruct((n, dv), F32), jax.ShapeDtypeStruct((n, dv), F32),
                   jax.ShapeDtypeStruct(sshape, F32)],
        scratch_shapes=[pltpu.VMEM(sshape, F32),
                        pltpu.VMEM((2, HEADS, CHUNK, CHUNK), F32),
                        pltpu.VMEM((2, CHUNK, dq), F32),
                        pltpu.VMEM((2, CHUNK, dq), F32),
                        pltpu.VMEM(sshape, F32)],
        compiler_params=_cparams("arbitrary"),
    )(ld, rq, rk, rv, rq, rk, rv, s0)


def _first_max(vals):
    idx = jnp.zeros(vals[0].shape, I32)
    best = vals[0]
    for j in range(1, len(vals)):
        upd = vals[j] > best
        idx = jnp.where(upd, j, idx)
        best = jnp.where(upd, vals[j], best)
    return idx, best


def _split_bf16(a):
    hi = a.astype(BF16)
    return hi, (a - hi.astype(F32)).astype(BF16)


def _pick(vals, idx):
    out = vals[-1]
    for j in range(len(vals) - 2, -1, -1):
        out = jnp.where(idx == j, vals[j], out)
    return out


def _merge_kernel(h_ref, y0_ref, y1_ref, y2_ref, rof_ref, rob_ref, rg_ref, x_ref, g1_ref, gn_ref, sh_ref, sc_ref,
                  wg_ref, bgate_ref, wb0_ref, wb1_ref, wb2_ref, wb3_ref, wo_ref, wr_ref, br_ref,
                  xo_ref, h2_ref, lp_ref, w_ref, tab_ref, cnt_ref, base):
    i = pl.program_id(0)
    tm = x_ref.shape[0]

    @pl.when(i == 0)
    def _():
        base[...] = jnp.zeros_like(base)

    ro = rof_ref[...] + rob_ref[...]
    gr = lax.broadcasted_iota(I32, (HEADS * DV, HEADS * DV), 0) // DV
    gc = lax.broadcasted_iota(I32, (HEADS * DV, HEADS * DV), 1) // DV
    avg = jnp.where(gr == gc, 1.0 / DV, 0.0).astype(BF16)

    def head_mean(a):
        hi, lo = _split_bf16(a)
        return (jnp.dot(hi, avg, preferred_element_type=F32) + jnp.dot(lo, avg, preferred_element_type=F32))

    mu = head_mean(ro)
    xc = ro - mu
    var = head_mean(xc * xc)
    rg = rg_ref[...]
    y3 = (rg * jax.nn.sigmoid(rg) * (xc * lax.rsqrt(var + EPS))).astype(BF16)

    hb = h_ref[...]
    ys = (y0_ref[...], y1_ref[...], y2_ref[...], y3)
    wbs = (wb0_ref, wb1_ref, wb2_ref, wb3_ref)
    parts = []
    for c0 in range(0, D, GATE_COLS):
        mc = None
        for b in range(4):
            lo = b * D + c0
            gl = jnp.dot(hb, wg_ref[:, lo:lo + GATE_COLS], preferred_element_type=F32)
            gate = jax.nn.sigmoid(gl + bgate_ref[:, lo:lo + GATE_COLS])
            t = gate * jnp.dot(ys[b], wbs[b][:, c0:c0 + GATE_COLS], preferred_element_type=F32)
            mc = t if mc is None else mc + t
        parts.append(mc.astype(BF16))
    yo = jnp.dot(jnp.concatenate(parts, axis=-1), wo_ref[...], preferred_element_type=F32)
    xn = x_ref[...] + g1_ref[...] * yo
    xo_ref[...] = xn
    yn = xn * lax.rsqrt(jnp.mean(xn * xn, axis=-1, keepdims=True) + EPS) * gn_ref[...]
    h2 = yn * (1.0 + sc_ref[...]) + sh_ref[...]
    h2_ref[...] = h2.astype(BF16)

    h_hi, h_lo = _split_bf16(h2)
    w_hi, w_lo = _split_bf16(wr_ref[...])
    nt = (((1,), (1,)), ((), ()))
    l2 = lax.dot_general(jnp.concatenate([w_hi, w_lo], axis=0), h_hi, nt, preferred_element_type=F32)
    lt = (l2[:N_EXPERTS] + l2[N_EXPERTS:]) + lax.dot_general(w_hi, h_lo, nt, preferred_element_type=F32)
    s = jax.nn.sigmoid(lt)
    sb = s + br_ref[...]
    r = [sb[e:e + 1, :] for e in range(N_EXPERTS)]
    sr = [s[e:e + 1, :] for e in range(N_EXPERTS)]
    gscore = []
    for g in range(N_GROUPS):
        a, b, c, d = r[EPG * g:EPG * (g + 1)]
        hi1, lo1, hi2, lo2 = jnp.maximum(a, b), jnp.minimum(a, b), jnp.maximum(c, d), jnp.minimum(c, d)
        gscore.append(jnp.maximum(hi1, hi2) + jnp.maximum(jnp.minimum(hi1, hi2), jnp.maximum(lo1, lo2)))
    gsel, _ = _first_max(gscore)
    v = [_pick([r[EPG * g + j] for g in range(N_GROUPS)], gsel) for j in range(EPG)]
    sv = [_pick([sr[EPG * g + j] for g in range(N_GROUPS)], gsel) for j in range(EPG)]
    i1, _ = _first_max(v)
    i2, _ = _first_max([jnp.where(i1 == j, -jnp.inf, v[j]) for j in range(EPG)])
    w1 = _pick(sv, i1)
    w2 = _pick(sv, i2)
    den = w1 + w2
    e1 = gsel * EPG + i1
    e2 = gsel * EPG + i2
    w_ref[0:1, :] = w1 / den
    w_ref[1:2, :] = w2 / den

    eio = lax.broadcasted_iota(I32, (N_EXPERTS, tm), 0)
    oh1 = eio == e1
    oh2 = eio == e2
    cnt = oh1.astype(F32) + oh2.astype(F32)
    ti = lax.broadcasted_iota(I32, (tm, tm), 0)
    tj = lax.broadcasted_iota(I32, (tm, tm), 1)
    before = jnp.where(ti < tj, 1.0, 0.0).astype(BF16)
    pref = jnp.dot(cnt.astype(BF16), before, preferred_element_type=F32)
    plen = jnp.floor((jnp.sum(cnt, axis=1, keepdims=True) + (RUN - 1.0)) * (1.0 / RUN)) * RUN
    eio1 = lax.broadcasted_iota(I32, (N_EXPERTS, 1), 0)
    loff = jnp.zeros((N_EXPERTS, 1), F32)
    run_start = jnp.zeros((1, 1), F32)
    for ex in range(N_EXPERTS):
        loff = jnp.where(eio1 == ex, run_start, loff)
        run_start = run_start + plen[ex:ex + 1, :]
    start = loff + pref
    lp_ref[0:1, :] = jnp.sum(jnp.where(oh1, start, 0.0), axis=0, keepdims=True).astype(I32)
    lp_ref[1:2, :] = jnp.sum(jnp.where(oh2, start, 0.0), axis=0, keepdims=True).astype(I32)
    goff = base[...]
    lane = lax.broadcasted_iota(I32, (1, LANES), 1)
    tab_ref[...] = jnp.where(lane == 0, plen, jnp.where(lane == 1, goff, 0.0)).astype(I32)
    base[...] = goff + plen
    cnt_ref[...] = jnp.broadcast_to(goff + plen, (N_EXPERTS, LANES)).astype(I32)


def _merge(h, y0, y1, y2, rof, rob, rg, x, g1, gn, sh, sc, wts):
    n = x.shape[0]
    tm = _row_tile(n, MOE_TILE)
    row = lambda c: pl.BlockSpec((tm, c), lambda i: (i, 0))
    full = lambda a: pl.BlockSpec(a.shape, lambda i: (0,) * a.ndim)
    tok = pl.BlockSpec((2, tm), lambda i: (0, i))
    return pl.pallas_call(
        _merge_kernel,
        grid=(n // tm,),
        in_specs=[row(D), row(BW), row(BW), row(HEADS * LANES),
                  row(HEADS * DV), row(HEADS * DV), row(BW), row(D),
                  full(g1), full(gn), full(sh), full(sc)] + [full(a) for a in wts],
        out_specs=[row(D), row(D), tok, tok,
                   pl.BlockSpec((N_EXPERTS, LANES), lambda i: (i, 0)),
                   pl.BlockSpec((N_EXPERTS, LANES), lambda i: (0, 0))],
        out_shape=[jax.ShapeDtypeStruct((n, D), F32), jax.ShapeDtypeStruct((n, D), BF16),
                   jax.ShapeDtypeStruct((2, n), I32), jax.ShapeDtypeStruct((2, n), F32),
                   jax.ShapeDtypeStruct((n // tm * N_EXPERTS, LANES), I32),
                   jax.ShapeDtypeStruct((N_EXPERTS, LANES), I32)],
        scratch_shapes=[pltpu.VMEM((N_EXPERTS, 1), F32)],
        compiler_params=_cparams("arbitrary"),
    )(h, y0, y1, y2, rof, rob, rg, x, g1, gn, sh, sc, *wts)


def _for_run_pieces(tab_ref, ps_ref, fn):
    loff = 0
    for ex in range(N_EXPERTS):
        plen = tab_ref[ex, 0]
        gbase = ps_ref[ex] + tab_ref[ex, 1]

        def piece(c, carry, loff=loff, gbase=gbase):
            fn(pl.multiple_of(loff + c * RUN, RUN), pl.multiple_of(gbase + c * RUN, RUN))
            return carry

        lax.fori_loop(0, plen // RUN, piece, 0)
        loff = loff + plen


def _dispatch_kernel(ps_ref, tab_ref, tail_ref, lp_ref, x_ref, xs_ref, buf, sem):
    rmax, tm = buf.shape[0], x_ref.shape[0]
    lp = lp_ref[...]
    r = lax.broadcasted_iota(I32, (rmax, tm), 0)
    sel = jnp.where((r == lp[0:1, :]) | (r == lp[1:2, :]), 1.0, 0.0).astype(BF16)
    buf[...] = jnp.dot(sel, x_ref[...], preferred_element_type=F32).astype(BF16)

    def copy(s, t):
        return pltpu.make_async_copy(buf.at[pl.ds(s, RUN)], xs_ref.at[pl.ds(t, RUN)], sem)

    _for_run_pieces(tab_ref, ps_ref, lambda s, t: copy(s, t).start())
    _for_run_pieces(tab_ref, ps_ref, lambda s, t: copy(s, t).wait())

    @pl.when(pl.program_id(0) == pl.num_programs(0) - 1)
    def _():
        buf[0:MOE_BLOCK, :] = jnp.zeros((MOE_BLOCK, D), BF16)

        def tails(fn):
            for ex in range(N_EXPERTS):
                first = tail_ref[0, ex]

                def piece(c, carry, first=first):
                    fn(pl.multiple_of(first + c * RUN, RUN))
                    return carry

                lax.fori_loop(0, tail_ref[1, ex] // RUN, piece, 0)

        def block_copy(b):
            dst = xs_ref.at[pl.ds(pl.multiple_of(b * MOE_BLOCK, MOE_BLOCK), MOE_BLOCK)]
            return pltpu.make_async_copy(buf.at[pl.ds(0, MOE_BLOCK)], dst, sem)

        def unused(fn):
            def blk(b, carry):
                fn(b)
                return carry
            lax.fori_loop(tail_ref[2, 0], xs_ref.shape[0] // MOE_BLOCK, blk, 0)

        tails(lambda t: copy(0, t).start())
        unused(lambda b: block_copy(b).start())
        tails(lambda t: copy(0, t).wait())
        unused(lambda b: block_copy(b).wait())


def _tile_tables(tm):
    return [pl.BlockSpec(memory_space=pltpu.SMEM),
            pl.BlockSpec((N_EXPERTS, LANES), lambda i: (i, 0), memory_space=pltpu.SMEM)]


def _dispatch(h2, lp, tab, pstart, tails, cap, rmax):
    n = h2.shape[0]
    tm = _row_tile(n, MOE_TILE)
    return pl.pallas_call(
        _dispatch_kernel,
        grid=(n // tm,),
        in_specs=_tile_tables(tm) + [pl.BlockSpec(memory_space=pltpu.SMEM),
                                     pl.BlockSpec((2, tm), lambda i: (0, i)),
                                     pl.BlockSpec((tm, D), lambda i: (i, 0))],
        out_specs=pl.BlockSpec(memory_space=pl.ANY),
        out_shape=jax.ShapeDtypeStruct((cap, D), BF16),
        scratch_shapes=[pltpu.VMEM((rmax, D), BF16), pltpu.SemaphoreType.DMA(())],
        compiler_params=_cparams("arbitrary"),
    )(pstart, tab, tails, lp, h2)


def _expert_kernel(be_ref, nu_ref, x_ref, w1_ref, w3_ref, w2_ref, y_ref, w1b, w3b, w2b):
    b = pl.program_id(0)

    @pl.when((b == 0) | (be_ref[b] != be_ref[jnp.maximum(b - 1, 0)]))
    def _():
        w1b[...] = w1_ref[0, 0].astype(BF16)
        w3b[...] = w3_ref[0, 0].astype(BF16)
        w2b[...] = w2_ref[0, 0].astype(BF16)

    @pl.when(b < nu_ref[0])
    def _():
        x = x_ref[...]
        a = jnp.dot(x, w1b[...], preferred_element_type=F32)
        c = jnp.dot(x, w3b[...], preferred_element_type=F32)
        u = (a * jax.nn.sigmoid(a) * c).astype(BF16)
        y_ref[...] = jnp.dot(u, w2b[...], preferred_element_type=F32).astype(BF16)

    @pl.when(b >= nu_ref[0])
    def _():
        y_ref[...] = jnp.zeros_like(y_ref)


def _experts(xs, blk_e, nused, l, w1, w3, w2):
    cap = xs.shape[0]
    nb = cap // MOE_BLOCK
    xmap = lambda b, be, nu: (jnp.minimum(b, jnp.maximum(nu[0] - 1, 0)), 0)
    wmap = lambda b, be, nu: (l, be[b], 0, 0)
    return pl.pallas_call(
        _expert_kernel,
        grid_spec=pltpu.PrefetchScalarGridSpec(
            num_scalar_prefetch=2,
            grid=(nb,),
            in_specs=[pl.BlockSpec((MOE_BLOCK, D), xmap),
                      pl.BlockSpec((1, 1, D, D_FF), wmap),
                      pl.BlockSpec((1, 1, D, D_FF), wmap),
                      pl.BlockSpec((1, 1, D_FF, D), wmap)],
            out_specs=pl.BlockSpec((MOE_BLOCK, D), lambda b, be, nu: (b, 0)),
            scratch_shapes=[pltpu.VMEM((D, D_FF), BF16), pltpu.VMEM((D, D_FF), BF16),
                            pltpu.VMEM((D_FF, D), BF16)]),
        out_shape=jax.ShapeDtypeStruct((cap, D), BF16),
        compiler_params=_cparams("arbitrary"),
    )(blk_e, nused, xs, w1, w3, w2)


def _combine_kernel(ps_ref, tab_ref, lp_ref, w_ref, x_ref, g2_ref, gf_ref, ys_ref, o_ref, buf, sem, *,
                    final_norm):
    rmax, tm = buf.shape[0], x_ref.shape[0]

    @pl.when(pl.program_id(0) == 0)
    def _():
        buf[...] = jnp.zeros_like(buf)

    def copy(s, t):
        return pltpu.make_async_copy(ys_ref.at[pl.ds(t, RUN)], buf.at[pl.ds(s, RUN)], sem)

    _for_run_pieces(tab_ref, ps_ref, lambda s, t: copy(s, t).start())
    lp = lp_ref[...]
    w = w_ref[...]
    r = lax.broadcasted_iota(I32, (tm, rmax), 1)
    mix = (jnp.where(r == lp[:, 0:1], w[:, 0:1], 0.0)
           + jnp.where(r == lp[:, 1:2], w[:, 1:2], 0.0)).astype(BF16)
    _for_run_pieces(tab_ref, ps_ref, lambda s, t: copy(s, t).wait())
    ml = jnp.dot(mix, buf[...], preferred_element_type=F32)
    xo = x_ref[...] + g2_ref[...] * ml
    if final_norm:
        xo = xo * lax.rsqrt(jnp.mean(xo * xo, axis=-1, keepdims=True) + EPS) * gf_ref[...]
    o_ref[...] = xo


def _combine(x, g2, lpt, wtok, tab, pstart, ys, gf, final_norm, rmax):
    n = x.shape[0]
    tm = _row_tile(n, MOE_TILE)
    vec = pl.BlockSpec((1, D), lambda i: (0, 0))
    tok = pl.BlockSpec((tm, 2), lambda i: (i, 0))
    return pl.pallas_call(
        functools.partial(_combine_kernel, final_norm=final_norm),
        grid=(n // tm,),
        in_specs=_tile_tables(tm) + [tok, tok, pl.BlockSpec((tm, D), lambda i: (i, 0)), vec, vec,
                                     pl.BlockSpec(memory_space=pl.ANY)],
        out_specs=pl.BlockSpec((tm, D), lambda i: (i, 0)),
        out_shape=jax.ShapeDtypeStruct((n, D), F32),
        scratch_shapes=[pltpu.VMEM((rmax, D), BF16), pltpu.SemaphoreType.DMA(())],
        compiler_params=_cparams("arbitrary"),
    )(pstart, tab, lpt, wtok, x, g2, gf, ys)


def _moe(x, h2, lp, wsel, tab, filled, g2, l, w1, w3, w2, gf, final_norm):
    n = x.shape[0]
    tm = _row_tile(n, MOE_TILE)
    rmax = 2 * tm + N_EXPERTS * RUN
    worst = n // tm * (2 * tm + N_EXPERTS * (RUN - 1)) + N_EXPERTS * (MOE_BLOCK - RUN)
    cap = (worst + MOE_BLOCK - 1) // MOE_BLOCK * MOE_BLOCK
    nb = cap // MOE_BLOCK
    region = (filled + MOE_BLOCK - 1) // MOE_BLOCK * MOE_BLOCK
    pend = jnp.cumsum(region)
    pstart = (pend - region).astype(I32)
    blk_start = jnp.arange(nb, dtype=I32) * MOE_BLOCK
    blk_e = jnp.minimum(jnp.sum(pend[None, :] <= blk_start[:, None], axis=-1), N_EXPERTS - 1).astype(I32)
    nused = (pend[-1:] // MOE_BLOCK).astype(I32)
    tails = jnp.stack([pstart + filled, region - filled, jnp.broadcast_to(nused, (N_EXPERTS,))]).astype(I32)
    xs = _dispatch(h2, lp, tab, pstart, tails, cap, rmax)
    ys = _experts(xs, blk_e, nused, l, w1, w3, w2)
    return _combine(x, g2, lp.T, wsel.T, tab, pstart, ys, gf, final_norm, rmax)


def _rope_tables(n):
    rows = n // GRID_W
    row = jnp.repeat(jnp.arange(rows, dtype=F32), GRID_W)
    col = jnp.tile(jnp.arange(GRID_W, dtype=F32), rows)
    nf = DK // 4
    inv = ROPE_BASE ** (-jnp.arange(nf, dtype=F32) / nf)
    ang = jnp.concatenate([row[:, None] * inv, col[:, None] * inv], axis=-1)
    cos, sin = jnp.cos(ang), jnp.sin(ang)
    reps = LANES // DK
    return (jnp.tile(jnp.concatenate([cos, cos], -1), (1, reps)),
            jnp.tile(jnp.concatenate([-sin, sin], -1), (1, reps)))


def _layer_weights(l, w_in, w_gate, b_gate, w_branch, w_o, w_router, b_router):
    w = w_in[l].astype(BF16)
    wb = w_branch[l]
    wb2 = wb[2].reshape(HEADS, DV, D)
    wb2 = jnp.concatenate([wb2, jnp.zeros((HEADS, LANES - DV, D), F32)], axis=1).reshape(HEADS * LANES, D)
    merge_w = (w_gate[l].astype(BF16), b_gate[l].reshape(1, 4 * D), wb[0].astype(BF16), wb[1].astype(BF16),
               wb2.astype(BF16), wb[3].astype(BF16), w_o[l].astype(BF16),
               w_router.T, b_router.reshape(N_EXPERTS, 1))
    return w, merge_w


def kernel(x, c, ctx, c_ctx, w_mod, b_mod, g_norm1, g_norm2, w_in, conv_a_w, conv_a_b, conv_a_g, conv_a_beta, conv_b_w, lam_q1, lam_k1, lam_q2, lam_k2, diff_g, ret_ld_f, ret_ld_b, w_gate, b_gate, w_branch, w_o, w_router, b_router, w1_e, w3_e, w2_e, g_final):
    assert x.shape[0] == 1 and ctx.shape[0] == 1
    xl, xc = x[0], ctx[0]
    n_lat, n_ctx = xl.shape[0], xc.shape[0]
    mods = _adaln(c, c_ctx, w_mod, b_mod)
    cos, sinp = _rope_tables(n_lat)
    zc = jnp.zeros((n_ctx, LANES), F32)
    vone = jnp.zeros((HEADS, LANES), F32).at[:, DV].set(1.0).reshape(1, HEADS * LANES)
    gf = g_final.reshape(1, D)
    vrow = lambda a: a.reshape(1, -1)

    for l in range(DEPTH):
        last = l == DEPTH - 1
        lam_init = 0.8 - 0.6 * math.exp(-0.3 * l)
        ml = [mods[l, 0:1, j * D:(j + 1) * D] for j in range(6)]
        mc = [mods[l, 1:2, j * D:(j + 1) * D] for j in range(6)]
        w, merge_w = _layer_weights(l, w_in, w_gate, b_gate, w_branch, w_o, w_router, b_router)
        g1 = vrow(g_norm1[l])
        g2n = vrow(g_norm2[l])
        lam = (jnp.exp(jnp.sum(lam_q1[l] * lam_k1[l])) - jnp.exp(jnp.sum(lam_q2[l] * lam_k2[l]))
               + lam_init).reshape(1, 1).astype(F32)
        gpad = jnp.concatenate([diff_g[l], jnp.zeros((LANES - DV,), F32)]).reshape(1, LANES)
        ld = jnp.stack([ret_ld_f[l], ret_ld_b[l]]).astype(F32)
        conv_w = (conv_a_w[l], vrow(conv_a_b[l]), vrow(conv_a_g[l]), vrow(conv_a_beta[l]), conv_b_w[l])

        (hc, uc, bgc, cxc, qc, kc, vc, rqc, rkc, rvc, rgc) = _inproj(
            xc, g1, mc[0], mc[1], w, zc, zc, vone, rope=False)
        (hl, ul, bgl, cxl, ql, kl, vl, rql, rkl, rvl, rgl) = _inproj(
            xl, g1, ml[0], ml[1], w, cos, sinp, vone, rope=True)

        kall = jnp.concatenate([kc, kl], axis=0)
        vall = jnp.concatenate([vc, vl], axis=0)
        out_scale = 1.0 - lam_init
        y2l = _attention(ql, kall, vall, lam, gpad, out_scale)
        s0 = jnp.zeros((2, HEADS * DK, HEADS * DV), F32)
        rofc, robc, sfin = _retention(rqc, rkc, rvc, ld, s0)
        rofl, robl, _ = _retention(rql, rkl, rvl, ld, sfin)
        y0l, y1l = _convs(ul, cxl, bgl, *conv_w)
        xl, h2l, lpl, wl, tabl, cl = _merge(hl, y0l, y1l, y2l, rofl, robl, rgl, xl, ml[2], g2n, ml[3], ml[4], merge_w)
        xl = _moe(xl, h2l, lpl, wl, tabl, cl[:, 0], ml[5], l, w1_e, w3_e, w2_e, gf, last)
        if not last:
            y2c = _attention(qc, kc, vc, lam, gpad, out_scale)
            y0c, y1c = _convs(uc, cxc, bgc, *conv_w)
            xc, h2c, lpc, wc, tabc, cc = _merge(hc, y0c, y1c, y2c, rofc, robc, rgc, xc, mc[2], g2n, mc[3], mc[4], merge_w)
            xc = _moe(xc, h2c, lpc, wc, tabc, cc[:, 0], mc[5], l, w1_e, w3_e, w2_e, gf, False)
    return xl[None]
```

```python
import functools
import math

import numpy as np
import jax
import jax.numpy as jnp
from jax import lax
from jax.experimental import pallas as pl
from jax.experimental.pallas import tpu as pltpu

F32 = jnp.float32
BF16 = jnp.bfloat16
I32 = jnp.int32
HI = lax.Precision.HIGHEST

D = 1024
DEPTH = 2
GRID_W = 64
BW = 256
CONF_K = 31
SCONV_K = 3
HEADS = 4
DK = 32
DV = 64
CHUNK = 128
ROPE_BASE = 10000.0
N_EXPERTS = 16
N_GROUPS = 4
EPG = N_EXPERTS // N_GROUPS
D_FF = 512
MOE_BLOCK = 512
EPS = 1e-6
LOG2E = math.log2(math.e)
BOUND_SLACK = 1.0 + 2.0 ** -10
L_MIN = 2.0 ** -64
ATTN_TQ = 512
ATTN_TK = (3328, 1280, 256)
ATTN_TILE = 512 * 3328
RET_ROWS = 512
GATE_COLS = 512
MOE_TILE = 512
RUN = 16
HALO = 16
LANES = 128
SUBLANES = 8
W_COLS = 2816
VMEM_LIMIT = 56 * 1024 * 1024


def _cparams(*sem):
    return pltpu.CompilerParams(dimension_semantics=sem, vmem_limit_bytes=VMEM_LIMIT)


def _row_tile(n, pref):
    return pref if n % pref == 0 else n


def _mod_kernel(c_ref, w_ref, b_ref, o_ref):
    a = c_ref[...]
    a = a * jax.nn.sigmoid(a)
    o_ref[0] = jnp.dot(a, w_ref[0], preferred_element_type=F32, precision=HI) + b_ref[0]


def _adaln(c, c_ctx, w_mod, b_mod):
    cs = jnp.zeros((8, D), F32).at[0].set(c[0]).at[1].set(c_ctx)
    return pl.pallas_call(
        _mod_kernel,
        grid=(DEPTH, 6),
        in_specs=[pl.BlockSpec((8, D), lambda l, j: (0, 0)),
                  pl.BlockSpec((1, D, D), lambda l, j: (l, 0, j)),
                  pl.BlockSpec((1, 1, D), lambda l, j: (l, 0, j))],
        out_specs=pl.BlockSpec((1, 8, D), lambda l, j: (l, 0, j)),
        out_shape=jax.ShapeDtypeStruct((DEPTH, 8, 6 * D), F32),
        compiler_params=_cparams("parallel", "parallel"),
    )(cs, w_mod, b_mod.reshape(DEPTH, 1, 6 * D))


def _rope(x, cos, sinp):
    lane = lax.broadcasted_iota(I32, (1, LANES), 1)
    first = (lane % 32) < 16
    outs = []
    for c in range(x.shape[1] // LANES):
        xc = x[:, c * LANES:(c + 1) * LANES]
        sw = jnp.where(first, pltpu.roll(xc, LANES - 16, 1), pltpu.roll(xc, 16, 1))
        outs.append(xc * cos + sw * sinp)
    return outs[0] if len(outs) == 1 else jnp.concatenate(outs, axis=-1)


def _inproj_kernel(x_ref, g_ref, sh_ref, sc_ref, w_ref, cos_ref, sin_ref, vone_ref,
                   h_ref, u_ref, bg_ref, cx_ref, q_ref, k_ref, v_ref,
                   rq_ref, rk_ref, rv_ref, rg_ref, *, rope):
    x = x_ref[...]
    y = x * lax.rsqrt(jnp.mean(x * x, axis=-1, keepdims=True) + EPS) * g_ref[...]
    hb = (y * (1.0 + sc_ref[...]) + sh_ref[...]).astype(BF16)
    h_ref[...] = hb

    def seg(a, b):
        return jnp.dot(hb, w_ref[:, a:b], preferred_element_type=F32)

    z = seg(0, 512)
    u_ref[...] = z[:, :BW] * jax.nn.sigmoid(z[:, BW:])
    z = seg(512, 1280)
    bg_ref[...] = z[:, :BW]
    cx_ref[...] = z[:, BW:2 * BW] * z[:, 2 * BW:]
    zq = seg(1280, 1536)
    zk = seg(1536, 1792)
    zrq = seg(2048, 2176)
    zrk = seg(2176, 2304)
    if rope:
        cos = cos_ref[...]
        sinp = sin_ref[...]
        zq, zk = _rope(zq, cos, sinp), _rope(zk, cos, sinp)
        zrq, zrk = _rope(zrq, cos, sinp), _rope(zrk, cos, sinp)
    q_ref[...] = (zq * (DK ** -0.5 * LOG2E)).astype(BF16)
    k_ref[...] = zk.astype(BF16)
    vr = lax.broadcasted_iota(I32, (HEADS * DV, HEADS * LANES), 0)
    vc = lax.broadcasted_iota(I32, (HEADS * DV, HEADS * LANES), 1)
    spread = jnp.where((vc // LANES == vr // DV) & (vc % LANES == vr % DV), 1.0, 0.0).astype(BF16)
    zv = seg(1792, 2048).astype(BF16)
    v_ref[...] = (jnp.dot(zv, spread, preferred_element_type=F32) + vone_ref[...]).astype(BF16)
    rq_ref[...] = zrq
    rk_ref[...] = zrk * (DK ** -0.5)
    rv_ref[...] = seg(2304, 2560)
    rg_ref[...] = seg(2560, 2816)


def _inproj(x, g, shift, scale, w, cos, sinp, vone, rope):
    n = x.shape[0]
    tm = _row_tile(n, 512)
    row = lambda c: pl.BlockSpec((tm, c), lambda i: (i, 0))
    vec = lambda c: pl.BlockSpec((1, c), lambda i: (0, 0))
    widths = [(D, BF16), (BW, F32), (BW, F32), (BW, F32), (256, BF16), (256, BF16), (512, BF16),
              (128, F32), (128, F32), (256, F32), (256, F32)]
    return pl.pallas_call(
        functools.partial(_inproj_kernel, rope=rope),
        grid=(n // tm,),
        in_specs=[row(D), vec(D), vec(D), vec(D),
                  pl.BlockSpec((D, W_COLS), lambda i: (0, 0)),
                  row(LANES), row(LANES), vec(512)],
        out_specs=[row(c) for c, _ in widths],
        out_shape=[jax.ShapeDtypeStruct((n, c), dt) for c, dt in widths],
        compiler_params=_cparams("parallel"),
    )(x, g, shift, scale, w, cos, sinp, vone)


def _conv_kernel(up_ref, um_ref, un_ref, cp_ref, cm_ref, cn_ref, bg_ref,
                 wa_ref, ba_ref, ga_ref, bta_ref, wb_ref, y0_ref, y1_ref, eu, ec):
    i = pl.program_id(0)
    last = pl.num_programs(0) - 1
    tm = um_ref.shape[0]
    pad_a = (CONF_K - 1) // 2
    pad_b = (SCONV_K - 1) // 2
    span = tm + 2 * HALO - SUBLANES
    for ext, p_ref, m_ref, n_ref, shifts in ((eu, up_ref, um_ref, un_ref, range(1, SUBLANES)),
                                             (ec, cp_ref, cm_ref, cn_ref, (1, SUBLANES - 1))):
        ext[0, 0:HALO, :] = jnp.where(i > 0, p_ref[...], 0.0)
        ext[0, HALO:HALO + tm, :] = m_ref[...]
        ext[0, HALO + tm:, :] = jnp.where(i < last, n_ref[...], 0.0)
        for s in shifts:
            ext[s, 0:span, :] = ext[0, pl.ds(s, span), :]

    def tap(ext, r0, off):
        return ext[off % SUBLANES, pl.ds(r0 + off - off % SUBLANES, rc), :]

    rc = 32
    for r0 in range(0, tm, rc):
        acc = jnp.zeros((rc, BW), F32)
        for k in range(CONF_K):
            acc = acc + tap(eu, r0, HALO + k - pad_a) * wa_ref[k:k + 1, :]
        acc = acc + ba_ref[...]
        mu = jnp.mean(acc, axis=-1, keepdims=True)
        xc = acc - mu
        var = jnp.mean(xc * xc, axis=-1, keepdims=True)
        yn = xc * lax.rsqrt(var + EPS) * ga_ref[...] + bta_ref[...]
        y0_ref[pl.ds(r0, rc), :] = (yn * jax.nn.sigmoid(yn)).astype(BF16)
        accb = jnp.zeros((rc, BW), F32)
        for k in range(SCONV_K):
            accb = accb + tap(ec, r0, HALO + k - pad_b) * wb_ref[k:k + 1, :]
        y1_ref[pl.ds(r0, rc), :] = (bg_ref[pl.ds(r0, rc), :] * accb).astype(BF16)


def _attn_kernel(lam_ref, q_ref, k_ref, v_ref, g_ref, o_ref, kmax_sc, acc_sc, *, tk, nk, out_scale):
    h = pl.program_id(0)
    kc = HEADS * 2 * DK

    @pl.when((h == 0) & (pl.program_id(1) == 0))
    def _():
        def colmax(j, mx):
            blk = k_ref[pl.ds(pl.multiple_of(j * kc, kc), kc), :].astype(F32)
            return jnp.maximum(mx, jnp.max(jnp.abs(blk), axis=0, keepdims=True))
        kmax_sc[...] = lax.fori_loop(0, nk * tk // kc, colmax, jnp.zeros((1, kc), F32))

    q = q_ref[...]
    tq = q.shape[0]
    lane_q = lax.broadcasted_iota(I32, (1, kc), 1)
    lane_o = lax.broadcasted_iota(I32, (1, LANES), 1)
    qbound = jnp.abs(q.astype(F32)) * kmax_sc[...]
    qms, shifts = [], []
    for m in range(2):
        lo = (h * 2 + m) * DK
        sel = (lane_q >= lo) & (lane_q < lo + DK)
        qms.append(jnp.where(sel, q, jnp.zeros_like(q)))
        ub = jnp.sum(jnp.where(sel, qbound, 0.0), axis=-1, keepdims=True) * BOUND_SLACK
        shifts.append(jnp.tile(jnp.broadcast_to(ub, (tq, LANES)), (1, tk // LANES)))

    def tiles(j):
        start = pl.multiple_of(j * tk, tk)
        return k_ref[pl.ds(start, tk), :], v_ref[pl.ds(start, tk), :]

    def scores(m, kj):
        return lax.dot_general(qms[m], kj, (((1,), (1,)), ((), ())), preferred_element_type=F32)

    acc_sc[...] = jnp.zeros(acc_sc.shape, F32)

    def shifted(j, carry):
        kj, vj = tiles(j)
        for m in range(2):
            p = jnp.exp2(scores(m, kj) - shifts[m]).astype(BF16)
            acc_sc[m] += jnp.dot(p, vj, preferred_element_type=F32)
        return carry

    lax.fori_loop(0, nk, shifted, 0)
    lmin = jnp.minimum(jnp.min(acc_sc[0][:, DV:DV + 1]), jnp.min(acc_sc[1][:, DV:DV + 1]))

    @pl.when(jnp.logical_not(lmin >= L_MIN))
    def _():
        def online(j, carry):
            kj, vj = tiles(j)
            new = []
            for m in range(2):
                mx, acc = carry[m]
                s = scores(m, kj)
                mnew = jnp.maximum(mx, jnp.max(s, axis=-1, keepdims=True))
                p = jnp.exp2(s - mnew).astype(BF16)
                new.append((mnew, jnp.exp2(mx - mnew) * acc + jnp.dot(p, vj, preferred_element_type=F32)))
            return tuple(new)

        init = (jnp.full((tq, 1), -jnp.inf, F32), jnp.zeros((tq, LANES), F32))
        res = lax.fori_loop(0, nk, online, (init, init))
        for m in range(2):
            acc_sc[m] = res[m][1]

    outs = [acc_sc[m] / acc_sc[m][:, DV:DV + 1] for m in range(2)]
    o = outs[0] - lam_ref[0, 0] * outs[1]
    valid = lane_o < DV
    ms = jnp.sum(jnp.where(valid, o * o, 0.0), axis=-1, keepdims=True) * (1.0 / DV)
    y = o * lax.rsqrt(ms + EPS) * g_ref[...] * out_scale
    o_ref[...] = jnp.where(valid, y, 0.0).astype(BF16)


def _attention(q, k, v, lam, gpad, out_scale):
    n = q.shape[0]
    nkeys = k.shape[0]
    tq = _row_tile(n, ATTN_TQ)
    tk = next(t for t in ATTN_TK if nkeys % t == 0 and t * tq <= ATTN_TILE)
    kern = functools.partial(_attn_kernel, tk=tk, nk=nkeys // tk, out_scale=out_scale)
    return pl.pallas_call(
        kern,
        grid=(HEADS, n // tq),
        in_specs=[pl.BlockSpec(memory_space=pltpu.SMEM),
                  pl.BlockSpec((tq, 256), lambda h, i: (i, 0)),
                  pl.BlockSpec((nkeys, 256), lambda h, i: (0, 0)),
                  pl.BlockSpec((nkeys, LANES), lambda h, i: (0, h)),
                  pl.BlockSpec((1, LANES), lambda h, i: (0, 0))],
        out_specs=pl.BlockSpec((tq, LANES), lambda h, i: (i, h)),
        out_shape=jax.ShapeDtypeStruct((n, HEADS * LANES), BF16),
        scratch_shapes=[pltpu.VMEM((1, HEADS * 2 * DK), F32), pltpu.VMEM((2, tq, LANES), F32)],
        compiler_params=_cparams("arbitrary", "arbitrary"),
    )(lam, q, k, v, gpad)


def _ret_kernel(ld_ref, qf_ref, kf_ref, vf_ref, qb_ref, kb_ref, vb_ref, s0_ref, of_ref, ob_ref, sf_ref,
                state, dmask, qdec, kdec, cdec, *, after_init=None):
    n = pl.program_id(0)
    c = CHUNK
    lane_q = lax.broadcasted_iota(I32, (1, HEADS * DK), 1)
    lane_v = lax.broadcasted_iota(I32, (1, HEADS * DV), 1)

    @pl.when(n == 0)
    def _():
        state[...] = s0_ref[...]
        row_h = lax.broadcasted_iota(I32, (HEADS * DK, 1), 0) // DK
        for d in range(2):
            pi = lax.broadcasted_iota(I32, (c, 1), 0).astype(F32)
            pj = lax.broadcasted_iota(I32, (1, c), 1).astype(F32)
            if d == 1:
                pi, pj = c - 1.0 - pi, c - 1.0 - pj
            diff = pi - pj
            lg_q = jnp.zeros((1, HEADS * DK), F32)
            lg_r = jnp.zeros((HEADS * DK, 1), F32)
            for h in range(HEADS):
                lg = ld_ref[d, h]
                dmask[d, h] = jnp.exp(jnp.where(diff >= 0, diff * lg, -jnp.inf))
                lg_q = jnp.where(lane_q // DK == h, lg, lg_q)
                lg_r = jnp.where(row_h == h, lg, lg_r)
            qdec[d] = jnp.exp((pi + 1.0) * lg_q)
            kdec[d] = jnp.exp((c - 1.0 - pi) * lg_q)
            cdec[d] = jnp.where(row_h == lane_v // DV, jnp.exp(c * lg_r), 0.0)

    if after_init is not None:
        after_init()

    for d, (q_ref, k_ref, v_ref, o_ref) in enumerate(((qf_ref, kf_ref, vf_ref, of_ref),
                                                      (qb_ref, kb_ref, vb_ref, ob_ref))):
        nsub = q_ref.shape[0] // c
        cd = cdec[d]
        s_cur = state[d]
        for sub in (range(nsub) if d == 0 else reversed(range(nsub))):
            rows = pl.ds(sub * c, c)
            q = q_ref[rows, :]
            k = k_ref[rows, :]
            kb = k.astype(BF16)
            vb = v_ref[rows, :].astype(BF16)
            o = jnp.dot((q * qdec[d]).astype(BF16), s_cur.astype(BF16), preferred_element_type=F32)
            for h in range(HEADS):
                qm = jnp.where(lane_q // DK == h, q, 0.0).astype(BF16)
                sc = lax.dot_general(qm, kb, (((1,), (1,)), ((), ())),
                                     preferred_element_type=F32) * dmask[d, h]
                oh = jnp.dot(sc.astype(BF16), vb, preferred_element_type=F32)
                o = o + jnp.where(lane_v // DV == h, oh, 0.0)
            o_ref[rows, :] = o
            kv = lax.dot_general((k * kdec[d]).astype(BF16), vb, (((0,), (0,)), ((), ())),
                                 preferred_element_type=F32)
            s_cur = cd * s_cur + jnp.where(cd != 0.0, kv, 0.0)
        state[d] = s_cur

    @pl.when(n == pl.num_programs(0) - 1)
    def _():
        sf_ref[...] = state[...]


N_CONV_IN, N_RET_IN, N_CONV_OUT, N_RET_OUT, N_CONV_SCR = 12, 8, 2, 3, 2


def _convret_kernel(*refs):
    cin, rin = refs[:N_CONV_IN], refs[N_CONV_IN:N_CONV_IN + N_RET_IN]
    outs = refs[N_CONV_IN + N_RET_IN:]
    cout, rout = outs[:N_CONV_OUT], outs[N_CONV_OUT:N_CONV_OUT + N_RET_OUT]
    scr = outs[N_CONV_OUT + N_RET_OUT:]
    _ret_kernel(*rin, *rout, *scr[N_CONV_SCR:],
                after_init=lambda: _conv_kernel(*cin, *cout, *scr[:N_CONV_SCR]))


def _convs_retention(u, cx, bg, conv_w, rq, rk, rv, ld, s0):
    n = u.shape[0]
    rows = _row_tile(n, RET_ROWS)
    nc = n // rows
    hb = rows // HALO
    nh = n // HALO
    prev = pl.BlockSpec((HALO, BW), lambda i: (jnp.maximum(i * hb - 1, 0), 0))
    nxt = pl.BlockSpec((HALO, BW), lambda i: (jnp.minimum((i + 1) * hb, nh - 1), 0))
    vec = lambda r: pl.BlockSpec((r, BW), lambda i: (0, 0))
    fwd = lambda w: pl.BlockSpec((rows, w), lambda i: (i, 0))
    bwd = lambda w: pl.BlockSpec((rows, w), lambda i: (nc - 1 - i, 0))
    sshape = (2, HEADS * DK, HEADS * DV)
    whole = pl.BlockSpec(sshape, lambda i: (0, 0, 0))
    dq, dv = HEADS * DK, HEADS * DV
    main = fwd(BW)
    return pl.pallas_call(
        _convret_kernel,
        grid=(nc,),
        in_specs=[prev, main, nxt, prev, main, nxt, main,
                  vec(CONF_K), vec(1), vec(1), vec(1), vec(SCONV_K),
                  pl.BlockSpec(memory_space=pltpu.SMEM),
                  fwd(dq), fwd(dq), fwd(dv), bwd(dq), bwd(dq), bwd(dv), whole],
        out_specs=[main, main, fwd(dv), bwd(dv), whole],
        out_shape=[jax.ShapeDtypeStruct((n, BW), BF16), jax.ShapeDtypeStruct((n, BW), BF16),
                   jax.ShapeDtypeStruct((n, dv), F32), jax.ShapeDtypeStruct((n, dv), F32),
                   jax.ShapeDtypeStruct(sshape, F32)],
        scratch_shapes=[pltpu.VMEM((SUBLANES, rows + 2 * HALO, BW), F32)] * N_CONV_SCR
                       + [pltpu.VMEM(sshape, F32),
                          pltpu.VMEM((2, HEADS, CHUNK, CHUNK), F32),
                          pltpu.VMEM((2, CHUNK, dq), F32),
                          pltpu.VMEM((2, CHUNK, dq), F32),
                          pltpu.VMEM(sshape, F32)],
        compiler_params=_cparams("arbitrary"),
    )(u, u, u, cx, cx, cx, bg, *conv_w, ld, rq, rk, rv, rq, rk, rv, s0)


def _first_max(vals):
    idx = jnp.zeros(vals[0].shape, I32)
    best = vals[0]
    for j in range(1, len(vals)):
        upd = vals[j] > best
        idx = jnp.where(upd, j, idx)
        best = jnp.where(upd, vals[j], best)
    return idx, best


def _split_bf16(a):
    hi = a.astype(BF16)
    return hi, (a - hi.astype(F32)).astype(BF16)


def _pick(vals, idx):
    out = vals[-1]
    for j in range(len(vals) - 2, -1, -1):
        out = jnp.where(idx == j, vals[j], out)
    return out


def _merge_kernel(h_ref, y0_ref, y1_ref, y2_ref, rof_ref, rob_ref, rg_ref, x_ref, g1_ref, gn_ref, sh_ref, sc_ref,
                  wg_ref, bgate_ref, wb0_ref, wb1_ref, wb2_ref, wb3_ref, wo_ref, wr_ref, br_ref,
                  xo_ref, h2_ref, lp_ref, w_ref, tab_ref, cnt_ref, base):
    i = pl.program_id(0)
    tm = x_ref.shape[0]

    @pl.when(i == 0)
    def _():
        base[...] = jnp.zeros_like(base)

    ro = rof_ref[...] + rob_ref[...]
    gr = lax.broadcasted_iota(I32, (HEADS * DV, HEADS * DV), 0) // DV
    gc = lax.broadcasted_iota(I32, (HEADS * DV, HEADS * DV), 1) // DV
    avg = jnp.where(gr == gc, 1.0 / DV, 0.0).astype(BF16)

    def head_mean(a):
        hi, lo = _split_bf16(a)
        return (jnp.dot(hi, avg, preferred_element_type=F32) + jnp.dot(lo, avg, preferred_element_type=F32))

    mu = head_mean(ro)
    xc = ro - mu
    var = head_mean(xc * xc)
    rg = rg_ref[...]
    y3 = (rg * jax.nn.sigmoid(rg) * (xc * lax.rsqrt(var + EPS))).astype(BF16)

    hb = h_ref[...]
    ys = (y0_ref[...], y1_ref[...], y2_ref[...], y3)
    wbs = (wb0_ref, wb1_ref, wb2_ref, wb3_ref)
    parts = []
    for c0 in range(0, D, GATE_COLS):
        mc = None
        for b in range(4):
            lo = b * D + c0
            gl = jnp.dot(hb, wg_ref[:, lo:lo + GATE_COLS], preferred_element_type=F32)
            gate = jax.nn.sigmoid(gl + bgate_ref[:, lo:lo + GATE_COLS])
            t = gate * jnp.dot(ys[b], wbs[b][:, c0:c0 + GATE_COLS], preferred_element_type=F32)
            mc = t if mc is None else mc + t
        parts.append(mc.astype(BF16))
    yo = jnp.dot(jnp.concatenate(parts, axis=-1), wo_ref[...], preferred_element_type=F32)
    xn = x_ref[...] + g1_ref[...] * yo
    xo_ref[...] = xn
    yn = xn * lax.rsqrt(jnp.mean(xn * xn, axis=-1, keepdims=True) + EPS) * gn_ref[...]
    h2 = yn * (1.0 + sc_ref[...]) + sh_ref[...]
    h2_ref[...] = h2.astype(BF16)

    h_hi, h_lo = _split_bf16(h2)
    w_hi, w_lo = _split_bf16(wr_ref[...])
    nt = (((1,), (1,)), ((), ()))
    l2 = lax.dot_general(jnp.concatenate([w_hi, w_lo], axis=0), h_hi, nt, preferred_element_type=F32)
    lt = (l2[:N_EXPERTS] + l2[N_EXPERTS:]) + lax.dot_general(w_hi, h_lo, nt, preferred_element_type=F32)
    s = jax.nn.sigmoid(lt)
    sb = s + br_ref[...]
    r = [sb[e:e + 1, :] for e in range(N_EXPERTS)]
    sr = [s[e:e + 1, :] for e in range(N_EXPERTS)]
    gscore = []
    for g in range(N_GROUPS):
        a, b, c, d = r[EPG * g:EPG * (g + 1)]
        hi1, lo1, hi2, lo2 = jnp.maximum(a, b), jnp.minimum(a, b), jnp.maximum(c, d), jnp.minimum(c, d)
        gscore.append(jnp.maximum(hi1, hi2) + jnp.maximum(jnp.minimum(hi1, hi2), jnp.maximum(lo1, lo2)))
    gsel, _ = _first_max(gscore)
    v = [_pick([r[EPG * g + j] for g in range(N_GROUPS)], gsel) for j in range(EPG)]
    sv = [_pick([sr[EPG * g + j] for g in range(N_GROUPS)], gsel) for j in range(EPG)]
    i1, _ = _first_max(v)
    i2, _ = _first_max([jnp.where(i1 == j, -jnp.inf, v[j]) for j in range(EPG)])
    w1 = _pick(sv, i1)
    w2 = _pick(sv, i2)
    den = w1 + w2
    e1 = gsel * EPG + i1
    e2 = gsel * EPG + i2
    w_ref[0:1, :] = w1 / den
    w_ref[1:2, :] = w2 / den

    eio = lax.broadcasted_iota(I32, (N_EXPERTS, tm), 0)
    oh1 = eio == e1
    oh2 = eio == e2
    cnt = oh1.astype(F32) + oh2.astype(F32)
    ti = lax.broadcasted_iota(I32, (tm, tm), 0)
    tj = lax.broadcasted_iota(I32, (tm, tm), 1)
    before = jnp.where(ti < tj, 1.0, 0.0).astype(BF16)
    pref = jnp.dot(cnt.astype(BF16), before, preferred_element_type=F32)
    plen = jnp.floor((jnp.sum(cnt, axis=1, keepdims=True) + (RUN - 1.0)) * (1.0 / RUN)) * RUN
    eio1 = lax.broadcasted_iota(I32, (N_EXPERTS, 1), 0)
    loff = jnp.zeros((N_EXPERTS, 1), F32)
    run_start = jnp.zeros((1, 1), F32)
    for ex in range(N_EXPERTS):
        loff = jnp.where(eio1 == ex, run_start, loff)
        run_start = run_start + plen[ex:ex + 1, :]
    start = loff + pref
    lp_ref[0:1, :] = jnp.sum(jnp.where(oh1, start, 0.0), axis=0, keepdims=True).astype(I32)
    lp_ref[1:2, :] = jnp.sum(jnp.where(oh2, start, 0.0), axis=0, keepdims=True).astype(I32)
    goff = base[...]
    lane = lax.broadcasted_iota(I32, (1, LANES), 1)
    tab_ref[...] = jnp.where(lane == 0, plen, jnp.where(lane == 1, goff, 0.0)).astype(I32)
    base[...] = goff + plen
    cnt_ref[...] = jnp.broadcast_to(goff + plen, (N_EXPERTS, LANES)).astype(I32)


def _merge(h, y0, y1, y2, rof, rob, rg, x, g1, gn, sh, sc, wts):
    n = x.shape[0]
    tm = _row_tile(n, MOE_TILE)
    row = lambda c: pl.BlockSpec((tm, c), lambda i: (i, 0))
    full = lambda a: pl.BlockSpec(a.shape, lambda i: (0,) * a.ndim)
    tok = pl.BlockSpec((2, tm), lambda i: (0, i))
    return pl.pallas_call(
        _merge_kernel,
        grid=(n // tm,),
        in_specs=[row(D), row(BW), row(BW), row(HEADS * LANES),
                  row(HEADS * DV), row(HEADS * DV), row(BW), row(D),
                  full(g1), full(gn), full(sh), full(sc)] + [full(a) for a in wts],
        out_specs=[row(D), row(D), tok, tok,
                   pl.BlockSpec((N_EXPERTS, LANES), lambda i: (i, 0)),
                   pl.BlockSpec((N_EXPERTS, LANES), lambda i: (0, 0))],
        out_shape=[jax.ShapeDtypeStruct((n, D), F32), jax.ShapeDtypeStruct((n, D), BF16),
                   jax.ShapeDtypeStruct((2, n), I32), jax.ShapeDtypeStruct((2, n), F32),
                   jax.ShapeDtypeStruct((n // tm * N_EXPERTS, LANES), I32),
                   jax.ShapeDtypeStruct((N_EXPERTS, LANES), I32)],
        scratch_shapes=[pltpu.VMEM((N_EXPERTS, 1), F32)],
        compiler_params=_cparams("arbitrary"),
    )(h, y0, y1, y2, rof, rob, rg, x, g1, gn, sh, sc, *wts)


def _for_run_pieces(tab_ref, ps_ref, fn):
    loff = 0
    for ex in range(N_EXPERTS):
        plen = tab_ref[ex, 0]
        gbase = ps_ref[ex] + tab_ref[ex, 1]

        def piece(c, carry, loff=loff, gbase=gbase):
            fn(pl.multiple_of(loff + c * RUN, RUN), pl.multiple_of(gbase + c * RUN, RUN))
            return carry

        lax.fori_loop(0, plen // RUN, piece, 0)
        loff = loff + plen


def _dispatch_kernel(ps_ref, tab_ref, tail_ref, lp_ref, x_ref, xs_ref, buf, sem):
    rmax, tm = buf.shape[0], x_ref.shape[0]
    lp = lp_ref[...]
    r = lax.broadcasted_iota(I32, (rmax, tm), 0)
    sel = jnp.where((r == lp[0:1, :]) | (r == lp[1:2, :]), 1.0, 0.0).astype(BF16)
    buf[...] = jnp.dot(sel, x_ref[...], preferred_element_type=F32).astype(BF16)

    def copy(s, t):
        return pltpu.make_async_copy(buf.at[pl.ds(s, RUN)], xs_ref.at[pl.ds(t, RUN)], sem)

    _for_run_pieces(tab_ref, ps_ref, lambda s, t: copy(s, t).start())
    _for_run_pieces(tab_ref, ps_ref, lambda s, t: copy(s, t).wait())

    @pl.when(pl.program_id(0) == pl.num_programs(0) - 1)
    def _():
        buf[0:MOE_BLOCK, :] = jnp.zeros((MOE_BLOCK, D), BF16)

        def tails(fn):
            for ex in range(N_EXPERTS):
                first = tail_ref[0, ex]

                def piece(c, carry, first=first):
                    fn(pl.multiple_of(first + c * RUN, RUN))
                    return carry

                lax.fori_loop(0, tail_ref[1, ex] // RUN, piece, 0)

        def block_copy(b):
            dst = xs_ref.at[pl.ds(pl.multiple_of(b * MOE_BLOCK, MOE_BLOCK), MOE_BLOCK)]
            return pltpu.make_async_copy(buf.at[pl.ds(0, MOE_BLOCK)], dst, sem)

        def unused(fn):
            def blk(b, carry):
                fn(b)
                return carry
            lax.fori_loop(tail_ref[2, 0], xs_ref.shape[0] // MOE_BLOCK, blk, 0)

        tails(lambda t: copy(0, t).start())
        unused(lambda b: block_copy(b).start())
        tails(lambda t: copy(0, t).wait())
        unused(lambda b: block_copy(b).wait())


def _tile_tables(tm):
    return [pl.BlockSpec(memory_space=pltpu.SMEM),
            pl.BlockSpec((N_EXPERTS, LANES), lambda i: (i, 0), memory_space=pltpu.SMEM)]


def _dispatch(h2, lp, tab, pstart, tails, cap, rmax):
    n = h2.shape[0]
    tm = _row_tile(n, MOE_TILE)
    return pl.pallas_call(
        _dispatch_kernel,
        grid=(n // tm,),
        in_specs=_tile_tables(tm) + [pl.BlockSpec(memory_space=pltpu.SMEM),
                                     pl.BlockSpec((2, tm), lambda i: (0, i)),
                                     pl.BlockSpec((tm, D), lambda i: (i, 0))],
        out_specs=pl.BlockSpec(memory_space=pl.ANY),
        out_shape=jax.ShapeDtypeStruct((cap, D), BF16),
        scratch_shapes=[pltpu.VMEM((rmax, D), BF16), pltpu.SemaphoreType.DMA(())],
        compiler_params=_cparams("arbitrary"),
    )(pstart, tab, tails, lp, h2)


def _expert_kernel(be_ref, nu_ref, x_ref, w1_ref, w3_ref, w2_ref, y_ref, w1b, w3b, w2b):
    b = pl.program_id(0)

    @pl.when((b == 0) | (be_ref[b] != be_ref[jnp.maximum(b - 1, 0)]))
    def _():
        w1b[...] = w1_ref[0, 0].astype(BF16)
        w3b[...] = w3_ref[0, 0].astype(BF16)
        w2b[...] = w2_ref[0, 0].astype(BF16)

    @pl.when(b < nu_ref[0])
    def _():
        x = x_ref[...]
        a = jnp.dot(x, w1b[...], preferred_element_type=F32)
        c = jnp.dot(x, w3b[...], preferred_element_type=F32)
        u = (a * jax.nn.sigmoid(a) * c).astype(BF16)
        y_ref[...] = jnp.dot(u, w2b[...], preferred_element_type=F32).astype(BF16)

    @pl.when(b >= nu_ref[0])
    def _():
        y_ref[...] = jnp.zeros_like(y_ref)


def _experts(xs, blk_e, nused, l, w1, w3, w2):
    cap = xs.shape[0]
    nb = cap // MOE_BLOCK
    xmap = lambda b, be, nu: (jnp.minimum(b, jnp.maximum(nu[0] - 1, 0)), 0)
    wmap = lambda b, be, nu: (l, be[b], 0, 0)
    return pl.pallas_call(
        _expert_kernel,
        grid_spec=pltpu.PrefetchScalarGridSpec(
            num_scalar_prefetch=2,
            grid=(nb,),
            in_specs=[pl.BlockSpec((MOE_BLOCK, D), xmap),
                      pl.BlockSpec((1, 1, D, D_FF), wmap),
                      pl.BlockSpec((1, 1, D, D_FF), wmap),
                      pl.BlockSpec((1, 1, D_FF, D), wmap)],
            out_specs=pl.BlockSpec((MOE_BLOCK, D), lambda b, be, nu: (b, 0)),
            scratch_shapes=[pltpu.VMEM((D, D_FF), BF16), pltpu.VMEM((D, D_FF), BF16),
                            pltpu.VMEM((D_FF, D), BF16)]),
        out_shape=jax.ShapeDtypeStruct((cap, D), BF16),
        compiler_params=_cparams("arbitrary"),
    )(blk_e, nused, xs, w1, w3, w2)


def _combine_kernel(ps_ref, tab_ref, lp_ref, w_ref, x_ref, g2_ref, gf_ref, ys_ref, o_ref, buf, sem, *,
                    final_norm):
    rmax, tm = buf.shape[0], x_ref.shape[0]

    @pl.when(pl.program_id(0) == 0)
    def _():
        buf[...] = jnp.zeros_like(buf)

    def copy(s, t):
        return pltpu.make_async_copy(ys_ref.at[pl.ds(t, RUN)], buf.at[pl.ds(s, RUN)], sem)

    _for_run_pieces(tab_ref, ps_ref, lambda s, t: copy(s, t).start())
    lp = lp_ref[...]
    w = w_ref[...]
    r = lax.broadcasted_iota(I32, (tm, rmax), 1)
    mix = (jnp.where(r == lp[:, 0:1], w[:, 0:1], 0.0)
           + jnp.where(r == lp[:, 1:2], w[:, 1:2], 0.0)).astype(BF16)
    _for_run_pieces(tab_ref, ps_ref, lambda s, t: copy(s, t).wait())
    ml = jnp.dot(mix, buf[...], preferred_element_type=F32)
    xo = x_ref[...] + g2_ref[...] * ml
    if final_norm:
        xo = xo * lax.rsqrt(jnp.mean(xo * xo, axis=-1, keepdims=True) + EPS) * gf_ref[...]
    o_ref[...] = xo


def _combine(x, g2, lpt, wtok, tab, pstart, ys, gf, final_norm, rmax):
    n = x.shape[0]
    tm = _row_tile(n, MOE_TILE)
    vec = pl.BlockSpec((1, D), lambda i: (0, 0))
    tok = pl.BlockSpec((tm, 2), lambda i: (i, 0))
    return pl.pallas_call(
        functools.partial(_combine_kernel, final_norm=final_norm),
        grid=(n // tm,),
        in_specs=_tile_tables(tm) + [tok, tok, pl.BlockSpec((tm, D), lambda i: (i, 0)), vec, vec,
                                     pl.BlockSpec(memory_space=pl.ANY)],
        out_specs=pl.BlockSpec((tm, D), lambda i: (i, 0)),
        out_shape=jax.ShapeDtypeStruct((n, D), F32),
        scratch_shapes=[pltpu.VMEM((rmax, D), BF16), pltpu.SemaphoreType.DMA(())],
        compiler_params=_cparams("arbitrary"),
    )(pstart, tab, lpt, wtok, x, g2, gf, ys)


def _moe(x, h2, lp, wsel, tab, filled, g2, l, w1, w3, w2, gf, final_norm):
    n = x.shape[0]
    tm = _row_tile(n, MOE_TILE)
    rmax = 2 * tm + N_EXPERTS * RUN
    worst = n // tm * (2 * tm + N_EXPERTS * (RUN - 1)) + N_EXPERTS * (MOE_BLOCK - RUN)
    cap = (worst + MOE_BLOCK - 1) // MOE_BLOCK * MOE_BLOCK
    nb = cap // MOE_BLOCK
    region = (filled + MOE_BLOCK - 1) // MOE_BLOCK * MOE_BLOCK
    pend = jnp.cumsum(region)
    pstart = (pend - region).astype(I32)
    blk_start = jnp.arange(nb, dtype=I32) * MOE_BLOCK
    blk_e = jnp.minimum(jnp.sum(pend[None, :] <= blk_start[:, None], axis=-1), N_EXPERTS - 1).astype(I32)
    nused = (pend[-1:] // MOE_BLOCK).astype(I32)
    tails = jnp.stack([pstart + filled, region - filled, jnp.broadcast_to(nused, (N_EXPERTS,))]).astype(I32)
    xs = _dispatch(h2, lp, tab, pstart, tails, cap, rmax)
    ys = _experts(xs, blk_e, nused, l, w1, w3, w2)
    return _combine(x, g2, lp.T, wsel.T, tab, pstart, ys, gf, final_norm, rmax)


def _rope_tables(n):
    rows = n // GRID_W
    row = jnp.repeat(jnp.arange(rows, dtype=F32), GRID_W)
    col = jnp.tile(jnp.arange(GRID_W, dtype=F32), rows)
    nf = DK // 4
    inv = ROPE_BASE ** (-jnp.arange(nf, dtype=F32) / nf)
    ang = jnp.concatenate([row[:, None] * inv, col[:, None] * inv], axis=-1)
    cos, sin = jnp.cos(ang), jnp.sin(ang)
    reps = LANES // DK
    return (jnp.tile(jnp.concatenate([cos, cos], -1), (1, reps)),
            jnp.tile(jnp.concatenate([-sin, sin], -1), (1, reps)))


def _layer_weights(l, w_in, w_gate, b_gate, w_branch, w_o, w_router, b_router):
    w = w_in[l].astype(BF16)
    wb = w_branch[l]
    wb2 = wb[2].reshape(HEADS, DV, D)
    wb2 = jnp.concatenate([wb2, jnp.zeros((HEADS, LANES - DV, D), F32)], axis=1).reshape(HEADS * LANES, D)
    merge_w = (w_gate[l].astype(BF16), b_gate[l].reshape(1, 4 * D), wb[0].astype(BF16), wb[1].astype(BF16),
               wb2.astype(BF16), wb[3].astype(BF16), w_o[l].astype(BF16),
               w_router.T, b_router.reshape(N_EXPERTS, 1))
    return w, merge_w


def kernel(x, c, ctx, c_ctx, w_mod, b_mod, g_norm1, g_norm2, w_in, conv_a_w, conv_a_b, conv_a_g, conv_a_beta, conv_b_w, lam_q1, lam_k1, lam_q2, lam_k2, diff_g, ret_ld_f, ret_ld_b, w_gate, b_gate, w_branch, w_o, w_router, b_router, w1_e, w3_e, w2_e, g_final):
    assert x.shape[0] == 1 and ctx.shape[0] == 1
    xl, xc = x[0], ctx[0]
    n_lat, n_ctx = xl.shape[0], xc.shape[0]
    mods = _adaln(c, c_ctx, w_mod, b_mod)
    cos, sinp = _rope_tables(n_lat)
    zc = jnp.zeros((n_ctx, LANES), F32)
    vone = jnp.zeros((HEADS, LANES), F32).at[:, DV].set(1.0).reshape(1, HEADS * LANES)
    gf = g_final.reshape(1, D)
    vrow = lambda a: a.reshape(1, -1)

    for l in range(DEPTH):
        last = l == DEPTH - 1
        lam_init = 0.8 - 0.6 * math.exp(-0.3 * l)
        ml = [mods[l, 0:1, j * D:(j + 1) * D] for j in range(6)]
        mc = [mods[l, 1:2, j * D:(j + 1) * D] for j in range(6)]
        w, merge_w = _layer_weights(l, w_in, w_gate, b_gate, w_branch, w_o, w_router, b_router)
        g1 = vrow(g_norm1[l])
        g2n = vrow(g_norm2[l])
        lam = (jnp.exp(jnp.sum(lam_q1[l] * lam_k1[l])) - jnp.exp(jnp.sum(lam_q2[l] * lam_k2[l]))
               + lam_init).reshape(1, 1).astype(F32)
        gpad = jnp.concatenate([diff_g[l], jnp.zeros((LANES - DV,), F32)]).reshape(1, LANES)
        ld = jnp.stack([ret_ld_f[l], ret_ld_b[l]]).astype(F32)
        conv_w = (conv_a_w[l], vrow(conv_a_b[l]), vrow(conv_a_g[l]), vrow(conv_a_beta[l]), conv_b_w[l])

        (hc, uc, bgc, cxc, qc, kc, vc, rqc, rkc, rvc, rgc) = _inproj(
            xc, g1, mc[0], mc[1], w, zc, zc, vone, rope=False)
        (hl, ul, bgl, cxl, ql, kl, vl, rql, rkl, rvl, rgl) = _inproj(
            xl, g1, ml[0], ml[1], w, cos, sinp, vone, rope=True)

        kall = jnp.concatenate([kc, kl], axis=0)
        vall = jnp.concatenate([vc, vl], axis=0)
        out_scale = 1.0 - lam_init
        y2l = _attention(ql, kall, vall, lam, gpad, out_scale)
        s0 = jnp.zeros((2, HEADS * DK, HEADS * DV), F32)
        y0c, y1c, rofc, robc, sfin = _convs_retention(uc, cxc, bgc, conv_w, rqc, rkc, rvc, ld, s0)
        y0l, y1l, rofl, robl, _ = _convs_retention(ul, cxl, bgl, conv_w, rql, rkl, rvl, ld, sfin)
        xl, h2l, lpl, wl, tabl, cl = _merge(hl, y0l, y1l, y2l, rofl, robl, rgl, xl, ml[2], g2n, ml[3], ml[4], merge_w)
        xl = _moe(xl, h2l, lpl, wl, tabl, cl[:, 0], ml[5], l, w1_e, w3_e, w2_e, gf, last)
        if not last:
            y2c = _attention(qc, kc, vc, lam, gpad, out_scale)
            xc, h2c, lpc, wc, tabc, cc = _merge(hc, y0c, y1c, y2c, rofc, robc, rgc, xc, mc[2], g2n, mc[3], mc[4], merge_w)
            xc = _moe(xc, h2c, lpc, wc, tabc, cc[:, 0], mc[5], l, w1_e, w3_e, w2_e, gf, False)
    return xl[None]
```

```python
import functools
import math

import numpy as np
import jax
import jax.numpy as jnp
from jax import lax
from jax.experimental import pallas as pl
from jax.experimental.pallas import tpu as pltpu

F32 = jnp.float32
BF16 = jnp.bfloat16
I32 = jnp.int32
HI = lax.Precision.HIGHEST

D = 1024
DEPTH = 2
GRID_W = 64
BW = 256
CONF_K = 31
SCONV_K = 3
HEADS = 4
DK = 32
DV = 64
CHUNK = 128
ROPE_BASE = 10000.0
N_EXPERTS = 16
N_GROUPS = 4
EPG = N_EXPERTS // N_GROUPS
D_FF = 512
MOE_BLOCK = 512
EPS = 1e-6
LOG2E = math.log2(math.e)
BOUND_SLACK = 1.0 + 2.0 ** -10
L_MIN = 2.0 ** -64
ATTN_TQ = 512
ATTN_TK = (3328, 1280, 256)
ATTN_TILE = 512 * 3328
RET_ROWS = 512
GATE_COLS = 512
MOE_TILE = 512
RUN = 16
HALO = 16
LANES = 128
SUBLANES = 8
W_COLS = 2816
VMEM_LIMIT = 56 * 1024 * 1024


def _cparams(*sem):
    return pltpu.CompilerParams(dimension_semantics=sem, vmem_limit_bytes=VMEM_LIMIT)


def _row_tile(n, pref):
    return pref if n % pref == 0 else n


def _mod_kernel(c_ref, w_ref, b_ref, o_ref):
    a = c_ref[...]
    a = a * jax.nn.sigmoid(a)
    o_ref[0] = jnp.dot(a, w_ref[0], preferred_element_type=F32, precision=HI) + b_ref[0]


def _adaln(c, c_ctx, w_mod, b_mod):
    cs = jnp.zeros((8, D), F32).at[0].set(c[0]).at[1].set(c_ctx)
    return pl.pallas_call(
        _mod_kernel,
        grid=(DEPTH, 6),
        in_specs=[pl.BlockSpec((8, D), lambda l, j: (0, 0)),
                  pl.BlockSpec((1, D, D), lambda l, j: (l, 0, j)),
                  pl.BlockSpec((1, 1, D), lambda l, j: (l, 0, j))],
        out_specs=pl.BlockSpec((1, 8, D), lambda l, j: (l, 0, j)),
        out_shape=jax.ShapeDtypeStruct((DEPTH, 8, 6 * D), F32),
        compiler_params=_cparams("parallel", "parallel"),
    )(cs, w_mod, b_mod.reshape(DEPTH, 1, 6 * D))


def _rope(x, cos, sinp):
    lane = lax.broadcasted_iota(I32, (1, LANES), 1)
    first = (lane % 32) < 16
    outs = []
    for c in range(x.shape[1] // LANES):
        xc = x[:, c * LANES:(c + 1) * LANES]
        sw = jnp.where(first, pltpu.roll(xc, LANES - 16, 1), pltpu.roll(xc, 16, 1))
        outs.append(xc * cos + sw * sinp)
    return outs[0] if len(outs) == 1 else jnp.concatenate(outs, axis=-1)


def _inproj_kernel(x_ref, g_ref, sh_ref, sc_ref, w_ref, cos_ref, sin_ref, vone_ref,
                   h_ref, u_ref, bg_ref, cx_ref, q_ref, k_ref, v_ref,
                   rq_ref, rk_ref, rv_ref, rg_ref, *, rope):
    x = x_ref[...]
    y = x * lax.rsqrt(jnp.mean(x * x, axis=-1, keepdims=True) + EPS) * g_ref[...]
    hb = (y * (1.0 + sc_ref[...]) + sh_ref[...]).astype(BF16)
    h_ref[...] = hb

    def seg(a, b):
        return jnp.dot(hb, w_ref[:, a:b], preferred_element_type=F32)

    z = seg(0, 512)
    u_ref[...] = z[:, :BW] * jax.nn.sigmoid(z[:, BW:])
    z = seg(512, 1280)
    bg_ref[...] = z[:, :BW]
    cx_ref[...] = z[:, BW:2 * BW] * z[:, 2 * BW:]
    zq = seg(1280, 1536)
    zk = seg(1536, 1792)
    zrq = seg(2048, 2176)
    zrk = seg(2176, 2304)
    if rope:
        cos = cos_ref[...]
        sinp = sin_ref[...]
        zq, zk = _rope(zq, cos, sinp), _rope(zk, cos, sinp)
        zrq, zrk = _rope(zrq, cos, sinp), _rope(zrk, cos, sinp)
    q_ref[...] = (zq * (DK ** -0.5 * LOG2E)).astype(BF16)
    k_ref[...] = zk.astype(BF16)
    vr = lax.broadcasted_iota(I32, (HEADS * DV, HEADS * LANES), 0)
    vc = lax.broadcasted_iota(I32, (HEADS * DV, HEADS * LANES), 1)
    spread = jnp.where((vc // LANES == vr // DV) & (vc % LANES == vr % DV), 1.0, 0.0).astype(BF16)
    zv = seg(1792, 2048).astype(BF16)
    v_ref[...] = (jnp.dot(zv, spread, preferred_element_type=F32) + vone_ref[...]).astype(BF16)
    rq_ref[...] = zrq
    rk_ref[...] = zrk * (DK ** -0.5)
    rv_ref[...] = seg(2304, 2560)
    rg_ref[...] = seg(2560, 2816)


def _inproj(x, g, shift, scale, w, cos, sinp, vone, rope):
    n = x.shape[0]
    tm = _row_tile(n, 512)
    row = lambda c: pl.BlockSpec((tm, c), lambda i: (i, 0))
    vec = lambda c: pl.BlockSpec((1, c), lambda i: (0, 0))
    widths = [(D, BF16), (BW, F32), (BW, F32), (BW, F32), (256, BF16), (256, BF16), (512, BF16),
              (128, F32), (128, F32), (256, F32), (256, F32)]
    return pl.pallas_call(
        functools.partial(_inproj_kernel, rope=rope),
        grid=(n // tm,),
        in_specs=[row(D), vec(D), vec(D), vec(D),
                  pl.BlockSpec((D, W_COLS), lambda i: (0, 0)),
                  row(LANES), row(LANES), vec(512)],
        out_specs=[row(c) for c, _ in widths],
        out_shape=[jax.ShapeDtypeStruct((n, c), dt) for c, dt in widths],
        compiler_params=_cparams("parallel"),
    )(x, g, shift, scale, w, cos, sinp, vone)


def _conv_kernel(up_ref, um_ref, un_ref, cp_ref, cm_ref, cn_ref, bg_ref,
                 wa_ref, ba_ref, ga_ref, bta_ref, wb_ref, y0_ref, y1_ref, eu, ec):
    i = pl.program_id(0)
    last = pl.num_programs(0) - 1
    tm = um_ref.shape[0]
    pad_a = (CONF_K - 1) // 2
    pad_b = (SCONV_K - 1) // 2
    span = tm + 2 * HALO - SUBLANES
    for ext, p_ref, m_ref, n_ref, shifts in ((eu, up_ref, um_ref, un_ref, range(1, SUBLANES)),
                                             (ec, cp_ref, cm_ref, cn_ref, (1, SUBLANES - 1))):
        ext[0, 0:HALO, :] = jnp.where(i > 0, p_ref[...], 0.0)
        ext[0, HALO:HALO + tm, :] = m_ref[...]
        ext[0, HALO + tm:, :] = jnp.where(i < last, n_ref[...], 0.0)
        for s in shifts:
            ext[s, 0:span, :] = ext[0, pl.ds(s, span), :]

    def tap(ext, r0, off):
        return ext[off % SUBLANES, pl.ds(r0 + off - off % SUBLANES, rc), :]

    rc = 32
    for r0 in range(0, tm, rc):
        acc = jnp.zeros((rc, BW), F32)
        for k in range(CONF_K):
            acc = acc + tap(eu, r0, HALO + k - pad_a) * wa_ref[k:k + 1, :]
        acc = acc + ba_ref[...]
        mu = jnp.mean(acc, axis=-1, keepdims=True)
        xc = acc - mu
        var = jnp.mean(xc * xc, axis=-1, keepdims=True)
        yn = xc * lax.rsqrt(var + EPS) * ga_ref[...] + bta_ref[...]
        y0_ref[pl.ds(r0, rc), :] = (yn * jax.nn.sigmoid(yn)).astype(BF16)
        accb = jnp.zeros((rc, BW), F32)
        for k in range(SCONV_K):
            accb = accb + tap(ec, r0, HALO + k - pad_b) * wb_ref[k:k + 1, :]
        y1_ref[pl.ds(r0, rc), :] = (bg_ref[pl.ds(r0, rc), :] * accb).astype(BF16)


def _attn_kernel(lam_ref, q_ref, k_ref, v_ref, g_ref, o_ref, kmax_sc, acc_sc, *, tk, nk, out_scale):
    h = pl.program_id(0)
    kc = HEADS * 2 * DK

    @pl.when((h == 0) & (pl.program_id(1) == 0))
    def _():
        def colmax(j, mx):
            blk = k_ref[pl.ds(pl.multiple_of(j * kc, kc), kc), :].astype(F32)
            return jnp.maximum(mx, jnp.max(jnp.abs(blk), axis=0, keepdims=True))
        kmax_sc[...] = lax.fori_loop(0, nk * tk // kc, colmax, jnp.zeros((1, kc), F32))

    q = q_ref[...]
    tq = q.shape[0]
    lane_q = lax.broadcasted_iota(I32, (1, kc), 1)
    lane_o = lax.broadcasted_iota(I32, (1, LANES), 1)
    qbound = jnp.abs(q.astype(F32)) * kmax_sc[...]
    qms, shifts = [], []
    for m in range(2):
        lo = (h * 2 + m) * DK
        sel = (lane_q >= lo) & (lane_q < lo + DK)
        qms.append(jnp.where(sel, q, jnp.zeros_like(q)))
        ub = jnp.sum(jnp.where(sel, qbound, 0.0), axis=-1, keepdims=True) * BOUND_SLACK
        shifts.append(jnp.tile(jnp.broadcast_to(ub, (tq, LANES)), (1, tk // LANES)))

    def tiles(j):
        start = j * tk if isinstance(j, int) else pl.multiple_of(j * tk, tk)
        return k_ref[pl.ds(start, tk), :], v_ref[pl.ds(start, tk), :]

    def scores(m, kj):
        return lax.dot_general(qms[m], kj, (((1,), (1,)), ((), ())), preferred_element_type=F32)

    acc_sc[...] = jnp.zeros(acc_sc.shape, F32)

    def shifted(j, carry):
        kj, vj = tiles(j)
        for m in range(2):
            p = jnp.exp2(scores(m, kj) - shifts[m]).astype(BF16)
            acc_sc[m] += jnp.dot(p, vj, preferred_element_type=F32)
        return carry

    for j in range(nk):
        shifted(j, 0)
    lmin = jnp.minimum(jnp.min(acc_sc[0][:, DV:DV + 1]), jnp.min(acc_sc[1][:, DV:DV + 1]))

    @pl.when(jnp.logical_not(lmin >= L_MIN))
    def _():
        def online(j, carry):
            kj, vj = tiles(j)
            new = []
            for m in range(2):
                mx, acc = carry[m]
                s = scores(m, kj)
                mnew = jnp.maximum(mx, jnp.max(s, axis=-1, keepdims=True))
                p = jnp.exp2(s - mnew).astype(BF16)
                new.append((mnew, jnp.exp2(mx - mnew) * acc + jnp.dot(p, vj, preferred_element_type=F32)))
            return tuple(new)

        init = (jnp.full((tq, 1), -jnp.inf, F32), jnp.zeros((tq, LANES), F32))
        res = lax.fori_loop(0, nk, online, (init, init))
        for m in range(2):
            acc_sc[m] = res[m][1]

    outs = [acc_sc[m] / acc_sc[m][:, DV:DV + 1] for m in range(2)]
    o = outs[0] - lam_ref[0, 0] * outs[1]
    valid = lane_o < DV
    ms = jnp.sum(jnp.where(valid, o * o, 0.0), axis=-1, keepdims=True) * (1.0 / DV)
    y = o * lax.rsqrt(ms + EPS) * g_ref[...] * out_scale
    o_ref[...] = jnp.where(valid, y, 0.0).astype(BF16)


def _attention(q, k, v, lam, gpad, out_scale):
    n = q.shape[0]
    nkeys = k.shape[0]
    tq = _row_tile(n, ATTN_TQ)
    tk = next(t for t in ATTN_TK if nkeys % t == 0 and t * tq <= ATTN_TILE)
    kern = functools.partial(_attn_kernel, tk=tk, nk=nkeys // tk, out_scale=out_scale)
    return pl.pallas_call(
        kern,
        grid=(HEADS, n // tq),
        in_specs=[pl.BlockSpec(memory_space=pltpu.SMEM),
                  pl.BlockSpec((tq, 256), lambda h, i: (i, 0)),
                  pl.BlockSpec((nkeys, 256), lambda h, i: (0, 0)),
                  pl.BlockSpec((nkeys, LANES), lambda h, i: (0, h)),
                  pl.BlockSpec((1, LANES), lambda h, i: (0, 0))],
        out_specs=pl.BlockSpec((tq, LANES), lambda h, i: (i, h)),
        out_shape=jax.ShapeDtypeStruct((n, HEADS * LANES), BF16),
        scratch_shapes=[pltpu.VMEM((1, HEADS * 2 * DK), F32), pltpu.VMEM((2, tq, LANES), F32)],
        compiler_params=_cparams("arbitrary", "arbitrary"),
    )(lam, q, k, v, gpad)


def _ret_kernel(ld_ref, qf_ref, kf_ref, vf_ref, qb_ref, kb_ref, vb_ref, s0_ref, of_ref, ob_ref, sf_ref,
                state, dmask, qdec, kdec, cdec, *, after_init=None):
    n = pl.program_id(0)
    c = CHUNK
    lane_q = lax.broadcasted_iota(I32, (1, HEADS * DK), 1)
    lane_v = lax.broadcasted_iota(I32, (1, HEADS * DV), 1)

    @pl.when(n == 0)
    def _():
        state[...] = s0_ref[...]
        row_h = lax.broadcasted_iota(I32, (HEADS * DK, 1), 0) // DK
        for d in range(2):
            pi = lax.broadcasted_iota(I32, (c, 1), 0).astype(F32)
            pj = lax.broadcasted_iota(I32, (1, c), 1).astype(F32)
            if d == 1:
                pi, pj = c - 1.0 - pi, c - 1.0 - pj
            diff = pi - pj
            lg_q = jnp.zeros((1, HEADS * DK), F32)
            lg_r = jnp.zeros((HEADS * DK, 1), F32)
            for h in range(HEADS):
                lg = ld_ref[d, h]
                dmask[d, h] = jnp.exp(jnp.where(diff >= 0, diff * lg, -jnp.inf))
                lg_q = jnp.where(lane_q // DK == h, lg, lg_q)
                lg_r = jnp.where(row_h == h, lg, lg_r)
            qdec[d] = jnp.exp((pi + 1.0) * lg_q)
            kdec[d] = jnp.exp((c - 1.0 - pi) * lg_q)
            cdec[d] = jnp.where(row_h == lane_v // DV, jnp.exp(c * lg_r), 0.0)

    if after_init is not None:
        after_init()

    for d, (q_ref, k_ref, v_ref, o_ref) in enumerate(((qf_ref, kf_ref, vf_ref, of_ref),
                                                      (qb_ref, kb_ref, vb_ref, ob_ref))):
        nsub = q_ref.shape[0] // c
        cd = cdec[d]
        s_cur = state[d]
        for sub in (range(nsub) if d == 0 else reversed(range(nsub))):
            rows = pl.ds(sub * c, c)
            q = q_ref[rows, :]
            k = k_ref[rows, :]
            kb = k.astype(BF16)
            vb = v_ref[rows, :].astype(BF16)
            o = jnp.dot((q * qdec[d]).astype(BF16), s_cur.astype(BF16), preferred_element_type=F32)
            for h in range(HEADS):
                qm = jnp.where(lane_q // DK == h, q, 0.0).astype(BF16)
                sc = lax.dot_general(qm, kb, (((1,), (1,)), ((), ())),
                                     preferred_element_type=F32) * dmask[d, h]
                oh = jnp.dot(sc.astype(BF16), vb, preferred_element_type=F32)
                o = o + jnp.where(lane_v // DV == h, oh, 0.0)
            o_ref[rows, :] = o
            kv = lax.dot_general((k * kdec[d]).astype(BF16), vb, (((0,), (0,)), ((), ())),
                                 preferred_element_type=F32)
            s_cur = cd * s_cur + jnp.where(cd != 0.0, kv, 0.0)
        state[d] = s_cur

    @pl.when(n == pl.num_programs(0) - 1)
    def _():
        sf_ref[...] = state[...]


N_CONV_IN, N_RET_IN, N_CONV_OUT, N_RET_OUT, N_CONV_SCR = 12, 8, 2, 3, 2


def _convret_kernel(*refs):
    cin, rin = refs[:N_CONV_IN], refs[N_CONV_IN:N_CONV_IN + N_RET_IN]
    outs = refs[N_CONV_IN + N_RET_IN:]
    cout, rout = outs[:N_CONV_OUT], outs[N_CONV_OUT:N_CONV_OUT + N_RET_OUT]
    scr = outs[N_CONV_OUT + N_RET_OUT:]
    _ret_kernel(*rin, *rout, *scr[N_CONV_SCR:],
                after_init=lambda: _conv_kernel(*cin, *cout, *scr[:N_CONV_SCR]))


def _convs_retention(u, cx, bg, conv_w, rq, rk, rv, ld, s0):
    n = u.shape[0]
    rows = _row_tile(n, RET_ROWS)
    nc = n // rows
    hb = rows // HALO
    nh = n // HALO
    prev = pl.BlockSpec((HALO, BW), lambda i: (jnp.maximum(i * hb - 1, 0), 0))
    nxt = pl.BlockSpec((HALO, BW), lambda i: (jnp.minimum((i + 1) * hb, nh - 1), 0))
    vec = lambda r: pl.BlockSpec((r, BW), lambda i: (0, 0))
    fwd = lambda w: pl.BlockSpec((rows, w), lambda i: (i, 0))
    bwd = lambda w: pl.BlockSpec((rows, w), lambda i: (nc - 1 - i, 0))
    sshape = (2, HEADS * DK, HEADS * DV)
    whole = pl.BlockSpec(sshape, lambda i: (0, 0, 0))
    dq, dv = HEADS * DK, HEADS * DV
    main = fwd(BW)
    return pl.pallas_call(
        _convret_kernel,
        grid=(nc,),
        in_specs=[prev, main, nxt, prev, main, nxt, main,
                  vec(CONF_K), vec(1), vec(1), vec(1), vec(SCONV_K),
                  pl.BlockSpec(memory_space=pltpu.SMEM),
                  fwd(dq), fwd(dq), fwd(dv), bwd(dq), bwd(dq), bwd(dv), whole],
        out_specs=[main, main, fwd(dv), bwd(dv), whole],
        out_shape=[jax.ShapeDtypeStruct((n, BW), BF16), jax.ShapeDtypeStruct((n, BW), BF16),
                   jax.ShapeDtypeStruct((n, dv), F32), jax.ShapeDtypeStruct((n, dv), F32),
                   jax.ShapeDtypeStruct(sshape, F32)],
        scratch_shapes=[pltpu.VMEM((SUBLANES, rows + 2 * HALO, BW), F32)] * N_CONV_SCR
                       + [pltpu.VMEM(sshape, F32),
                          pltpu.VMEM((2, HEADS, CHUNK, CHUNK), F32),
                          pltpu.VMEM((2, CHUNK, dq), F32),
                          pltpu.VMEM((2, CHUNK, dq), F32),
                          pltpu.VMEM(sshape, F32)],
        compiler_params=_cparams("arbitrary"),
    )(u, u, u, cx, cx, cx, bg, *conv_w, ld, rq, rk, rv, rq, rk, rv, s0)


def _first_max(vals):
    idx = jnp.zeros(vals[0].shape, I32)
    best = vals[0]
    for j in range(1, len(vals)):
        upd = vals[j] > best
        idx = jnp.where(upd, j, idx)
        best = jnp.where(upd, vals[j], best)
    return idx, best


def _split_bf16(a):
    hi = a.astype(BF16)
    return hi, (a - hi.astype(F32)).astype(BF16)


def _pick(vals, idx):
    out = vals[-1]
    for j in range(len(vals) - 2, -1, -1):
        out = jnp.where(idx == j, vals[j], out)
    return out


def _merge_kernel(h_ref, y0_ref, y1_ref, y2_ref, rof_ref, rob_ref, rg_ref, x_ref, g1_ref, gn_ref, sh_ref, sc_ref,
                  wg_ref, bgate_ref, wb0_ref, wb1_ref, wb2_ref, wb3_ref, wo_ref, wr_ref, br_ref,
                  xo_ref, h2_ref, lp_ref, w_ref, tab_ref, cnt_ref, base):
    i = pl.program_id(0)
    tm = x_ref.shape[0]

    @pl.when(i == 0)
    def _():
        base[...] = jnp.zeros_like(base)

    ro = rof_ref[...] + rob_ref[...]
    gr = lax.broadcasted_iota(I32, (HEADS * DV, HEADS * DV), 0) // DV
    gc = lax.broadcasted_iota(I32, (HEADS * DV, HEADS * DV), 1) // DV
    avg = jnp.where(gr == gc, 1.0 / DV, 0.0).astype(BF16)

    def head_mean(a):
        hi, lo = _split_bf16(a)
        return (jnp.dot(hi, avg, preferred_element_type=F32) + jnp.dot(lo, avg, preferred_element_type=F32))

    mu = head_mean(ro)
    xc = ro - mu
    var = head_mean(xc * xc)
    rg = rg_ref[...]
    y3 = (rg * jax.nn.sigmoid(rg) * (xc * lax.rsqrt(var + EPS))).astype(BF16)

    hb = h_ref[...]
    ys = (y0_ref[...], y1_ref[...], y2_ref[...], y3)
    wbs = (wb0_ref, wb1_ref, wb2_ref, wb3_ref)
    parts = []
    for c0 in range(0, D, GATE_COLS):
        mc = None
        for b in range(4):
            lo = b * D + c0
            gl = jnp.dot(hb, wg_ref[:, lo:lo + GATE_COLS], preferred_element_type=F32)
            gate = jax.nn.sigmoid(gl + bgate_ref[:, lo:lo + GATE_COLS])
            t = gate * jnp.dot(ys[b], wbs[b][:, c0:c0 + GATE_COLS], preferred_element_type=F32)
            mc = t if mc is None else mc + t
        parts.append(mc.astype(BF16))
    yo = jnp.dot(jnp.concatenate(parts, axis=-1), wo_ref[...], preferred_element_type=F32)
    xn = x_ref[...] + g1_ref[...] * yo
    xo_ref[...] = xn
    yn = xn * lax.rsqrt(jnp.mean(xn * xn, axis=-1, keepdims=True) + EPS) * gn_ref[...]
    h2 = yn * (1.0 + sc_ref[...]) + sh_ref[...]
    h2_ref[...] = h2.astype(BF16)

    h_hi, h_lo = _split_bf16(h2)
    w_hi, w_lo = _split_bf16(wr_ref[...])
    nt = (((1,), (1,)), ((), ()))
    l2 = lax.dot_general(jnp.concatenate([w_hi, w_lo], axis=0), h_hi, nt, preferred_element_type=F32)
    lt = (l2[:N_EXPERTS] + l2[N_EXPERTS:]) + lax.dot_general(w_hi, h_lo, nt, preferred_element_type=F32)
    s = jax.nn.sigmoid(lt)
    sb = s + br_ref[...]
    r = [sb[e:e + 1, :] for e in range(N_EXPERTS)]
    sr = [s[e:e + 1, :] for e in range(N_EXPERTS)]
    gscore = []
    for g in range(N_GROUPS):
        a, b, c, d = r[EPG * g:EPG * (g + 1)]
        hi1, lo1, hi2, lo2 = jnp.maximum(a, b), jnp.minimum(a, b), jnp.maximum(c, d), jnp.minimum(c, d)
        gscore.append(jnp.maximum(hi1, hi2) + jnp.maximum(jnp.minimum(hi1, hi2), jnp.maximum(lo1, lo2)))
    gsel, _ = _first_max(gscore)
    v = [_pick([r[EPG * g + j] for g in range(N_GROUPS)], gsel) for j in range(EPG)]
    sv = [_pick([sr[EPG * g + j] for g in range(N_GROUPS)], gsel) for j in range(EPG)]
    i1, _ = _first_max(v)
    i2, _ = _first_max([jnp.where(i1 == j, -jnp.inf, v[j]) for j in range(EPG)])
    w1 = _pick(sv, i1)
    w2 = _pick(sv, i2)
    den = w1 + w2
    e1 = gsel * EPG + i1
    e2 = gsel * EPG + i2
    w_ref[0:1, :] = w1 / den
    w_ref[1:2, :] = w2 / den

    eio = lax.broadcasted_iota(I32, (N_EXPERTS, tm), 0)
    oh1 = eio == e1
    oh2 = eio == e2
    cnt = oh1.astype(F32) + oh2.astype(F32)
    ti = lax.broadcasted_iota(I32, (tm, tm), 0)
    tj = lax.broadcasted_iota(I32, (tm, tm), 1)
    before = jnp.where(ti < tj, 1.0, 0.0).astype(BF16)
    pref = jnp.dot(cnt.astype(BF16), before, preferred_element_type=F32)
    plen = jnp.floor((jnp.sum(cnt, axis=1, keepdims=True) + (RUN - 1.0)) * (1.0 / RUN)) * RUN
    eio1 = lax.broadcasted_iota(I32, (N_EXPERTS, 1), 0)
    loff = jnp.zeros((N_EXPERTS, 1), F32)
    run_start = jnp.zeros((1, 1), F32)
    for ex in range(N_EXPERTS):
        loff = jnp.where(eio1 == ex, run_start, loff)
        run_start = run_start + plen[ex:ex + 1, :]
    start = loff + pref
    lp_ref[0:1, :] = jnp.sum(jnp.where(oh1, start, 0.0), axis=0, keepdims=True).astype(I32)
    lp_ref[1:2, :] = jnp.sum(jnp.where(oh2, start, 0.0), axis=0, keepdims=True).astype(I32)
    goff = base[...]
    lane = lax.broadcasted_iota(I32, (1, LANES), 1)
    tab_ref[...] = jnp.where(lane == 0, plen, jnp.where(lane == 1, goff, 0.0)).astype(I32)
    base[...] = goff + plen
    cnt_ref[...] = jnp.broadcast_to(goff + plen, (N_EXPERTS, LANES)).astype(I32)


def _merge(h, y0, y1, y2, rof, rob, rg, x, g1, gn, sh, sc, wts):
    n = x.shape[0]
    tm = _row_tile(n, MOE_TILE)
    row = lambda c: pl.BlockSpec((tm, c), lambda i: (i, 0))
    full = lambda a: pl.BlockSpec(a.shape, lambda i: (0,) * a.ndim)
    tok = pl.BlockSpec((2, tm), lambda i: (0, i))
    return pl.pallas_call(
        _merge_kernel,
        grid=(n // tm,),
        in_specs=[row(D), row(BW), row(BW), row(HEADS * LANES),
                  row(HEADS * DV), row(HEADS * DV), row(BW), row(D),
                  full(g1), full(gn), full(sh), full(sc)] + [full(a) for a in wts],
        out_specs=[row(D), row(D), tok, tok,
                   pl.BlockSpec((N_EXPERTS, LANES), lambda i: (i, 0)),
                   pl.BlockSpec((N_EXPERTS, LANES), lambda i: (0, 0))],
        out_shape=[jax.ShapeDtypeStruct((n, D), F32), jax.ShapeDtypeStruct((n, D), BF16),
                   jax.ShapeDtypeStruct((2, n), I32), jax.ShapeDtypeStruct((2, n), F32),
                   jax.ShapeDtypeStruct((n // tm * N_EXPERTS, LANES), I32),
                   jax.ShapeDtypeStruct((N_EXPERTS, LANES), I32)],
        scratch_shapes=[pltpu.VMEM((N_EXPERTS, 1), F32)],
        compiler_params=_cparams("arbitrary"),
    )(h, y0, y1, y2, rof, rob, rg, x, g1, gn, sh, sc, *wts)


def _for_run_pieces(tab_ref, ps_ref, fn):
    loff = 0
    for ex in range(N_EXPERTS):
        plen = tab_ref[ex, 0]
        gbase = ps_ref[ex] + tab_ref[ex, 1]

        def piece(c, carry, loff=loff, gbase=gbase):
            fn(pl.multiple_of(loff + c * RUN, RUN), pl.multiple_of(gbase + c * RUN, RUN))
            return carry

        lax.fori_loop(0, plen // RUN, piece, 0)
        loff = loff + plen


def _dispatch_kernel(ps_ref, tab_ref, tail_ref, lp_ref, x_ref, xs_ref, buf, sem):
    rmax, tm = buf.shape[0], x_ref.shape[0]
    lp = lp_ref[...]
    r = lax.broadcasted_iota(I32, (rmax, tm), 0)
    sel = jnp.where((r == lp[0:1, :]) | (r == lp[1:2, :]), 1.0, 0.0).astype(BF16)
    buf[...] = jnp.dot(sel, x_ref[...], preferred_element_type=F32).astype(BF16)

    def copy(s, t):
        return pltpu.make_async_copy(buf.at[pl.ds(s, RUN)], xs_ref.at[pl.ds(t, RUN)], sem)

    _for_run_pieces(tab_ref, ps_ref, lambda s, t: copy(s, t).start())
    _for_run_pieces(tab_ref, ps_ref, lambda s, t: copy(s, t).wait())

    @pl.when(pl.program_id(0) == pl.num_programs(0) - 1)
    def _():
        buf[0:MOE_BLOCK, :] = jnp.zeros((MOE_BLOCK, D), BF16)

        def tails(fn):
            for ex in range(N_EXPERTS):
                first = tail_ref[0, ex]

                def piece(c, carry, first=first):
                    fn(pl.multiple_of(first + c * RUN, RUN))
                    return carry

                lax.fori_loop(0, tail_ref[1, ex] // RUN, piece, 0)

        def block_copy(b):
            dst = xs_ref.at[pl.ds(pl.multiple_of(b * MOE_BLOCK, MOE_BLOCK), MOE_BLOCK)]
            return pltpu.make_async_copy(buf.at[pl.ds(0, MOE_BLOCK)], dst, sem)

        def unused(fn):
            def blk(b, carry):
                fn(b)
                return carry
            lax.fori_loop(tail_ref[2, 0], xs_ref.shape[0] // MOE_BLOCK, blk, 0)

        tails(lambda t: copy(0, t).start())
        unused(lambda b: block_copy(b).start())
        tails(lambda t: copy(0, t).wait())
        unused(lambda b: block_copy(b).wait())


def _tile_tables(tm):
    return [pl.BlockSpec(memory_space=pltpu.SMEM),
            pl.BlockSpec((N_EXPERTS, LANES), lambda i: (i, 0), memory_space=pltpu.SMEM)]


def _dispatch(h2, lp, tab, pstart, tails, cap, rmax):
    n = h2.shape[0]
    tm = _row_tile(n, MOE_TILE)
    return pl.pallas_call(
        _dispatch_kernel,
        grid=(n // tm,),
        in_specs=_tile_tables(tm) + [pl.BlockSpec(memory_space=pltpu.SMEM),
                                     pl.BlockSpec((2, tm), lambda i: (0, i)),
                                     pl.BlockSpec((tm, D), lambda i: (i, 0))],
        out_specs=pl.BlockSpec(memory_space=pl.ANY),
        out_shape=jax.ShapeDtypeStruct((cap, D), BF16),
        scratch_shapes=[pltpu.VMEM((rmax, D), BF16), pltpu.SemaphoreType.DMA(())],
        compiler_params=_cparams("arbitrary"),
    )(pstart, tab, tails, lp, h2)


def _expert_kernel(be_ref, nu_ref, x_ref, w1_ref, w3_ref, w2_ref, y_ref, w1b, w3b, w2b):
    b = pl.program_id(0)

    @pl.when((b == 0) | (be_ref[b] != be_ref[jnp.maximum(b - 1, 0)]))
    def _():
        w1b[...] = w1_ref[0, 0].astype(BF16)
        w3b[...] = w3_ref[0, 0].astype(BF16)
        w2b[...] = w2_ref[0, 0].astype(BF16)

    @pl.when(b < nu_ref[0])
    def _():
        x = x_ref[...]
        a = jnp.dot(x, w1b[...], preferred_element_type=F32)
        c = jnp.dot(x, w3b[...], preferred_element_type=F32)
        u = (a * jax.nn.sigmoid(a) * c).astype(BF16)
        y_ref[...] = jnp.dot(u, w2b[...], preferred_element_type=F32).astype(BF16)

    @pl.when(b >= nu_ref[0])
    def _():
        y_ref[...] = jnp.zeros_like(y_ref)


def _experts(xs, blk_e, nused, l, w1, w3, w2):
    cap = xs.shape[0]
    nb = cap // MOE_BLOCK
    xmap = lambda b, be, nu: (jnp.minimum(b, jnp.maximum(nu[0] - 1, 0)), 0)
    wmap = lambda b, be, nu: (l, be[b], 0, 0)
    return pl.pallas_call(
        _expert_kernel,
        grid_spec=pltpu.PrefetchScalarGridSpec(
            num_scalar_prefetch=2,
            grid=(nb,),
            in_specs=[pl.BlockSpec((MOE_BLOCK, D), xmap),
                      pl.BlockSpec((1, 1, D, D_FF), wmap),
                      pl.BlockSpec((1, 1, D, D_FF), wmap),
                      pl.BlockSpec((1, 1, D_FF, D), wmap)],
            out_specs=pl.BlockSpec((MOE_BLOCK, D), lambda b, be, nu: (b, 0)),
            scratch_shapes=[pltpu.VMEM((D, D_FF), BF16), pltpu.VMEM((D, D_FF), BF16),
                            pltpu.VMEM((D_FF, D), BF16)]),
        out_shape=jax.ShapeDtypeStruct((cap, D), BF16),
        compiler_params=_cparams("arbitrary"),
    )(blk_e, nused, xs, w1, w3, w2)


def _combine_kernel(ps_ref, tab_ref, lp_ref, w_ref, x_ref, g2_ref, gf_ref, ys_ref, o_ref, buf, sem, *,
                    final_norm):
    rmax, tm = buf.shape[0], x_ref.shape[0]

    @pl.when(pl.program_id(0) == 0)
    def _():
        buf[...] = jnp.zeros_like(buf)

    def copy(s, t):
        return pltpu.make_async_copy(ys_ref.at[pl.ds(t, RUN)], buf.at[pl.ds(s, RUN)], sem)

    _for_run_pieces(tab_ref, ps_ref, lambda s, t: copy(s, t).start())
    lp = lp_ref[...]
    w = w_ref[...]
    r = lax.broadcasted_iota(I32, (tm, rmax), 1)
    mix = (jnp.where(r == lp[:, 0:1], w[:, 0:1], 0.0)
           + jnp.where(r == lp[:, 1:2], w[:, 1:2], 0.0)).astype(BF16)
    _for_run_pieces(tab_ref, ps_ref, lambda s, t: copy(s, t).wait())
    ml = jnp.dot(mix, buf[...], preferred_element_type=F32)
    xo = x_ref[...] + g2_ref[...] * ml
    if final_norm:
        xo = xo * lax.rsqrt(jnp.mean(xo * xo, axis=-1, keepdims=True) + EPS) * gf_ref[...]
    o_ref[...] = xo


def _combine(x, g2, lpt, wtok, tab, pstart, ys, gf, final_norm, rmax):
    n = x.shape[0]
    tm = _row_tile(n, MOE_TILE)
    vec = pl.BlockSpec((1, D), lambda i: (0, 0))
    tok = pl.BlockSpec((tm, 2), lambda i: (i, 0))
    return pl.pallas_call(
        functools.partial(_combine_kernel, final_norm=final_norm),
        grid=(n // tm,),
        in_specs=_tile_tables(tm) + [tok, tok, pl.BlockSpec((tm, D), lambda i: (i, 0)), vec, vec,
                                     pl.BlockSpec(memory_space=pl.ANY)],
        out_specs=pl.BlockSpec((tm, D), lambda i: (i, 0)),
        out_shape=jax.ShapeDtypeStruct((n, D), F32),
        scratch_shapes=[pltpu.VMEM((rmax, D), BF16), pltpu.SemaphoreType.DMA(())],
        compiler_params=_cparams("arbitrary"),
    )(pstart, tab, lpt, wtok, x, g2, gf, ys)


def _moe(x, h2, lp, wsel, tab, filled, g2, l, w1, w3, w2, gf, final_norm):
    n = x.shape[0]
    tm = _row_tile(n, MOE_TILE)
    rmax = 2 * tm + N_EXPERTS * RUN
    worst = n // tm * (2 * tm + N_EXPERTS * (RUN - 1)) + N_EXPERTS * (MOE_BLOCK - RUN)
    cap = (worst + MOE_BLOCK - 1) // MOE_BLOCK * MOE_BLOCK
    nb = cap // MOE_BLOCK
    region = (filled + MOE_BLOCK - 1) // MOE_BLOCK * MOE_BLOCK
    pend = jnp.cumsum(region)
    pstart = (pend - region).astype(I32)
    blk_start = jnp.arange(nb, dtype=I32) * MOE_BLOCK
    blk_e = jnp.minimum(jnp.sum(pend[None, :] <= blk_start[:, None], axis=-1), N_EXPERTS - 1).astype(I32)
    nused = (pend[-1:] // MOE_BLOCK).astype(I32)
    tails = jnp.stack([pstart + filled, region - filled, jnp.broadcast_to(nused, (N_EXPERTS,))]).astype(I32)
    xs = _dispatch(h2, lp, tab, pstart, tails, cap, rmax)
    ys = _experts(xs, blk_e, nused, l, w1, w3, w2)
    return _combine(x, g2, lp.T, wsel.T, tab, pstart, ys, gf, final_norm, rmax)


def _rope_tables(n):
    rows = n // GRID_W
    row = jnp.repeat(jnp.arange(rows, dtype=F32), GRID_W)
    col = jnp.tile(jnp.arange(GRID_W, dtype=F32), rows)
    nf = DK // 4
    inv = ROPE_BASE ** (-jnp.arange(nf, dtype=F32) / nf)
    ang = jnp.concatenate([row[:, None] * inv, col[:, None] * inv], axis=-1)
    cos, sin = jnp.cos(ang), jnp.sin(ang)
    reps = LANES // DK
    return (jnp.tile(jnp.concatenate([cos, cos], -1), (1, reps)),
            jnp.tile(jnp.concatenate([-sin, sin], -1), (1, reps)))


def _layer_weights(l, w_in, w_gate, b_gate, w_branch, w_o, w_router, b_router):
    w = w_in[l].astype(BF16)
    wb = w_branch[l]
    wb2 = wb[2].reshape(HEADS, DV, D)
    wb2 = jnp.concatenate([wb2, jnp.zeros((HEADS, LANES - DV, D), F32)], axis=1).reshape(HEADS * LANES, D)
    merge_w = (w_gate[l].astype(BF16), b_gate[l].reshape(1, 4 * D), wb[0].astype(BF16), wb[1].astype(BF16),
               wb2.astype(BF16), wb[3].astype(BF16), w_o[l].astype(BF16),
               w_router.T, b_router.reshape(N_EXPERTS, 1))
    return w, merge_w


def kernel(x, c, ctx, c_ctx, w_mod, b_mod, g_norm1, g_norm2, w_in, conv_a_w, conv_a_b, conv_a_g, conv_a_beta, conv_b_w, lam_q1, lam_k1, lam_q2, lam_k2, diff_g, ret_ld_f, ret_ld_b, w_gate, b_gate, w_branch, w_o, w_router, b_router, w1_e, w3_e, w2_e, g_final):
    assert x.shape[0] == 1 and ctx.shape[0] == 1
    xl, xc = x[0], ctx[0]
    n_lat, n_ctx = xl.shape[0], xc.shape[0]
    mods = _adaln(c, c_ctx, w_mod, b_mod)
    cos, sinp = _rope_tables(n_lat)
    zc = jnp.zeros((n_ctx, LANES), F32)
    vone = jnp.zeros((HEADS, LANES), F32).at[:, DV].set(1.0).reshape(1, HEADS * LANES)
    gf = g_final.reshape(1, D)
    vrow = lambda a: a.reshape(1, -1)

    for l in range(DEPTH):
        last = l == DEPTH - 1
        lam_init = 0.8 - 0.6 * math.exp(-0.3 * l)
        ml = [mods[l, 0:1, j * D:(j + 1) * D] for j in range(6)]
        mc = [mods[l, 1:2, j * D:(j + 1) * D] for j in range(6)]
        w, merge_w = _layer_weights(l, w_in, w_gate, b_gate, w_branch, w_o, w_router, b_router)
        g1 = vrow(g_norm1[l])
        g2n = vrow(g_norm2[l])
        lam = (jnp.exp(jnp.sum(lam_q1[l] * lam_k1[l])) - jnp.exp(jnp.sum(lam_q2[l] * lam_k2[l]))
               + lam_init).reshape(1, 1).astype(F32)
        gpad = jnp.concatenate([diff_g[l], jnp.zeros((LANES - DV,), F32)]).reshape(1, LANES)
        ld = jnp.stack([ret_ld_f[l], ret_ld_b[l]]).astype(F32)
        conv_w = (conv_a_w[l], vrow(conv_a_b[l]), vrow(conv_a_g[l]), vrow(conv_a_beta[l]), conv_b_w[l])

        (hc, uc, bgc, cxc, qc, kc, vc, rqc, rkc, rvc, rgc) = _inproj(
            xc, g1, mc[0], mc[1], w, zc, zc, vone, rope=False)
        (hl, ul, bgl, cxl, ql, kl, vl, rql, rkl, rvl, rgl) = _inproj(
            xl, g1, ml[0], ml[1], w, cos, sinp, vone, rope=True)

        kall = jnp.concatenate([kc, kl], axis=0)
        vall = jnp.concatenate([vc, vl], axis=0)
        out_scale = 1.0 - lam_init
        y2l = _attention(ql, kall, vall, lam, gpad, out_scale)
        s0 = jnp.zeros((2, HEADS * DK, HEADS * DV), F32)
        y0c, y1c, rofc, robc, sfin = _convs_retention(uc, cxc, bgc, conv_w, rqc, rkc, rvc, ld, s0)
        y0l, y1l, rofl, robl, _ = _convs_retention(ul, cxl, bgl, conv_w, rql, rkl, rvl, ld, sfin)
        xl, h2l, lpl, wl, tabl, cl = _merge(hl, y0l, y1l, y2l, rofl, robl, rgl, xl, ml[2], g2n, ml[3], ml[4], merge_w)
        xl = _moe(xl, h2l, lpl, wl, tabl, cl[:, 0], ml[5], l, w1_e, w3_e, w2_e, gf, last)
        if not last:
            y2c = _attention(qc, kc, vc, lam, gpad, out_scale)
            xc, h2c, lpc, wc, tabc, cc = _merge(hc, y0c, y1c, y2c, rofc, robc, rgc, xc, mc[2], g2n, mc[3], mc[4], merge_w)
            xc = _moe(xc, h2c, lpc, wc, tabc, cc[:, 0], mc[5], l, w1_e, w3_e, w2_e, gf, False)
    return xl[None]
```

```python
import functools
import math

import numpy as np
import jax
import jax.numpy as jnp
from jax import lax
from jax.experimental import pallas as pl
from jax.experimental.pallas import tpu as pltpu

F32 = jnp.float32
BF16 = jnp.bfloat16
I32 = jnp.int32
HI = lax.Precision.HIGHEST

D = 1024
DEPTH = 2
GRID_W = 64
BW = 256
CONF_K = 31
SCONV_K = 3
HEADS = 4
DK = 32
DV = 64
CHUNK = 128
ROPE_BASE = 10000.0
N_EXPERTS = 16
N_GROUPS = 4
EPG = N_EXPERTS // N_GROUPS
D_FF = 512
MOE_BLOCK = 512
EPS = 1e-6
LOG2E = math.log2(math.e)
BOUND_SLACK = 1.0 + 2.0 ** -10
L_MIN = 2.0 ** -64
ATTN_TQ = 512
ATTN_SUB = 2
ATTN_TK = (3328, 1280, 256)
ATTN_TILE = 512 * 3328
RET_ROWS = 512
GATE_COLS = 512
MOE_TILE = 512
RUN = 16
HALO = 16
LANES = 128
SUBLANES = 8
W_COLS = 2816
VMEM_LIMIT = 56 * 1024 * 1024


def _cparams(*sem):
    return pltpu.CompilerParams(dimension_semantics=sem, vmem_limit_bytes=VMEM_LIMIT)


def _row_tile(n, pref):
    return pref if n % pref == 0 else n


def _mod_kernel(c_ref, w_ref, b_ref, o_ref):
    a = c_ref[...]
    a = a * jax.nn.sigmoid(a)
    o_ref[0] = jnp.dot(a, w_ref[0], preferred_element_type=F32, precision=HI) + b_ref[0]


def _adaln(c, c_ctx, w_mod, b_mod):
    cs = jnp.zeros((8, D), F32).at[0].set(c[0]).at[1].set(c_ctx)
    return pl.pallas_call(
        _mod_kernel,
        grid=(DEPTH, 6),
        in_specs=[pl.BlockSpec((8, D), lambda l, j: (0, 0)),
                  pl.BlockSpec((1, D, D), lambda l, j: (l, 0, j)),
                  pl.BlockSpec((1, 1, D), lambda l, j: (l, 0, j))],
        out_specs=pl.BlockSpec((1, 8, D), lambda l, j: (l, 0, j)),
        out_shape=jax.ShapeDtypeStruct((DEPTH, 8, 6 * D), F32),
        compiler_params=_cparams("parallel", "parallel"),
    )(cs, w_mod, b_mod.reshape(DEPTH, 1, 6 * D))


def _rope(x, cos, sinp):
    lane = lax.broadcasted_iota(I32, (1, LANES), 1)
    first = (lane % 32) < 16
    outs = []
    for c in range(x.shape[1] // LANES):
        xc = x[:, c * LANES:(c + 1) * LANES]
        sw = jnp.where(first, pltpu.roll(xc, LANES - 16, 1), pltpu.roll(xc, 16, 1))
        outs.append(xc * cos + sw * sinp)
    return outs[0] if len(outs) == 1 else jnp.concatenate(outs, axis=-1)


def _inproj_kernel(x_ref, g_ref, sh_ref, sc_ref, w_ref, cos_ref, sin_ref, vone_ref,
                   h_ref, u_ref, bg_ref, cx_ref, q_ref, k_ref, v_ref,
                   rq_ref, rk_ref, rv_ref, rg_ref, *, rope):
    x = x_ref[...]
    y = x * lax.rsqrt(jnp.mean(x * x, axis=-1, keepdims=True) + EPS) * g_ref[...]
    hb = (y * (1.0 + sc_ref[...]) + sh_ref[...]).astype(BF16)
    h_ref[...] = hb

    def seg(a, b):
        return jnp.dot(hb, w_ref[:, a:b], preferred_element_type=F32)

    z = seg(0, 512)
    u_ref[...] = z[:, :BW] * jax.nn.sigmoid(z[:, BW:])
    z = seg(512, 1280)
    bg_ref[...] = z[:, :BW]
    cx_ref[...] = z[:, BW:2 * BW] * z[:, 2 * BW:]
    zq = seg(1280, 1536)
    zk = seg(1536, 1792)
    zrq = seg(2048, 2176)
    zrk = seg(2176, 2304)
    if rope:
        cos = cos_ref[...]
        sinp = sin_ref[...]
        zq, zk = _rope(zq, cos, sinp), _rope(zk, cos, sinp)
        zrq, zrk = _rope(zrq, cos, sinp), _rope(zrk, cos, sinp)
    q_ref[...] = (zq * (DK ** -0.5 * LOG2E)).astype(BF16)
    k_ref[...] = zk.astype(BF16)
    vr = lax.broadcasted_iota(I32, (HEADS * DV, HEADS * LANES), 0)
    vc = lax.broadcasted_iota(I32, (HEADS * DV, HEADS * LANES), 1)
    spread = jnp.where((vc // LANES == vr // DV) & (vc % LANES == vr % DV), 1.0, 0.0).astype(BF16)
    zv = seg(1792, 2048).astype(BF16)
    v_ref[...] = (jnp.dot(zv, spread, preferred_element_type=F32) + vone_ref[...]).astype(BF16)
    rq_ref[...] = zrq
    rk_ref[...] = zrk * (DK ** -0.5)
    rv_ref[...] = seg(2304, 2560)
    rg_ref[...] = seg(2560, 2816)


def _inproj(x, g, shift, scale, w, cos, sinp, vone, rope):
    n = x.shape[0]
    tm = _row_tile(n, 512)
    row = lambda c: pl.BlockSpec((tm, c), lambda i: (i, 0))
    vec = lambda c: pl.BlockSpec((1, c), lambda i: (0, 0))
    widths = [(D, BF16), (BW, F32), (BW, F32), (BW, F32), (256, BF16), (256, BF16), (512, BF16),
              (128, F32), (128, F32), (256, F32), (256, F32)]
    return pl.pallas_call(
        functools.partial(_inproj_kernel, rope=rope),
        grid=(n // tm,),
        in_specs=[row(D), vec(D), vec(D), vec(D),
                  pl.BlockSpec((D, W_COLS), lambda i: (0, 0)),
                  row(LANES), row(LANES), vec(512)],
        out_specs=[row(c) for c, _ in widths],
        out_shape=[jax.ShapeDtypeStruct((n, c), dt) for c, dt in widths],
        compiler_params=_cparams("parallel"),
    )(x, g, shift, scale, w, cos, sinp, vone)


def _conv_kernel(up_ref, um_ref, un_ref, cp_ref, cm_ref, cn_ref, bg_ref,
                 wa_ref, ba_ref, ga_ref, bta_ref, wb_ref, y0_ref, y1_ref, eu, ec):
    i = pl.program_id(0)
    last = pl.num_programs(0) - 1
    tm = um_ref.shape[0]
    pad_a = (CONF_K - 1) // 2
    pad_b = (SCONV_K - 1) // 2
    span = tm + 2 * HALO - SUBLANES
    for ext, p_ref, m_ref, n_ref, shifts in ((eu, up_ref, um_ref, un_ref, range(1, SUBLANES)),
                                             (ec, cp_ref, cm_ref, cn_ref, (1, SUBLANES - 1))):
        ext[0, 0:HALO, :] = jnp.where(i > 0, p_ref[...], 0.0)
        ext[0, HALO:HALO + tm, :] = m_ref[...]
        ext[0, HALO + tm:, :] = jnp.where(i < last, n_ref[...], 0.0)
        for s in shifts:
            ext[s, 0:span, :] = ext[0, pl.ds(s, span), :]

    def tap(ext, r0, off):
        return ext[off % SUBLANES, pl.ds(r0 + off - off % SUBLANES, rc), :]

    rc = 32
    for r0 in range(0, tm, rc):
        acc = jnp.zeros((rc, BW), F32)
        for k in range(CONF_K):
            acc = acc + tap(eu, r0, HALO + k - pad_a) * wa_ref[k:k + 1, :]
        acc = acc + ba_ref[...]
        mu = jnp.mean(acc, axis=-1, keepdims=True)
        xc = acc - mu
        var = jnp.mean(xc * xc, axis=-1, keepdims=True)
        yn = xc * lax.rsqrt(var + EPS) * ga_ref[...] + bta_ref[...]
        y0_ref[pl.ds(r0, rc), :] = (yn * jax.nn.sigmoid(yn)).astype(BF16)
        accb = jnp.zeros((rc, BW), F32)
        for k in range(SCONV_K):
            accb = accb + tap(ec, r0, HALO + k - pad_b) * wb_ref[k:k + 1, :]
        y1_ref[pl.ds(r0, rc), :] = (bg_ref[pl.ds(r0, rc), :] * accb).astype(BF16)


def _attn_kernel(lam_ref, q_ref, k_ref, v_ref, g_ref, o_ref, kmax_sc, acc_sc, *, sub, tk, nk, out_scale):
    h = pl.program_id(0)
    kc = HEADS * 2 * DK

    @pl.when((h == 0) & (pl.program_id(1) == 0))
    def _():
        def colmax(j, mx):
            blk = k_ref[pl.ds(pl.multiple_of(j * kc, kc), kc), :].astype(F32)
            return jnp.maximum(mx, jnp.max(jnp.abs(blk), axis=0, keepdims=True))
        kmax_sc[...] = lax.fori_loop(0, nk * tk // kc, colmax, jnp.zeros((1, kc), F32))

    for t in range(q_ref.shape[0] // sub):
        rows = pl.ds(t * sub, sub)
        _attn_tile(lam_ref, q_ref.at[rows], k_ref, v_ref, g_ref, o_ref.at[rows], kmax_sc, acc_sc.at[t],
                   h=h, tk=tk, nk=nk, out_scale=out_scale)


def _attn_tile(lam_ref, q_ref, k_ref, v_ref, g_ref, o_ref, kmax_sc, acc_sc, *, h, tk, nk, out_scale):
    kc = HEADS * 2 * DK
    q = q_ref[...]
    tq = q.shape[0]
    lane_q = lax.broadcasted_iota(I32, (1, kc), 1)
    lane_o = lax.broadcasted_iota(I32, (1, LANES), 1)
    qbound = jnp.abs(q.astype(F32)) * kmax_sc[...]
    qms, shifts = [], []
    for m in range(2):
        lo = (h * 2 + m) * DK
        sel = (lane_q >= lo) & (lane_q < lo + DK)
        qms.append(jnp.where(sel, q, jnp.zeros_like(q)))
        ub = jnp.sum(jnp.where(sel, qbound, 0.0), axis=-1, keepdims=True) * BOUND_SLACK
        shifts.append(jnp.tile(jnp.broadcast_to(ub, (tq, LANES)), (1, tk // LANES)))

    def tiles(j):
        start = j * tk if isinstance(j, int) else pl.multiple_of(j * tk, tk)
        return k_ref[pl.ds(start, tk), :], v_ref[pl.ds(start, tk), :]

    def scores(m, kj):
        return lax.dot_general(qms[m], kj, (((1,), (1,)), ((), ())), preferred_element_type=F32)

    acc_sc[...] = jnp.zeros(acc_sc.shape, F32)

    def shifted(j, carry):
        kj, vj = tiles(j)
        for m in range(2):
            p = jnp.exp2(scores(m, kj) - shifts[m]).astype(BF16)
            acc_sc[m] += jnp.dot(p, vj, preferred_element_type=F32)
        return carry

    for j in range(nk):
        shifted(j, 0)
    lmin = jnp.minimum(jnp.min(acc_sc[0][:, DV:DV + 1]), jnp.min(acc_sc[1][:, DV:DV + 1]))

    @pl.when(jnp.logical_not(lmin >= L_MIN))
    def _():
        def online(j, carry):
            kj, vj = tiles(j)
            new = []
            for m in range(2):
                mx, acc = carry[m]
                s = scores(m, kj)
                mnew = jnp.maximum(mx, jnp.max(s, axis=-1, keepdims=True))
                p = jnp.exp2(s - mnew).astype(BF16)
                new.append((mnew, jnp.exp2(mx - mnew) * acc + jnp.dot(p, vj, preferred_element_type=F32)))
            return tuple(new)

        init = (jnp.full((tq, 1), -jnp.inf, F32), jnp.zeros((tq, LANES), F32))
        res = lax.fori_loop(0, nk, online, (init, init))
        for m in range(2):
            acc_sc[m] = res[m][1]

    outs = [acc_sc[m] / acc_sc[m][:, DV:DV + 1] for m in range(2)]
    o = outs[0] - lam_ref[0, 0] * outs[1]
    valid = lane_o < DV
    ms = jnp.sum(jnp.where(valid, o * o, 0.0), axis=-1, keepdims=True) * (1.0 / DV)
    y = o * lax.rsqrt(ms + EPS) * g_ref[...] * out_scale
    o_ref[...] = jnp.where(valid, y, 0.0).astype(BF16)


def _attention(q, k, v, lam, gpad, out_scale):
    n = q.shape[0]
    nkeys = k.shape[0]
    sub = _row_tile(n, ATTN_TQ)
    tq = _row_tile(n, sub * ATTN_SUB)
    tk = next(t for t in ATTN_TK if nkeys % t == 0 and t * sub <= ATTN_TILE)
    kern = functools.partial(_attn_kernel, sub=sub, tk=tk, nk=nkeys // tk, out_scale=out_scale)
    return pl.pallas_call(
        kern,
        grid=(HEADS, n // tq),
        in_specs=[pl.BlockSpec(memory_space=pltpu.SMEM),
                  pl.BlockSpec((tq, 256), lambda h, i: (i, 0)),
                  pl.BlockSpec((nkeys, 256), lambda h, i: (0, 0)),
                  pl.BlockSpec((nkeys, LANES), lambda h, i: (0, h)),
                  pl.BlockSpec((1, LANES), lambda h, i: (0, 0))],
        out_specs=pl.BlockSpec((tq, LANES), lambda h, i: (i, h)),
        out_shape=jax.ShapeDtypeStruct((n, HEADS * LANES), BF16),
        scratch_shapes=[pltpu.VMEM((1, HEADS * 2 * DK), F32), pltpu.VMEM((tq // sub, 2, sub, LANES), F32)],
        compiler_params=_cparams("arbitrary", "arbitrary"),
    )(lam, q, k, v, gpad)


def _ret_kernel(ld_ref, qf_ref, kf_ref, vf_ref, qb_ref, kb_ref, vb_ref, s0_ref, of_ref, ob_ref, sf_ref,
                state, dmask, qdec, kdec, cdec, *, after_init=None):
    n = pl.program_id(0)
    c = CHUNK
    lane_q = lax.broadcasted_iota(I32, (1, HEADS * DK), 1)
    lane_v = lax.broadcasted_iota(I32, (1, HEADS * DV), 1)

    @pl.when(n == 0)
    def _():
        state[...] = s0_ref[...]
        row_h = lax.broadcasted_iota(I32, (HEADS * DK, 1), 0) // DK
        for d in range(2):
            pi = lax.broadcasted_iota(I32, (c, 1), 0).astype(F32)
            pj = lax.broadcasted_iota(I32, (1, c), 1).astype(F32)
            if d == 1:
                pi, pj = c - 1.0 - pi, c - 1.0 - pj
            diff = pi - pj
            lg_q = jnp.zeros((1, HEADS * DK), F32)
            lg_r = jnp.zeros((HEADS * DK, 1), F32)
            for h in range(HEADS):
                lg = ld_ref[d, h]
                dmask[d, h] = jnp.exp(jnp.where(diff >= 0, diff * lg, -jnp.inf))
                lg_q = jnp.where(lane_q // DK == h, lg, lg_q)
                lg_r = jnp.where(row_h == h, lg, lg_r)
            qdec[d] = jnp.exp((pi + 1.0) * lg_q)
            kdec[d] = jnp.exp((c - 1.0 - pi) * lg_q)
            cdec[d] = jnp.where(row_h == lane_v // DV, jnp.exp(c * lg_r), 0.0)

    if after_init is not None:
        after_init()

    for d, (q_ref, k_ref, v_ref, o_ref) in enumerate(((qf_ref, kf_ref, vf_ref, of_ref),
                                                      (qb_ref, kb_ref, vb_ref, ob_ref))):
        nsub = q_ref.shape[0] // c
        cd = cdec[d]
        s_cur = state[d]
        for sub in (range(nsub) if d == 0 else reversed(range(nsub))):
            rows = pl.ds(sub * c, c)
            q = q_ref[rows, :]
            k = k_ref[rows, :]
            kb = k.astype(BF16)
            vb = v_ref[rows, :].astype(BF16)
            o = jnp.dot((q * qdec[d]).astype(BF16), s_cur.astype(BF16), preferred_element_type=F32)
            for h in range(HEADS):
                qm = jnp.where(lane_q // DK == h, q, 0.0).astype(BF16)
                sc = lax.dot_general(qm, kb, (((1,), (1,)), ((), ())),
                                     preferred_element_type=F32) * dmask[d, h]
                oh = jnp.dot(sc.astype(BF16), vb, preferred_element_type=F32)
                o = o + jnp.where(lane_v // DV == h, oh, 0.0)
            o_ref[rows, :] = o
            kv = lax.dot_general((k * kdec[d]).astype(BF16), vb, (((0,), (0,)), ((), ())),
                                 preferred_element_type=F32)
            s_cur = cd * s_cur + jnp.where(cd != 0.0, kv, 0.0)
        state[d] = s_cur

    @pl.when(n == pl.num_programs(0) - 1)
    def _():
        sf_ref[...] = state[...]


N_CONV_IN, N_RET_IN, N_CONV_OUT, N_RET_OUT, N_CONV_SCR = 12, 8, 2, 3, 2


def _convret_kernel(*refs):
    cin, rin = refs[:N_CONV_IN], refs[N_CONV_IN:N_CONV_IN + N_RET_IN]
    outs = refs[N_CONV_IN + N_RET_IN:]
    cout, rout = outs[:N_CONV_OUT], outs[N_CONV_OUT:N_CONV_OUT + N_RET_OUT]
    scr = outs[N_CONV_OUT + N_RET_OUT:]
    _ret_kernel(*rin, *rout, *scr[N_CONV_SCR:],
                after_init=lambda: _conv_kernel(*cin, *cout, *scr[:N_CONV_SCR]))


def _convs_retention(u, cx, bg, conv_w, rq, rk, rv, ld, s0):
    n = u.shape[0]
    rows = _row_tile(n, RET_ROWS)
    nc = n // rows
    hb = rows // HALO
    nh = n // HALO
    prev = pl.BlockSpec((HALO, BW), lambda i: (jnp.maximum(i * hb - 1, 0), 0))
    nxt = pl.BlockSpec((HALO, BW), lambda i: (jnp.minimum((i + 1) * hb, nh - 1), 0))
    vec = lambda r: pl.BlockSpec((r, BW), lambda i: (0, 0))
    fwd = lambda w: pl.BlockSpec((rows, w), lambda i: (i, 0))
    bwd = lambda w: pl.BlockSpec((rows, w), lambda i: (nc - 1 - i, 0))
    sshape = (2, HEADS * DK, HEADS * DV)
    whole = pl.BlockSpec(sshape, lambda i: (0, 0, 0))
    dq, dv = HEADS * DK, HEADS * DV
    main = fwd(BW)
    return pl.pallas_call(
        _convret_kernel,
        grid=(nc,),
        in_specs=[prev, main, nxt, prev, main, nxt, main,
                  vec(CONF_K), vec(1), vec(1), vec(1), vec(SCONV_K),
                  pl.BlockSpec(memory_space=pltpu.SMEM),
                  fwd(dq), fwd(dq), fwd(dv), bwd(dq), bwd(dq), bwd(dv), whole],
        out_specs=[main, main, fwd(dv), bwd(dv), whole],
        out_shape=[jax.ShapeDtypeStruct((n, BW), BF16), jax.ShapeDtypeStruct((n, BW), BF16),
                   jax.ShapeDtypeStruct((n, dv), F32), jax.ShapeDtypeStruct((n, dv), F32),
                   jax.ShapeDtypeStruct(sshape, F32)],
        scratch_shapes=[pltpu.VMEM((SUBLANES, rows + 2 * HALO, BW), F32)] * N_CONV_SCR
                       + [pltpu.VMEM(sshape, F32),
                          pltpu.VMEM((2, HEADS, CHUNK, CHUNK), F32),
                          pltpu.VMEM((2, CHUNK, dq), F32),
                          pltpu.VMEM((2, CHUNK, dq), F32),
                          pltpu.VMEM(sshape, F32)],
        compiler_params=_cparams("arbitrary"),
    )(u, u, u, cx, cx, cx, bg, *conv_w, ld, rq, rk, rv, rq, rk, rv, s0)


def _first_max(vals):
    idx = jnp.zeros(vals[0].shape, I32)
    best = vals[0]
    for j in range(1, len(vals)):
        upd = vals[j] > best
        idx = jnp.where(upd, j, idx)
        best = jnp.where(upd, vals[j], best)
    return idx, best


def _split_bf16(a):
    hi = a.astype(BF16)
    return hi, (a - hi.astype(F32)).astype(BF16)


def _pick(vals, idx):
    out = vals[-1]
    for j in range(len(vals) - 2, -1, -1):
        out = jnp.where(idx == j, vals[j], out)
    return out


def _merge_kernel(h_ref, y0_ref, y1_ref, y2_ref, rof_ref, rob_ref, rg_ref, x_ref, g1_ref, gn_ref, sh_ref, sc_ref,
                  wg_ref, bgate_ref, wb0_ref, wb1_ref, wb2_ref, wb3_ref, wo_ref, wr_ref, br_ref,
                  xo_ref, h2_ref, lp_ref, w_ref, tab_ref, cnt_ref, base):
    i = pl.program_id(0)
    tm = x_ref.shape[0]

    @pl.when(i == 0)
    def _():
        base[...] = jnp.zeros_like(base)

    ro = rof_ref[...] + rob_ref[...]
    gr = lax.broadcasted_iota(I32, (HEADS * DV, HEADS * DV), 0) // DV
    gc = lax.broadcasted_iota(I32, (HEADS * DV, HEADS * DV), 1) // DV
    avg = jnp.where(gr == gc, 1.0 / DV, 0.0).astype(BF16)

    def head_mean(a):
        hi, lo = _split_bf16(a)
        return (jnp.dot(hi, avg, preferred_element_type=F32) + jnp.dot(lo, avg, preferred_element_type=F32))

    mu = head_mean(ro)
    xc = ro - mu
    var = head_mean(xc * xc)
    rg = rg_ref[...]
    y3 = (rg * jax.nn.sigmoid(rg) * (xc * lax.rsqrt(var + EPS))).astype(BF16)

    hb = h_ref[...]
    ys = (y0_ref[...], y1_ref[...], y2_ref[...], y3)
    wbs = (wb0_ref, wb1_ref, wb2_ref, wb3_ref)
    parts = []
    for c0 in range(0, D, GATE_COLS):
        mc = None
        for b in range(4):
            lo = b * D + c0
            gl = jnp.dot(hb, wg_ref[:, lo:lo + GATE_COLS], preferred_element_type=F32)
            gate = jax.nn.sigmoid(gl + bgate_ref[:, lo:lo + GATE_COLS])
            t = gate * jnp.dot(ys[b], wbs[b][:, c0:c0 + GATE_COLS], preferred_element_type=F32)
            mc = t if mc is None else mc + t
        parts.append(mc.astype(BF16))
    yo = jnp.dot(jnp.concatenate(parts, axis=-1), wo_ref[...], preferred_element_type=F32)
    xn = x_ref[...] + g1_ref[...] * yo
    xo_ref[...] = xn
    yn = xn * lax.rsqrt(jnp.mean(xn * xn, axis=-1, keepdims=True) + EPS) * gn_ref[...]
    h2 = yn * (1.0 + sc_ref[...]) + sh_ref[...]
    h2_ref[...] = h2.astype(BF16)

    h_hi, h_lo = _split_bf16(h2)
    w_hi, w_lo = _split_bf16(wr_ref[...])
    nt = (((1,), (1,)), ((), ()))
    l2 = lax.dot_general(jnp.concatenate([w_hi, w_lo], axis=0), h_hi, nt, preferred_element_type=F32)
    lt = (l2[:N_EXPERTS] + l2[N_EXPERTS:]) + lax.dot_general(w_hi, h_lo, nt, preferred_element_type=F32)
    s = jax.nn.sigmoid(lt)
    sb = s + br_ref[...]
    r = [sb[e:e + 1, :] for e in range(N_EXPERTS)]
    sr = [s[e:e + 1, :] for e in range(N_EXPERTS)]
    gscore = []
    for g in range(N_GROUPS):
        a, b, c, d = r[EPG * g:EPG * (g + 1)]
        hi1, lo1, hi2, lo2 = jnp.maximum(a, b), jnp.minimum(a, b), jnp.maximum(c, d), jnp.minimum(c, d)
        gscore.append(jnp.maximum(hi1, hi2) + jnp.maximum(jnp.minimum(hi1, hi2), jnp.maximum(lo1, lo2)))
    gsel, _ = _first_max(gscore)
    v = [_pick([r[EPG * g + j] for g in range(N_GROUPS)], gsel) for j in range(EPG)]
    sv = [_pick([sr[EPG * g + j] for g in range(N_GROUPS)], gsel) for j in range(EPG)]
    i1, _ = _first_max(v)
    i2, _ = _first_max([jnp.where(i1 == j, -jnp.inf, v[j]) for j in range(EPG)])
    w1 = _pick(sv, i1)
    w2 = _pick(sv, i2)
    den = w1 + w2
    e1 = gsel * EPG + i1
    e2 = gsel * EPG + i2
    w_ref[0:1, :] = w1 / den
    w_ref[1:2, :] = w2 / den

    eio = lax.broadcasted_iota(I32, (N_EXPERTS, tm), 0)
    oh1 = eio == e1
    oh2 = eio == e2
    cnt = oh1.astype(F32) + oh2.astype(F32)
    ti = lax.broadcasted_iota(I32, (tm, tm), 0)
    tj = lax.broadcasted_iota(I32, (tm, tm), 1)
    before = jnp.where(ti < tj, 1.0, 0.0).astype(BF16)
    pref = jnp.dot(cnt.astype(BF16), before, preferred_element_type=F32)
    plen = jnp.floor((jnp.sum(cnt, axis=1, keepdims=True) + (RUN - 1.0)) * (1.0 / RUN)) * RUN
    eio1 = lax.broadcasted_iota(I32, (N_EXPERTS, 1), 0)
    loff = jnp.zeros((N_EXPERTS, 1), F32)
    run_start = jnp.zeros((1, 1), F32)
    for ex in range(N_EXPERTS):
        loff = jnp.where(eio1 == ex, run_start, loff)
        run_start = run_start + plen[ex:ex + 1, :]
    start = loff + pref
    lp_ref[0:1, :] = jnp.sum(jnp.where(oh1, start, 0.0), axis=0, keepdims=True).astype(I32)
    lp_ref[1:2, :] = jnp.sum(jnp.where(oh2, start, 0.0), axis=0, keepdims=True).astype(I32)
    goff = base[...]
    lane = lax.broadcasted_iota(I32, (1, LANES), 1)
    tab_ref[...] = jnp.where(lane == 0, plen, jnp.where(lane == 1, goff, 0.0)).astype(I32)
    base[...] = goff + plen
    cnt_ref[...] = jnp.broadcast_to(goff + plen, (N_EXPERTS, LANES)).astype(I32)


def _merge(h, y0, y1, y2, rof, rob, rg, x, g1, gn, sh, sc, wts):
    n = x.shape[0]
    tm = _row_tile(n, MOE_TILE)
    row = lambda c: pl.BlockSpec((tm, c), lambda i: (i, 0))
    full = lambda a: pl.BlockSpec(a.shape, lambda i: (0,) * a.ndim)
    tok = pl.BlockSpec((2, tm), lambda i: (0, i))
    return pl.pallas_call(
        _merge_kernel,
        grid=(n // tm,),
        in_specs=[row(D), row(BW), row(BW), row(HEADS * LANES),
                  row(HEADS * DV), row(HEADS * DV), row(BW), row(D),
                  full(g1), full(gn), full(sh), full(sc)] + [full(a) for a in wts],
        out_specs=[row(D), row(D), tok, tok,
                   pl.BlockSpec((N_EXPERTS, LANES), lambda i: (i, 0)),
                   pl.BlockSpec((N_EXPERTS, LANES), lambda i: (0, 0))],
        out_shape=[jax.ShapeDtypeStruct((n, D), F32), jax.ShapeDtypeStruct((n, D), BF16),
                   jax.ShapeDtypeStruct((2, n), I32), jax.ShapeDtypeStruct((2, n), F32),
                   jax.ShapeDtypeStruct((n // tm * N_EXPERTS, LANES), I32),
                   jax.ShapeDtypeStruct((N_EXPERTS, LANES), I32)],
        scratch_shapes=[pltpu.VMEM((N_EXPERTS, 1), F32)],
        compiler_params=_cparams("arbitrary"),
    )(h, y0, y1, y2, rof, rob, rg, x, g1, gn, sh, sc, *wts)


def _for_run_pieces(tab_ref, ps_ref, fn):
    loff = 0
    for ex in range(N_EXPERTS):
        plen = tab_ref[ex, 0]
        gbase = ps_ref[ex] + tab_ref[ex, 1]

        def piece(c, carry, loff=loff, gbase=gbase):
            fn(pl.multiple_of(loff + c * RUN, RUN), pl.multiple_of(gbase + c * RUN, RUN))
            return carry

        lax.fori_loop(0, plen // RUN, piece, 0)
        loff = loff + plen


def _dispatch_kernel(ps_ref, tab_ref, tail_ref, lp_ref, x_ref, xs_ref, buf, sem):
    rmax, tm = buf.shape[0], x_ref.shape[0]
    lp = lp_ref[...]
    r = lax.broadcasted_iota(I32, (rmax, tm), 0)
    sel = jnp.where((r == lp[0:1, :]) | (r == lp[1:2, :]), 1.0, 0.0).astype(BF16)
    buf[...] = jnp.dot(sel, x_ref[...], preferred_element_type=F32).astype(BF16)

    def copy(s, t):
        return pltpu.make_async_copy(buf.at[pl.ds(s, RUN)], xs_ref.at[pl.ds(t, RUN)], sem)

    _for_run_pieces(tab_ref, ps_ref, lambda s, t: copy(s, t).start())
    _for_run_pieces(tab_ref, ps_ref, lambda s, t: copy(s, t).wait())

    @pl.when(pl.program_id(0) == pl.num_programs(0) - 1)
    def _():
        buf[0:MOE_BLOCK, :] = jnp.zeros((MOE_BLOCK, D), BF16)

        def tails(fn):
            for ex in range(N_EXPERTS):
                first = tail_ref[0, ex]

                def piece(c, carry, first=first):
                    fn(pl.multiple_of(first + c * RUN, RUN))
                    return carry

                lax.fori_loop(0, tail_ref[1, ex] // RUN, piece, 0)

        def block_copy(b):
            dst = xs_ref.at[pl.ds(pl.multiple_of(b * MOE_BLOCK, MOE_BLOCK), MOE_BLOCK)]
            return pltpu.make_async_copy(buf.at[pl.ds(0, MOE_BLOCK)], dst, sem)

        def unused(fn):
            def blk(b, carry):
                fn(b)
                return carry
            lax.fori_loop(tail_ref[2, 0], xs_ref.shape[0] // MOE_BLOCK, blk, 0)

        tails(lambda t: copy(0, t).start())
        unused(lambda b: block_copy(b).start())
        tails(lambda t: copy(0, t).wait())
        unused(lambda b: block_copy(b).wait())


def _tile_tables(tm):
    return [pl.BlockSpec(memory_space=pltpu.SMEM),
            pl.BlockSpec((N_EXPERTS, LANES), lambda i: (i, 0), memory_space=pltpu.SMEM)]


def _dispatch(h2, lp, tab, pstart, tails, cap, rmax):
    n = h2.shape[0]
    tm = _row_tile(n, MOE_TILE)
    return pl.pallas_call(
        _dispatch_kernel,
        grid=(n // tm,),
        in_specs=_tile_tables(tm) + [pl.BlockSpec(memory_space=pltpu.SMEM),
                                     pl.BlockSpec((2, tm), lambda i: (0, i)),
                                     pl.BlockSpec((tm, D), lambda i: (i, 0))],
        out_specs=pl.BlockSpec(memory_space=pl.ANY),
        out_shape=jax.ShapeDtypeStruct((cap, D), BF16),
        scratch_shapes=[pltpu.VMEM((rmax, D), BF16), pltpu.SemaphoreType.DMA(())],
        compiler_params=_cparams("arbitrary"),
    )(pstart, tab, tails, lp, h2)


def _expert_kernel(be_ref, nu_ref, x_ref, w1_ref, w3_ref, w2_ref, y_ref, w1b, w3b, w2b):
    b = pl.program_id(0)

    @pl.when((b == 0) | (be_ref[b] != be_ref[jnp.maximum(b - 1, 0)]))
    def _():
        w1b[...] = w1_ref[0, 0].astype(BF16)
        w3b[...] = w3_ref[0, 0].astype(BF16)
        w2b[...] = w2_ref[0, 0].astype(BF16)

    @pl.when(b < nu_ref[0])
    def _():
        x = x_ref[...]
        a = jnp.dot(x, w1b[...], preferred_element_type=F32)
        c = jnp.dot(x, w3b[...], preferred_element_type=F32)
        u = (a * jax.nn.sigmoid(a) * c).astype(BF16)
        y_ref[...] = jnp.dot(u, w2b[...], preferred_element_type=F32).astype(BF16)

    @pl.when(b >= nu_ref[0])
    def _():
        y_ref[...] = jnp.zeros_like(y_ref)


def _experts(xs, blk_e, nused, l, w1, w3, w2):
    cap = xs.shape[0]
    nb = cap // MOE_BLOCK
    xmap = lambda b, be, nu: (jnp.minimum(b, jnp.maximum(nu[0] - 1, 0)), 0)
    wmap = lambda b, be, nu: (l, be[b], 0, 0)
    return pl.pallas_call(
        _expert_kernel,
        grid_spec=pltpu.PrefetchScalarGridSpec(
            num_scalar_prefetch=2,
            grid=(nb,),
            in_specs=[pl.BlockSpec((MOE_BLOCK, D), xmap),
                      pl.BlockSpec((1, 1, D, D_FF), wmap),
                      pl.BlockSpec((1, 1, D, D_FF), wmap),
                      pl.BlockSpec((1, 1, D_FF, D), wmap)],
            out_specs=pl.BlockSpec((MOE_BLOCK, D), lambda b, be, nu: (b, 0)),
            scratch_shapes=[pltpu.VMEM((D, D_FF), BF16), pltpu.VMEM((D, D_FF), BF16),
                            pltpu.VMEM((D_FF, D), BF16)]),
        out_shape=jax.ShapeDtypeStruct((cap, D), BF16),
        compiler_params=_cparams("arbitrary"),
    )(blk_e, nused, xs, w1, w3, w2)


def _combine_kernel(ps_ref, tab_ref, lp_ref, w_ref, x_ref, g2_ref, gf_ref, ys_ref, o_ref, buf, sem, *,
                    final_norm):
    rmax, tm = buf.shape[0], x_ref.shape[0]

    @pl.when(pl.program_id(0) == 0)
    def _():
        buf[...] = jnp.zeros_like(buf)

    def copy(s, t):
        return pltpu.make_async_copy(ys_ref.at[pl.ds(t, RUN)], buf.at[pl.ds(s, RUN)], sem)

    _for_run_pieces(tab_ref, ps_ref, lambda s, t: copy(s, t).start())
    lp = lp_ref[...]
    w = w_ref[...]
    r = lax.broadcasted_iota(I32, (tm, rmax), 1)
    mix = (jnp.where(r == lp[:, 0:1], w[:, 0:1], 0.0)
           + jnp.where(r == lp[:, 1:2], w[:, 1:2], 0.0)).astype(BF16)
    _for_run_pieces(tab_ref, ps_ref, lambda s, t: copy(s, t).wait())
    ml = jnp.dot(mix, buf[...], preferred_element_type=F32)
    xo = x_ref[...] + g2_ref[...] * ml
    if final_norm:
        xo = xo * lax.rsqrt(jnp.mean(xo * xo, axis=-1, keepdims=True) + EPS) * gf_ref[...]
    o_ref[...] = xo


def _combine(x, g2, lpt, wtok, tab, pstart, ys, gf, final_norm, rmax):
    n = x.shape[0]
    tm = _row_tile(n, MOE_TILE)
    vec = pl.BlockSpec((1, D), lambda i: (0, 0))
    tok = pl.BlockSpec((tm, 2), lambda i: (i, 0))
    return pl.pallas_call(
        functools.partial(_combine_kernel, final_norm=final_norm),
        grid=(n // tm,),
        in_specs=_tile_tables(tm) + [tok, tok, pl.BlockSpec((tm, D), lambda i: (i, 0)), vec, vec,
                                     pl.BlockSpec(memory_space=pl.ANY)],
        out_specs=pl.BlockSpec((tm, D), lambda i: (i, 0)),
        out_shape=jax.ShapeDtypeStruct((n, D), F32),
        scratch_shapes=[pltpu.VMEM((rmax, D), BF16), pltpu.SemaphoreType.DMA(())],
        compiler_params=_cparams("arbitrary"),
    )(pstart, tab, lpt, wtok, x, g2, gf, ys)


def _moe(x, h2, lp, wsel, tab, filled, g2, l, w1, w3, w2, gf, final_norm):
    n = x.shape[0]
    tm = _row_tile(n, MOE_TILE)
    rmax = 2 * tm + N_EXPERTS * RUN
    worst = n // tm * (2 * tm + N_EXPERTS * (RUN - 1)) + N_EXPERTS * (MOE_BLOCK - RUN)
    cap = (worst + MOE_BLOCK - 1) // MOE_BLOCK * MOE_BLOCK
    nb = cap // MOE_BLOCK
    region = (filled + MOE_BLOCK - 1) // MOE_BLOCK * MOE_BLOCK
    pend = jnp.cumsum(region)
    pstart = (pend - region).astype(I32)
    blk_start = jnp.arange(nb, dtype=I32) * MOE_BLOCK
    blk_e = jnp.minimum(jnp.sum(pend[None, :] <= blk_start[:, None], axis=-1), N_EXPERTS - 1).astype(I32)
    nused = (pend[-1:] // MOE_BLOCK).astype(I32)
    tails = jnp.stack([pstart + filled, region - filled, jnp.broadcast_to(nused, (N_EXPERTS,))]).astype(I32)
    xs = _dispatch(h2, lp, tab, pstart, tails, cap, rmax)
    ys = _experts(xs, blk_e, nused, l, w1, w3, w2)
    return _combine(x, g2, lp.T, wsel.T, tab, pstart, ys, gf, final_norm, rmax)


def _rope_tables(n):
    rows = n // GRID_W
    row = jnp.repeat(jnp.arange(rows, dtype=F32), GRID_W)
    col = jnp.tile(jnp.arange(GRID_W, dtype=F32), rows)
    nf = DK // 4
    inv = ROPE_BASE ** (-jnp.arange(nf, dtype=F32) / nf)
    ang = jnp.concatenate([row[:, None] * inv, col[:, None] * inv], axis=-1)
    cos, sin = jnp.cos(ang), jnp.sin(ang)
    reps = LANES // DK
    return (jnp.tile(jnp.concatenate([cos, cos], -1), (1, reps)),
            jnp.tile(jnp.concatenate([-sin, sin], -1), (1, reps)))


def _layer_weights(l, w_in, w_gate, b_gate, w_branch, w_o, w_router, b_router):
    w = w_in[l].astype(BF16)
    wb = w_branch[l]
    wb2 = wb[2].reshape(HEADS, DV, D)
    wb2 = jnp.concatenate([wb2, jnp.zeros((HEADS, LANES - DV, D), F32)], axis=1).reshape(HEADS * LANES, D)
    merge_w = (w_gate[l].astype(BF16), b_gate[l].reshape(1, 4 * D), wb[0].astype(BF16), wb[1].astype(BF16),
               wb2.astype(BF16), wb[3].astype(BF16), w_o[l].astype(BF16),
               w_router.T, b_router.reshape(N_EXPERTS, 1))
    return w, merge_w


def kernel(x, c, ctx, c_ctx, w_mod, b_mod, g_norm1, g_norm2, w_in, conv_a_w, conv_a_b, conv_a_g, conv_a_beta, conv_b_w, lam_q1, lam_k1, lam_q2, lam_k2, diff_g, ret_ld_f, ret_ld_b, w_gate, b_gate, w_branch, w_o, w_router, b_router, w1_e, w3_e, w2_e, g_final):
    assert x.shape[0] == 1 and ctx.shape[0] == 1
    xl, xc = x[0], ctx[0]
    n_lat, n_ctx = xl.shape[0], xc.shape[0]
    mods = _adaln(c, c_ctx, w_mod, b_mod)
    cos, sinp = _rope_tables(n_lat)
    zc = jnp.zeros((n_ctx, LANES), F32)
    vone = jnp.zeros((HEADS, LANES), F32).at[:, DV].set(1.0).reshape(1, HEADS * LANES)
    gf = g_final.reshape(1, D)
    vrow = lambda a: a.reshape(1, -1)

    for l in range(DEPTH):
        last = l == DEPTH - 1
        lam_init = 0.8 - 0.6 * math.exp(-0.3 * l)
        ml = [mods[l, 0:1, j * D:(j + 1) * D] for j in range(6)]
        mc = [mods[l, 1:2, j * D:(j + 1) * D] for j in range(6)]
        w, merge_w = _layer_weights(l, w_in, w_gate, b_gate, w_branch, w_o, w_router, b_router)
        g1 = vrow(g_norm1[l])
        g2n = vrow(g_norm2[l])
        lam = (jnp.exp(jnp.sum(lam_q1[l] * lam_k1[l])) - jnp.exp(jnp.sum(lam_q2[l] * lam_k2[l]))
               + lam_init).reshape(1, 1).astype(F32)
        gpad = jnp.concatenate([diff_g[l], jnp.zeros((LANES - DV,), F32)]).reshape(1, LANES)
        ld = jnp.stack([ret_ld_f[l], ret_ld_b[l]]).astype(F32)
        conv_w = (conv_a_w[l], vrow(conv_a_b[l]), vrow(conv_a_g[l]), vrow(conv_a_beta[l]), conv_b_w[l])

        (hc, uc, bgc, cxc, qc, kc, vc, rqc, rkc, rvc, rgc) = _inproj(
            xc, g1, mc[0], mc[1], w, zc, zc, vone, rope=False)
        (hl, ul, bgl, cxl, ql, kl, vl, rql, rkl, rvl, rgl) = _inproj(
            xl, g1, ml[0], ml[1], w, cos, sinp, vone, rope=True)

        kall = jnp.concatenate([kc, kl], axis=0)
        vall = jnp.concatenate([vc, vl], axis=0)
        out_scale = 1.0 - lam_init
        y2l = _attention(ql, kall, vall, lam, gpad, out_scale)
        s0 = jnp.zeros((2, HEADS * DK, HEADS * DV), F32)
        y0c, y1c, rofc, robc, sfin = _convs_retention(uc, cxc, bgc, conv_w, rqc, rkc, rvc, ld, s0)
        y0l, y1l, rofl, robl, _ = _convs_retention(ul, cxl, bgl, conv_w, rql, rkl, rvl, ld, sfin)
        xl, h2l, lpl, wl, tabl, cl = _merge(hl, y0l, y1l, y2l, rofl, robl, rgl, xl, ml[2], g2n, ml[3], ml[4], merge_w)
        xl = _moe(xl, h2l, lpl, wl, tabl, cl[:, 0], ml[5], l, w1_e, w3_e, w2_e, gf, last)
        if not last:
            y2c = _attention(qc, kc, vc, lam, gpad, out_scale)
            xc, h2c, lpc, wc, tabc, cc = _merge(hc, y0c, y1c, y2c, rofc, robc, rgc, xc, mc[2], g2n, mc[3], mc[4], merge_w)
            xc = _moe(xc, h2c, lpc, wc, tabc, cc[:, 0], mc[5], l, w1_e, w3_e, w2_e, gf, False)
    return xl[None]
```

```python
import functools
import itertools
import math

import jax
import jax.numpy as jnp
from jax import lax
from jax.experimental import pallas as pl
from jax.experimental.pallas import tpu as pltpu

F32 = jnp.float32
BF16 = jnp.bfloat16
I32 = jnp.int32
HI = lax.Precision.HIGHEST

D = 1024
DEPTH = 2
GRID_W = 64
BW = 256
CONF_K = 31
SCONV_K = 3
HEADS = 4
DK = 32
DV = 64
CHUNK = 128
ROPE_BASE = 10000.0
N_EXPERTS = 16
N_GROUPS = 4
EPG = N_EXPERTS // N_GROUPS
D_FF = 512
MOE_BLOCK = 512
EPS = 1e-6
LOG2E = math.log2(math.e)
BOUND_SLACK = 1.0 + 2.0 ** -10
L_MIN = 2.0 ** -64
ATTN_TQ = 512
ATTN_TK = (3328, 1280, 256)
ATTN_TILE = 512 * 3328
RET_ROWS = 512
GATE_COLS = 512
MOE_TILE = 512
RUN = 16
HALO = 16
LANES = 128
SUBLANES = 8
QK_W = HEADS * 2 * DK
VPAD_W = HEADS * LANES
RQ_W = HEADS * DK
RV_W = HEADS * DV
IN_OFFS = tuple(itertools.accumulate((0, 2 * BW, 3 * BW, QK_W, QK_W, RV_W, RQ_W, RQ_W, RV_W, BW)))
W_COLS = IN_OFFS[-1]
INPROJ_TILE = 512
VMEM_LIMIT = 56 * 1024 * 1024


def _cparams(*sem):
    return pltpu.CompilerParams(dimension_semantics=sem, vmem_limit_bytes=VMEM_LIMIT)


def _row_tile(n, pref):
    return pref if n % pref == 0 else n


def _mod_kernel(c_ref, w_ref, b_ref, o_ref):
    a = c_ref[...]
    a = a * jax.nn.sigmoid(a)
    o_ref[0] = jnp.dot(a, w_ref[0], preferred_element_type=F32, precision=HI) + b_ref[0]


def _adaln(c, c_ctx, w_mod, b_mod):
    cs = jnp.zeros((8, D), F32).at[0].set(c[0]).at[1].set(c_ctx)
    return pl.pallas_call(
        _mod_kernel,
        grid=(DEPTH, 6),
        in_specs=[pl.BlockSpec((8, D), lambda l, j: (0, 0)),
                  pl.BlockSpec((1, D, D), lambda l, j: (l, 0, j)),
                  pl.BlockSpec((1, 1, D), lambda l, j: (l, 0, j))],
        out_specs=pl.BlockSpec((1, 8, D), lambda l, j: (l, 0, j)),
        out_shape=jax.ShapeDtypeStruct((DEPTH, 8, 6 * D), F32),
        compiler_params=_cparams("parallel", "parallel"),
    )(cs, w_mod, b_mod.reshape(DEPTH, 1, 6 * D))


def _rope(x, cos, sinp):
    lane = lax.broadcasted_iota(I32, (1, LANES), 1)
    first = (lane % 32) < 16
    outs = []
    for c in range(x.shape[1] // LANES):
        xc = x[:, c * LANES:(c + 1) * LANES]
        sw = jnp.where(first, pltpu.roll(xc, LANES - 16, 1), pltpu.roll(xc, 16, 1))
        outs.append(xc * cos + sw * sinp)
    return outs[0] if len(outs) == 1 else jnp.concatenate(outs, axis=-1)


def _inproj_kernel(x_ref, g_ref, sh_ref, sc_ref, w_ref, cos_ref, sin_ref, vone_ref,
                   h_ref, u_ref, bg_ref, cx_ref, q_ref, k_ref, v_ref,
                   rq_ref, rk_ref, rv_ref, rg_ref, *, rope):
    x = x_ref[...]
    y = x * lax.rsqrt(jnp.mean(x * x, axis=-1, keepdims=True) + EPS) * g_ref[...]
    hb = (y * (1.0 + sc_ref[...]) + sh_ref[...]).astype(BF16)
    h_ref[...] = hb

    def seg(g):
        return jnp.dot(hb, w_ref[:, IN_OFFS[g]:IN_OFFS[g + 1]], preferred_element_type=F32)

    z = seg(0)
    u_ref[...] = z[:, :BW] * jax.nn.sigmoid(z[:, BW:])
    z = seg(1)
    bg_ref[...] = z[:, :BW]
    cx_ref[...] = z[:, BW:2 * BW] * z[:, 2 * BW:]
    zq = seg(2)
    zk = seg(3)
    zrq = seg(5)
    zrk = seg(6)
    if rope:
        cos = cos_ref[...]
        sinp = sin_ref[...]
        zq, zk = _rope(zq, cos, sinp), _rope(zk, cos, sinp)
        zrq, zrk = _rope(zrq, cos, sinp), _rope(zrk, cos, sinp)
    q_ref[...] = (zq * (DK ** -0.5 * LOG2E)).astype(BF16)
    k_ref[...] = zk.astype(BF16)
    vr = lax.broadcasted_iota(I32, (HEADS * DV, HEADS * LANES), 0)
    vc = lax.broadcasted_iota(I32, (HEADS * DV, HEADS * LANES), 1)
    spread = jnp.where((vc // LANES == vr // DV) & (vc % LANES == vr % DV), 1.0, 0.0).astype(BF16)
    zv = seg(4).astype(BF16)
    v_ref[...] = (jnp.dot(zv, spread, preferred_element_type=F32) + vone_ref[...]).astype(BF16)
    rq_ref[...] = zrq
    rk_ref[...] = zrk * (DK ** -0.5)
    rv_ref[...] = seg(7)
    rg_ref[...] = seg(8)


def _inproj(x, g, shift, scale, w, cos, sinp, vone, rope):
    n = x.shape[0]
    tm = _row_tile(n, INPROJ_TILE)
    row = lambda c: pl.BlockSpec((tm, c), lambda i: (i, 0))
    vec = lambda c: pl.BlockSpec((1, c), lambda i: (0, 0))
    widths = [(D, BF16), (BW, F32), (BW, F32), (BW, F32), (QK_W, BF16), (QK_W, BF16), (VPAD_W, BF16),
              (RQ_W, F32), (RQ_W, F32), (RV_W, F32), (BW, F32)]
    return pl.pallas_call(
        functools.partial(_inproj_kernel, rope=rope),
        grid=(n // tm,),
        in_specs=[row(D), vec(D), vec(D), vec(D),
                  pl.BlockSpec((D, W_COLS), lambda i: (0, 0)),
                  row(LANES), row(LANES), vec(VPAD_W)],
        out_specs=[row(c) for c, _ in widths],
        out_shape=[jax.ShapeDtypeStruct((n, c), dt) for c, dt in widths],
        compiler_params=_cparams("parallel"),
    )(x, g, shift, scale, w, cos, sinp, vone)


def _conv_kernel(up_ref, um_ref, un_ref, cp_ref, cm_ref, cn_ref, bg_ref,
                 wa_ref, ba_ref, ga_ref, bta_ref, wb_ref, y0_ref, y1_ref, eu, ec):
    i = pl.program_id(0)
    last = pl.num_programs(0) - 1
    tm = um_ref.shape[0]
    pad_a = (CONF_K - 1) // 2
    pad_b = (SCONV_K - 1) // 2
    span = tm + 2 * HALO - SUBLANES
    for ext, p_ref, m_ref, n_ref, shifts in ((eu, up_ref, um_ref, un_ref, range(1, SUBLANES)),
                                             (ec, cp_ref, cm_ref, cn_ref, (1, SUBLANES - 1))):
        ext[0, 0:HALO, :] = jnp.where(i > 0, p_ref[...], 0.0)
        ext[0, HALO:HALO + tm, :] = m_ref[...]
        ext[0, HALO + tm:, :] = jnp.where(i < last, n_ref[...], 0.0)
        for s in shifts:
            ext[s, 0:span, :] = ext[0, pl.ds(s, span), :]

    def tap(ext, r0, off):
        return ext[off % SUBLANES, pl.ds(r0 + off - off % SUBLANES, rc), :]

    rc = 32
    for r0 in range(0, tm, rc):
        acc = jnp.zeros((rc, BW), F32)
        for k in range(CONF_K):
            acc = acc + tap(eu, r0, HALO + k - pad_a) * wa_ref[k:k + 1, :]
        acc = acc + ba_ref[...]
        mu = jnp.mean(acc, axis=-1, keepdims=True)
        xc = acc - mu
        var = jnp.mean(xc * xc, axis=-1, keepdims=True)
        yn = xc * lax.rsqrt(var + EPS) * ga_ref[...] + bta_ref[...]
        y0_ref[pl.ds(r0, rc), :] = (yn * jax.nn.sigmoid(yn)).astype(BF16)
        accb = jnp.zeros((rc, BW), F32)
        for k in range(SCONV_K):
            accb = accb + tap(ec, r0, HALO + k - pad_b) * wb_ref[k:k + 1, :]
        y1_ref[pl.ds(r0, rc), :] = (bg_ref[pl.ds(r0, rc), :] * accb).astype(BF16)


def _attn_kernel(lam_ref, q_ref, k_ref, v_ref, g_ref, o_ref, kmax_sc, acc_sc, *, tk, nk, out_scale):
    h = pl.program_id(0)
    kc = HEADS * 2 * DK

    @pl.when((h == 0) & (pl.program_id(1) == 0))
    def _():
        def colmax(j, mx):
            blk = k_ref[pl.ds(pl.multiple_of(j * kc, kc), kc), :].astype(F32)
            return jnp.maximum(mx, jnp.max(jnp.abs(blk), axis=0, keepdims=True))
        kmax_sc[...] = lax.fori_loop(0, nk * tk // kc, colmax, jnp.zeros((1, kc), F32))

    q = q_ref[...]
    tq = q.shape[0]
    lane_q = lax.broadcasted_iota(I32, (1, kc), 1)
    lane_o = lax.broadcasted_iota(I32, (1, LANES), 1)
    qbound = jnp.abs(q.astype(F32)) * kmax_sc[...]
    qms, shifts = [], []
    for m in range(2):
        lo = (h * 2 + m) * DK
        sel = (lane_q >= lo) & (lane_q < lo + DK)
        qms.append(jnp.where(sel, q, jnp.zeros_like(q)))
        ub = jnp.sum(jnp.where(sel, qbound, 0.0), axis=-1, keepdims=True) * BOUND_SLACK
        shifts.append(jnp.tile(jnp.broadcast_to(ub, (tq, LANES)), (1, tk // LANES)))

    def tiles(j):
        start = j * tk if isinstance(j, int) else pl.multiple_of(j * tk, tk)
        return k_ref[pl.ds(start, tk), :], v_ref[pl.ds(start, tk), :]

    def scores(m, kj):
        return lax.dot_general(qms[m], kj, (((1,), (1,)), ((), ())), preferred_element_type=F32)

    acc_sc[...] = jnp.zeros(acc_sc.shape, F32)

    def shifted(j, carry):
        kj, vj = tiles(j)
        for m in range(2):
            p = jnp.exp2(scores(m, kj) - shifts[m]).astype(BF16)
            acc_sc[m] += jnp.dot(p, vj, preferred_element_type=F32)
        return carry

    for j in range(nk):
        shifted(j, 0)
    lmin = jnp.minimum(jnp.min(acc_sc[0][:, DV:DV + 1]), jnp.min(acc_sc[1][:, DV:DV + 1]))

    @pl.when(jnp.logical_not(lmin >= L_MIN))
    def _():
        def online(j, carry):
            kj, vj = tiles(j)
            new = []
            for m in range(2):
                mx, acc = carry[m]
                s = scores(m, kj)
                mnew = jnp.maximum(mx, jnp.max(s, axis=-1, keepdims=True))
                p = jnp.exp2(s - mnew).astype(BF16)
                new.append((mnew, jnp.exp2(mx - mnew) * acc + jnp.dot(p, vj, preferred_element_type=F32)))
            return tuple(new)

        init = (jnp.full((tq, 1), -jnp.inf, F32), jnp.zeros((tq, LANES), F32))
        res = lax.fori_loop(0, nk, online, (init, init))
        for m in range(2):
            acc_sc[m] = res[m][1]

    outs = [acc_sc[m] / acc_sc[m][:, DV:DV + 1] for m in range(2)]
    o = outs[0] - lam_ref[0, 0] * outs[1]
    valid = lane_o < DV
    ms = jnp.sum(jnp.where(valid, o * o, 0.0), axis=-1, keepdims=True) * (1.0 / DV)
    y = o * lax.rsqrt(ms + EPS) * g_ref[...] * out_scale
    o_ref[...] = jnp.where(valid, y, 0.0).astype(BF16)


def _attention(q, k, v, lam, gpad, out_scale):
    n = q.shape[0]
    nkeys = k.shape[0]
    tq = _row_tile(n, ATTN_TQ)
    tk = next(t for t in ATTN_TK if nkeys % t == 0 and t * tq <= ATTN_TILE)
    kern = functools.partial(_attn_kernel, tk=tk, nk=nkeys // tk, out_scale=out_scale)
    return pl.pallas_call(
        kern,
        grid=(HEADS, n // tq),
        in_specs=[pl.BlockSpec(memory_space=pltpu.SMEM),
                  pl.BlockSpec((tq, QK_W), lambda h, i: (i, 0)),
                  pl.BlockSpec((nkeys, QK_W), lambda h, i: (0, 0)),
                  pl.BlockSpec((nkeys, LANES), lambda h, i: (0, h)),
                  pl.BlockSpec((1, LANES), lambda h, i: (0, 0))],
        out_specs=pl.BlockSpec((tq, LANES), lambda h, i: (i, h)),
        out_shape=jax.ShapeDtypeStruct((n, HEADS * LANES), BF16),
        scratch_shapes=[pltpu.VMEM((1, HEADS * 2 * DK), F32), pltpu.VMEM((2, tq, LANES), F32)],
        compiler_params=_cparams("arbitrary", "arbitrary"),
    )(lam, q, k, v, gpad)


def _ret_kernel(ld_ref, qf_ref, kf_ref, vf_ref, qb_ref, kb_ref, vb_ref, s0_ref, of_ref, ob_ref, sf_ref,
                state, dmask, qdec, kdec, cdec, *, after_init=None):
    n = pl.program_id(0)
    c = CHUNK
    lane_q = lax.broadcasted_iota(I32, (1, HEADS * DK), 1)
    lane_v = lax.broadcasted_iota(I32, (1, HEADS * DV), 1)

    @pl.when(n == 0)
    def _():
        state[...] = s0_ref[...]
        row_h = lax.broadcasted_iota(I32, (HEADS * DK, 1), 0) // DK
        for d in range(2):
            pi = lax.broadcasted_iota(I32, (c, 1), 0).astype(F32)
            pj = lax.broadcasted_iota(I32, (1, c), 1).astype(F32)
            if d == 1:
                pi, pj = c - 1.0 - pi, c - 1.0 - pj
            diff = pi - pj
            lg_q = jnp.zeros((1, HEADS * DK), F32)
            lg_r = jnp.zeros((HEADS * DK, 1), F32)
            for h in range(HEADS):
                lg = ld_ref[d, h]
                dmask[d, h] = jnp.exp(jnp.where(diff >= 0, diff * lg, -jnp.inf))
                lg_q = jnp.where(lane_q // DK == h, lg, lg_q)
                lg_r = jnp.where(row_h == h, lg, lg_r)
            qdec[d] = jnp.exp((pi + 1.0) * lg_q)
            kdec[d] = jnp.exp((c - 1.0 - pi) * lg_q)
            cdec[d] = jnp.where(row_h == lane_v // DV, jnp.exp(c * lg_r), 0.0)

    if after_init is not None:
        after_init()

    for d, (q_ref, k_ref, v_ref, o_ref) in enumerate(((qf_ref, kf_ref, vf_ref, of_ref),
                                                      (qb_ref, kb_ref, vb_ref, ob_ref))):
        nsub = q_ref.shape[0] // c
        cd = cdec[d]
        s_cur = state[d]
        for sub in (range(nsub) if d == 0 else reversed(range(nsub))):
            rows = pl.ds(sub * c, c)
            q = q_ref[rows, :]
            k = k_ref[rows, :]
            kb = k.astype(BF16)
            vb = v_ref[rows, :].astype(BF16)
            o = jnp.dot((q * qdec[d]).astype(BF16), s_cur.astype(BF16), preferred_element_type=F32)
            for h in range(HEADS):
                qm = jnp.where(lane_q // DK == h, q, 0.0).astype(BF16)
                sc = lax.dot_general(qm, kb, (((1,), (1,)), ((), ())),
                                     preferred_element_type=F32) * dmask[d, h]
                oh = jnp.dot(sc.astype(BF16), vb, preferred_element_type=F32)
                o = o + jnp.where(lane_v // DV == h, oh, 0.0)
            o_ref[rows, :] = o
            kv = lax.dot_general((k * kdec[d]).astype(BF16), vb, (((0,), (0,)), ((), ())),
                                 preferred_element_type=F32)
            s_cur = cd * s_cur + jnp.where(cd != 0.0, kv, 0.0)
        state[d] = s_cur

    @pl.when(n == pl.num_programs(0) - 1)
    def _():
        sf_ref[...] = state[...]


N_CONV_IN, N_RET_IN, N_CONV_OUT, N_RET_OUT, N_CONV_SCR = 12, 8, 2, 3, 2


def _convret_kernel(*refs):
    cin, rin = refs[:N_CONV_IN], refs[N_CONV_IN:N_CONV_IN + N_RET_IN]
    outs = refs[N_CONV_IN + N_RET_IN:]
    cout, rout = outs[:N_CONV_OUT], outs[N_CONV_OUT:N_CONV_OUT + N_RET_OUT]
    scr = outs[N_CONV_OUT + N_RET_OUT:]
    _ret_kernel(*rin, *rout, *scr[N_CONV_SCR:],
                after_init=lambda: _conv_kernel(*cin, *cout, *scr[:N_CONV_SCR]))


def _convs_retention(u, cx, bg, conv_w, rq, rk, rv, ld, s0):
    n = u.shape[0]
    rows = _row_tile(n, RET_ROWS)
    nc = n // rows
    hb = rows // HALO
    nh = n // HALO
    prev = pl.BlockSpec((HALO, BW), lambda i: (jnp.maximum(i * hb - 1, 0), 0))
    nxt = pl.BlockSpec((HALO, BW), lambda i: (jnp.minimum((i + 1) * hb, nh - 1), 0))
    vec = lambda r: pl.BlockSpec((r, BW), lambda i: (0, 0))
    fwd = lambda w: pl.BlockSpec((rows, w), lambda i: (i, 0))
    bwd = lambda w: pl.BlockSpec((rows, w), lambda i: (nc - 1 - i, 0))
    sshape = (2, HEADS * DK, HEADS * DV)
    whole = pl.BlockSpec(sshape, lambda i: (0, 0, 0))
    dq, dv = HEADS * DK, HEADS * DV
    main = fwd(BW)
    return pl.pallas_call(
        _convret_kernel,
        grid=(nc,),
        in_specs=[prev, main, nxt, prev, main, nxt, main,
                  vec(CONF_K), vec(1), vec(1), vec(1), vec(SCONV_K),
                  pl.BlockSpec(memory_space=pltpu.SMEM),
                  fwd(dq), fwd(dq), fwd(dv), bwd(dq), bwd(dq), bwd(dv), whole],
        out_specs=[main, main, fwd(dv), bwd(dv), whole],
        out_shape=[jax.ShapeDtypeStruct((n, BW), BF16), jax.ShapeDtypeStruct((n, BW), BF16),
                   jax.ShapeDtypeStruct((n, dv), F32), jax.ShapeDtypeStruct((n, dv), F32),
                   jax.ShapeDtypeStruct(sshape, F32)],
        scratch_shapes=[pltpu.VMEM((SUBLANES, rows + 2 * HALO, BW), F32)] * N_CONV_SCR
                       + [pltpu.VMEM(sshape, F32),
                          pltpu.VMEM((2, HEADS, CHUNK, CHUNK), F32),
                          pltpu.VMEM((2, CHUNK, dq), F32),
                          pltpu.VMEM((2, CHUNK, dq), F32),
                          pltpu.VMEM(sshape, F32)],
        compiler_params=_cparams("arbitrary"),
    )(u, u, u, cx, cx, cx, bg, *conv_w, ld, rq, rk, rv, rq, rk, rv, s0)


def _first_max(vals):
    idx = jnp.zeros(vals[0].shape, I32)
    best = vals[0]
    for j in range(1, len(vals)):
        upd = vals[j] > best
        idx = jnp.where(upd, j, idx)
        best = jnp.where(upd, vals[j], best)
    return idx, best


def _split_bf16(a):
    hi = a.astype(BF16)
    return hi, (a - hi.astype(F32)).astype(BF16)


def _pick(vals, idx):
    out = vals[-1]
    for j in range(len(vals) - 2, -1, -1):
        out = jnp.where(idx == j, vals[j], out)
    return out


def _merge_kernel(h_ref, y0_ref, y1_ref, y2_ref, rof_ref, rob_ref, rg_ref, x_ref, g1_ref, gn_ref, sh_ref, sc_ref,
                  wg_ref, bgate_ref, wb0_ref, wb1_ref, wb2_ref, wb3_ref, wo_ref, wr_ref, br_ref,
                  xo_ref, h2_ref, lp_ref, w_ref, tab_ref, cnt_ref, base):
    i = pl.program_id(0)
    tm = x_ref.shape[0]

    @pl.when(i == 0)
    def _():
        base[...] = jnp.zeros_like(base)

    ro = rof_ref[...] + rob_ref[...]
    gr = lax.broadcasted_iota(I32, (HEADS * DV, HEADS * DV), 0) // DV
    gc = lax.broadcasted_iota(I32, (HEADS * DV, HEADS * DV), 1) // DV
    avg = jnp.where(gr == gc, 1.0 / DV, 0.0).astype(BF16)

    def head_mean(a):
        hi, lo = _split_bf16(a)
        return (jnp.dot(hi, avg, preferred_element_type=F32) + jnp.dot(lo, avg, preferred_element_type=F32))

    mu = head_mean(ro)
    xc = ro - mu
    var = head_mean(xc * xc)
    rg = rg_ref[...]
    y3 = (rg * jax.nn.sigmoid(rg) * (xc * lax.rsqrt(var + EPS))).astype(BF16)

    hb = h_ref[...]
    ys = (y0_ref[...], y1_ref[...], y2_ref[...], y3)
    wbs = (wb0_ref, wb1_ref, wb2_ref, wb3_ref)
    parts = []
    for c0 in range(0, D, GATE_COLS):
        mc = None
        for b in range(4):
            lo = b * D + c0
            gl = jnp.dot(hb, wg_ref[:, lo:lo + GATE_COLS], preferred_element_type=F32)
            gate = jax.nn.sigmoid(gl + bgate_ref[:, lo:lo + GATE_COLS])
            t = gate * jnp.dot(ys[b], wbs[b][:, c0:c0 + GATE_COLS], preferred_element_type=F32)
            mc = t if mc is None else mc + t
        parts.append(mc.astype(BF16))
    yo = jnp.dot(jnp.concatenate(parts, axis=-1), wo_ref[...], preferred_element_type=F32)
    xn = x_ref[...] + g1_ref[...] * yo
    xo_ref[...] = xn
    yn = xn * lax.rsqrt(jnp.mean(xn * xn, axis=-1, keepdims=True) + EPS) * gn_ref[...]
    h2 = yn * (1.0 + sc_ref[...]) + sh_ref[...]
    h2_ref[...] = h2.astype(BF16)

    h_hi, h_lo = _split_bf16(h2)
    w_hi, w_lo = _split_bf16(wr_ref[...])
    nt = (((1,), (1,)), ((), ()))
    l2 = lax.dot_general(jnp.concatenate([w_hi, w_lo], axis=0), h_hi, nt, preferred_element_type=F32)
    lt = (l2[:N_EXPERTS] + l2[N_EXPERTS:]) + lax.dot_general(w_hi, h_lo, nt, preferred_element_type=F32)
    s = jax.nn.sigmoid(lt)
    sb = s + br_ref[...]
    r = [sb[e:e + 1, :] for e in range(N_EXPERTS)]
    sr = [s[e:e + 1, :] for e in range(N_EXPERTS)]
    gscore = []
    for g in range(N_GROUPS):
        a, b, c, d = r[EPG * g:EPG * (g + 1)]
        hi1, lo1, hi2, lo2 = jnp.maximum(a, b), jnp.minimum(a, b), jnp.maximum(c, d), jnp.minimum(c, d)
        gscore.append(jnp.maximum(hi1, hi2) + jnp.maximum(jnp.minimum(hi1, hi2), jnp.maximum(lo1, lo2)))
    gsel, _ = _first_max(gscore)
    v = [_pick([r[EPG * g + j] for g in range(N_GROUPS)], gsel) for j in range(EPG)]
    sv = [_pick([sr[EPG * g + j] for g in range(N_GROUPS)], gsel) for j in range(EPG)]
    i1, _ = _first_max(v)
    i2, _ = _first_max([jnp.where(i1 == j, -jnp.inf, v[j]) for j in range(EPG)])
    w1 = _pick(sv, i1)
    w2 = _pick(sv, i2)
    den = w1 + w2
    e1 = gsel * EPG + i1
    e2 = gsel * EPG + i2
    w_ref[0:1, :] = w1 / den
    w_ref[1:2, :] = w2 / den

    eio = lax.broadcasted_iota(I32, (N_EXPERTS, tm), 0)
    oh1 = eio == e1
    oh2 = eio == e2
    cnt = oh1.astype(F32) + oh2.astype(F32)
    ti = lax.broadcasted_iota(I32, (tm, tm), 0)
    tj = lax.broadcasted_iota(I32, (tm, tm), 1)
    before = jnp.where(ti < tj, 1.0, 0.0).astype(BF16)
    pref = jnp.dot(cnt.astype(BF16), before, preferred_element_type=F32)
    plen = jnp.floor((jnp.sum(cnt, axis=1, keepdims=True) + (RUN - 1.0)) * (1.0 / RUN)) * RUN
    eio1 = lax.broadcasted_iota(I32, (N_EXPERTS, 1), 0)
    loff = jnp.zeros((N_EXPERTS, 1), F32)
    run_start = jnp.zeros((1, 1), F32)
    for ex in range(N_EXPERTS):
        loff = jnp.where(eio1 == ex, run_start, loff)
        run_start = run_start + plen[ex:ex + 1, :]
    start = loff + pref
    lp_ref[0:1, :] = jnp.sum(jnp.where(oh1, start, 0.0), axis=0, keepdims=True).astype(I32)
    lp_ref[1:2, :] = jnp.sum(jnp.where(oh2, start, 0.0), axis=0, keepdims=True).astype(I32)
    goff = base[...]
    lane = lax.broadcasted_iota(I32, (1, LANES), 1)
    tab_ref[...] = jnp.where(lane == 0, plen, jnp.where(lane == 1, goff, 0.0)).astype(I32)
    base[...] = goff + plen
    cnt_ref[...] = jnp.broadcast_to(goff + plen, (N_EXPERTS, LANES)).astype(I32)


def _merge(h, y0, y1, y2, rof, rob, rg, x, g1, gn, sh, sc, wts):
    n = x.shape[0]
    tm = _row_tile(n, MOE_TILE)
    row = lambda c: pl.BlockSpec((tm, c), lambda i: (i, 0))
    full = lambda a: pl.BlockSpec(a.shape, lambda i: (0,) * a.ndim)
    tok = pl.BlockSpec((2, tm), lambda i: (0, i))
    return pl.pallas_call(
        _merge_kernel,
        grid=(n // tm,),
        in_specs=[row(D), row(BW), row(BW), row(HEADS * LANES),
                  row(HEADS * DV), row(HEADS * DV), row(BW), row(D),
                  full(g1), full(gn), full(sh), full(sc)] + [full(a) for a in wts],
        out_specs=[row(D), row(D), tok, tok,
                   pl.BlockSpec((N_EXPERTS, LANES), lambda i: (i, 0)),
                   pl.BlockSpec((N_EXPERTS, LANES), lambda i: (0, 0))],
        out_shape=[jax.ShapeDtypeStruct((n, D), F32), jax.ShapeDtypeStruct((n, D), BF16),
                   jax.ShapeDtypeStruct((2, n), I32), jax.ShapeDtypeStruct((2, n), F32),
                   jax.ShapeDtypeStruct((n // tm * N_EXPERTS, LANES), I32),
                   jax.ShapeDtypeStruct((N_EXPERTS, LANES), I32)],
        scratch_shapes=[pltpu.VMEM((N_EXPERTS, 1), F32)],
        compiler_params=_cparams("arbitrary"),
    )(h, y0, y1, y2, rof, rob, rg, x, g1, gn, sh, sc, *wts)


def _for_run_pieces(tab_ref, ps_ref, fn):
    loff = 0
    for ex in range(N_EXPERTS):
        plen = tab_ref[ex, 0]
        gbase = ps_ref[ex] + tab_ref[ex, 1]

        def piece(c, carry, loff=loff, gbase=gbase):
            fn(pl.multiple_of(loff + c * RUN, RUN), pl.multiple_of(gbase + c * RUN, RUN))
            return carry

        lax.fori_loop(0, plen // RUN, piece, 0)
        loff = loff + plen


def _dispatch_kernel(ps_ref, tab_ref, tail_ref, lp_ref, x_ref, xs_ref, buf, sem):
    rmax, tm = buf.shape[0], x_ref.shape[0]
    lp = lp_ref[...]
    r = lax.broadcasted_iota(I32, (rmax, tm), 0)
    sel = jnp.where((r == lp[0:1, :]) | (r == lp[1:2, :]), 1.0, 0.0).astype(BF16)
    buf[...] = jnp.dot(sel, x_ref[...], preferred_element_type=F32).astype(BF16)

    def copy(s, t):
        return pltpu.make_async_copy(buf.at[pl.ds(s, RUN)], xs_ref.at[pl.ds(t, RUN)], sem)

    _for_run_pieces(tab_ref, ps_ref, lambda s, t: copy(s, t).start())
    _for_run_pieces(tab_ref, ps_ref, lambda s, t: copy(s, t).wait())

    @pl.when(pl.program_id(0) == pl.num_programs(0) - 1)
    def _():
        buf[0:MOE_BLOCK, :] = jnp.zeros((MOE_BLOCK, D), BF16)

        def tails(fn):
            for ex in range(N_EXPERTS):
                first = tail_ref[0, ex]

                def piece(c, carry, first=first):
                    fn(pl.multiple_of(first + c * RUN, RUN))
                    return carry

                lax.fori_loop(0, tail_ref[1, ex] // RUN, piece, 0)

        def block_copy(b):
            dst = xs_ref.at[pl.ds(pl.multiple_of(b * MOE_BLOCK, MOE_BLOCK), MOE_BLOCK)]
            return pltpu.make_async_copy(buf.at[pl.ds(0, MOE_BLOCK)], dst, sem)

        def unused(fn):
            def blk(b, carry):
                fn(b)
                return carry
            lax.fori_loop(tail_ref[2, 0], xs_ref.shape[0] // MOE_BLOCK, blk, 0)

        tails(lambda t: copy(0, t).start())
        unused(lambda b: block_copy(b).start())
        tails(lambda t: copy(0, t).wait())
        unused(lambda b: block_copy(b).wait())


def _tile_tables(tm):
    return [pl.BlockSpec(memory_space=pltpu.SMEM),
            pl.BlockSpec((N_EXPERTS, LANES), lambda i: (i, 0), memory_space=pltpu.SMEM)]


def _dispatch(h2, lp, tab, pstart, tails, cap, rmax):
    n = h2.shape[0]
    tm = _row_tile(n, MOE_TILE)
    return pl.pallas_call(
        _dispatch_kernel,
        grid=(n // tm,),
        in_specs=_tile_tables(tm) + [pl.BlockSpec(memory_space=pltpu.SMEM),
                                     pl.BlockSpec((2, tm), lambda i: (0, i)),
                                     pl.BlockSpec((tm, D), lambda i: (i, 0))],
        out_specs=pl.BlockSpec(memory_space=pl.ANY),
        out_shape=jax.ShapeDtypeStruct((cap, D), BF16),
        scratch_shapes=[pltpu.VMEM((rmax, D), BF16), pltpu.SemaphoreType.DMA(())],
        compiler_params=_cparams("arbitrary"),
    )(pstart, tab, tails, lp, h2)


def _expert_kernel(be_ref, nu_ref, x_ref, w1_ref, w3_ref, w2_ref, y_ref, w1b, w3b, w2b):
    b = pl.program_id(0)

    @pl.when((b == 0) | (be_ref[b] != be_ref[jnp.maximum(b - 1, 0)]))
    def _():
        w1b[...] = w1_ref[0, 0].astype(BF16)
        w3b[...] = w3_ref[0, 0].astype(BF16)
        w2b[...] = w2_ref[0, 0].astype(BF16)

    @pl.when(b < nu_ref[0])
    def _():
        x = x_ref[...]
        a = jnp.dot(x, w1b[...], preferred_element_type=F32)
        c = jnp.dot(x, w3b[...], preferred_element_type=F32)
        u = (a * jax.nn.sigmoid(a) * c).astype(BF16)
        y_ref[...] = jnp.dot(u, w2b[...], preferred_element_type=F32).astype(BF16)

    @pl.when(b >= nu_ref[0])
    def _():
        y_ref[...] = jnp.zeros_like(y_ref)


def _experts(xs, blk_e, nused, l, w1, w3, w2):
    cap = xs.shape[0]
    nb = cap // MOE_BLOCK
    xmap = lambda b, be, nu: (jnp.minimum(b, jnp.maximum(nu[0] - 1, 0)), 0)
    wmap = lambda b, be, nu: (l, be[b], 0, 0)
    return pl.pallas_call(
        _expert_kernel,
        grid_spec=pltpu.PrefetchScalarGridSpec(
            num_scalar_prefetch=2,
            grid=(nb,),
            in_specs=[pl.BlockSpec((MOE_BLOCK, D), xmap),
                      pl.BlockSpec((1, 1, D, D_FF), wmap),
                      pl.BlockSpec((1, 1, D, D_FF), wmap),
                      pl.BlockSpec((1, 1, D_FF, D), wmap)],
            out_specs=pl.BlockSpec((MOE_BLOCK, D), lambda b, be, nu: (b, 0)),
            scratch_shapes=[pltpu.VMEM((D, D_FF), BF16), pltpu.VMEM((D, D_FF), BF16),
                            pltpu.VMEM((D_FF, D), BF16)]),
        out_shape=jax.ShapeDtypeStruct((cap, D), BF16),
        compiler_params=_cparams("arbitrary"),
    )(blk_e, nused, xs, w1, w3, w2)


def _combine_kernel(ps_ref, tab_ref, lp_ref, w_ref, x_ref, g2_ref, gf_ref, ys_ref, o_ref, buf, sem, *,
                    final_norm):
    rmax, tm = buf.shape[0], x_ref.shape[0]

    @pl.when(pl.program_id(0) == 0)
    def _():
        buf[...] = jnp.zeros_like(buf)

    def copy(s, t):
        return pltpu.make_async_copy(ys_ref.at[pl.ds(t, RUN)], buf.at[pl.ds(s, RUN)], sem)

    _for_run_pieces(tab_ref, ps_ref, lambda s, t: copy(s, t).start())
    lp = lp_ref[...]
    w = w_ref[...]
    r = lax.broadcasted_iota(I32, (tm, rmax), 1)
    mix = (jnp.where(r == lp[:, 0:1], w[:, 0:1], 0.0)
           + jnp.where(r == lp[:, 1:2], w[:, 1:2], 0.0)).astype(BF16)
    _for_run_pieces(tab_ref, ps_ref, lambda s, t: copy(s, t).wait())
    ml = jnp.dot(mix, buf[...], preferred_element_type=F32)
    xo = x_ref[...] + g2_ref[...] * ml
    if final_norm:
        xo = xo * lax.rsqrt(jnp.mean(xo * xo, axis=-1, keepdims=True) + EPS) * gf_ref[...]
    o_ref[...] = xo


def _combine(x, g2, lpt, wtok, tab, pstart, ys, gf, final_norm, rmax):
    n = x.shape[0]
    tm = _row_tile(n, MOE_TILE)
    vec = pl.BlockSpec((1, D), lambda i: (0, 0))
    tok = pl.BlockSpec((tm, 2), lambda i: (i, 0))
    return pl.pallas_call(
        functools.partial(_combine_kernel, final_norm=final_norm),
        grid=(n // tm,),
        in_specs=_tile_tables(tm) + [tok, tok, pl.BlockSpec((tm, D), lambda i: (i, 0)), vec, vec,
                                     pl.BlockSpec(memory_space=pl.ANY)],
        out_specs=pl.BlockSpec((tm, D), lambda i: (i, 0)),
        out_shape=jax.ShapeDtypeStruct((n, D), F32),
        scratch_shapes=[pltpu.VMEM((rmax, D), BF16), pltpu.SemaphoreType.DMA(())],
        compiler_params=_cparams("arbitrary"),
    )(pstart, tab, lpt, wtok, x, g2, gf, ys)


def _moe(x, h2, lp, wsel, tab, filled, g2, l, w1, w3, w2, gf, final_norm):
    n = x.shape[0]
    tm = _row_tile(n, MOE_TILE)
    rmax = 2 * tm + N_EXPERTS * RUN
    worst = n // tm * (2 * tm + N_EXPERTS * (RUN - 1)) + N_EXPERTS * (MOE_BLOCK - RUN)
    cap = (worst + MOE_BLOCK - 1) // MOE_BLOCK * MOE_BLOCK
    nb = cap // MOE_BLOCK
    region = (filled + MOE_BLOCK - 1) // MOE_BLOCK * MOE_BLOCK
    pend = jnp.cumsum(region)
    pstart = (pend - region).astype(I32)
    blk_start = jnp.arange(nb, dtype=I32) * MOE_BLOCK
    blk_e = jnp.minimum(jnp.sum(pend[None, :] <= blk_start[:, None], axis=-1), N_EXPERTS - 1).astype(I32)
    nused = (pend[-1:] // MOE_BLOCK).astype(I32)
    tails = jnp.stack([pstart + filled, region - filled, jnp.broadcast_to(nused, (N_EXPERTS,))]).astype(I32)
    xs = _dispatch(h2, lp, tab, pstart, tails, cap, rmax)
    ys = _experts(xs, blk_e, nused, l, w1, w3, w2)
    return _combine(x, g2, lp.T, wsel.T, tab, pstart, ys, gf, final_norm, rmax)


def _rope_tables(n):
    rows = n // GRID_W
    row = jnp.repeat(jnp.arange(rows, dtype=F32), GRID_W)
    col = jnp.tile(jnp.arange(GRID_W, dtype=F32), rows)
    nf = DK // 4
    inv = ROPE_BASE ** (-jnp.arange(nf, dtype=F32) / nf)
    ang = jnp.concatenate([row[:, None] * inv, col[:, None] * inv], axis=-1)
    cos, sin = jnp.cos(ang), jnp.sin(ang)
    reps = LANES // DK
    return (jnp.tile(jnp.concatenate([cos, cos], -1), (1, reps)),
            jnp.tile(jnp.concatenate([-sin, sin], -1), (1, reps)))


def _layer_weights(l, w_in, w_gate, b_gate, w_branch, w_o, w_router, b_router):
    w = w_in[l].astype(BF16)
    wb = w_branch[l]
    wb2 = wb[2].reshape(HEADS, DV, D)
    wb2 = jnp.concatenate([wb2, jnp.zeros((HEADS, LANES - DV, D), F32)], axis=1).reshape(HEADS * LANES, D)
    merge_w = (w_gate[l].astype(BF16), b_gate[l].reshape(1, 4 * D), wb[0].astype(BF16), wb[1].astype(BF16),
               wb2.astype(BF16), wb[3].astype(BF16), w_o[l].astype(BF16),
               w_router.T, b_router.reshape(N_EXPERTS, 1))
    return w, merge_w


def kernel(x, c, ctx, c_ctx, w_mod, b_mod, g_norm1, g_norm2, w_in, conv_a_w, conv_a_b, conv_a_g, conv_a_beta, conv_b_w, lam_q1, lam_k1, lam_q2, lam_k2, diff_g, ret_ld_f, ret_ld_b, w_gate, b_gate, w_branch, w_o, w_router, b_router, w1_e, w3_e, w2_e, g_final):
    assert x.shape[0] == 1 and ctx.shape[0] == 1
    xl, xc = x[0], ctx[0]
    n_lat, n_ctx = xl.shape[0], xc.shape[0]
    mods = _adaln(c, c_ctx, w_mod, b_mod)
    cos, sinp = _rope_tables(n_lat)
    zc = jnp.zeros((n_ctx, LANES), F32)
    vone = jnp.zeros((HEADS, LANES), F32).at[:, DV].set(1.0).reshape(1, HEADS * LANES)
    gf = g_final.reshape(1, D)
    vrow = lambda a: a.reshape(1, -1)

    for l in range(DEPTH):
        last = l == DEPTH - 1
        lam_init = 0.8 - 0.6 * math.exp(-0.3 * l)
        ml = [mods[l, 0:1, j * D:(j + 1) * D] for j in range(6)]
        mc = [mods[l, 1:2, j * D:(j + 1) * D] for j in range(6)]
        w, merge_w = _layer_weights(l, w_in, w_gate, b_gate, w_branch, w_o, w_router, b_router)
        g1 = vrow(g_norm1[l])
        g2n = vrow(g_norm2[l])
        lam = (jnp.exp(jnp.sum(lam_q1[l] * lam_k1[l])) - jnp.exp(jnp.sum(lam_q2[l] * lam_k2[l]))
               + lam_init).reshape(1, 1).astype(F32)
        gpad = jnp.concatenate([diff_g[l], jnp.zeros((LANES - DV,), F32)]).reshape(1, LANES)
        ld = jnp.stack([ret_ld_f[l], ret_ld_b[l]]).astype(F32)
        conv_w = (conv_a_w[l], vrow(conv_a_b[l]), vrow(conv_a_g[l]), vrow(conv_a_beta[l]), conv_b_w[l])

        (hc, uc, bgc, cxc, qc, kc, vc, rqc, rkc, rvc, rgc) = _inproj(
            xc, g1, mc[0], mc[1], w, zc, zc, vone, rope=False)
        (hl, ul, bgl, cxl, ql, kl, vl, rql, rkl, rvl, rgl) = _inproj(
            xl, g1, ml[0], ml[1], w, cos, sinp, vone, rope=True)

        kall = jnp.concatenate([kc, kl], axis=0)
        vall = jnp.concatenate([vc, vl], axis=0)
        out_scale = 1.0 - lam_init
        y2l = _attention(ql, kall, vall, lam, gpad, out_scale)
        s0 = jnp.zeros((2, HEADS * DK, HEADS * DV), F32)
        y0c, y1c, rofc, robc, sfin = _convs_retention(uc, cxc, bgc, conv_w, rqc, rkc, rvc, ld, s0)
        y0l, y1l, rofl, robl, _ = _convs_retention(ul, cxl, bgl, conv_w, rql, rkl, rvl, ld, sfin)
        xl, h2l, lpl, wl, tabl, cl = _merge(hl, y0l, y1l, y2l, rofl, robl, rgl, xl, ml[2], g2n, ml[3], ml[4], merge_w)
        xl = _moe(xl, h2l, lpl, wl, tabl, cl[:, 0], ml[5], l, w1_e, w3_e, w2_e, gf, last)
        if not last:
            y2c = _attention(qc, kc, vc, lam, gpad, out_scale)
            xc, h2c, lpc, wc, tabc, cc = _merge(hc, y0c, y1c, y2c, rofc, robc, rgc, xc, mc[2], g2n, mc[3], mc[4], merge_w)
            xc = _moe(xc, h2c, lpc, wc, tabc, cc[:, 0], mc[5], l, w1_e, w3_e, w2_e, gf, False)
    return xl[None]
```

```python
import functools
import itertools
import math

import jax
import jax.numpy as jnp
from jax import lax
from jax.experimental import pallas as pl
from jax.experimental.pallas import tpu as pltpu

F32 = jnp.float32
BF16 = jnp.bfloat16
I32 = jnp.int32
HI = lax.Precision.HIGHEST

D = 1024
DEPTH = 2
GRID_W = 64
BW = 256
CONF_K = 31
SCONV_K = 3
HEADS = 4
DK = 32
DV = 64
CHUNK = 128
ROPE_BASE = 10000.0
N_EXPERTS = 16
N_GROUPS = 4
EPG = N_EXPERTS // N_GROUPS
D_FF = 512
MOE_BLOCK = 512
EPS = 1e-6
LOG2E = math.log2(math.e)
BOUND_SLACK = 1.0 + 2.0 ** -10
L_MIN = 2.0 ** -64
ATTN_TQ = 512
ATTN_TK = (3328, 1280, 256)
ATTN_TILE = 512 * 3328
RET_ROWS = 1024
GATE_COLS = 512
MOE_TILE = 512
RUN = 16
HALO = 16
LANES = 128
SUBLANES = 8
QK_W = HEADS * 2 * DK
VPAD_W = HEADS * LANES
RQ_W = HEADS * DK
RV_W = HEADS * DV
IN_OFFS = tuple(itertools.accumulate((0, 2 * BW, 3 * BW, QK_W, QK_W, RV_W, RQ_W, RQ_W, RV_W, BW)))
W_COLS = IN_OFFS[-1]
INPROJ_TILE = 1024
VMEM_LIMIT = 56 * 1024 * 1024


def _cparams(*sem):
    return pltpu.CompilerParams(dimension_semantics=sem, vmem_limit_bytes=VMEM_LIMIT)


def _row_tile(n, pref):
    return pref if n % pref == 0 else n


def _mod_kernel(c_ref, w_ref, b_ref, o_ref):
    a = c_ref[...]
    a = a * jax.nn.sigmoid(a)
    o_ref[0] = jnp.dot(a, w_ref[0], preferred_element_type=F32, precision=HI) + b_ref[0]


def _adaln(c, c_ctx, w_mod, b_mod):
    cs = jnp.zeros((8, D), F32).at[0].set(c[0]).at[1].set(c_ctx)
    return pl.pallas_call(
        _mod_kernel,
        grid=(DEPTH, 6),
        in_specs=[pl.BlockSpec((8, D), lambda l, j: (0, 0)),
                  pl.BlockSpec((1, D, D), lambda l, j: (l, 0, j)),
                  pl.BlockSpec((1, 1, D), lambda l, j: (l, 0, j))],
        out_specs=pl.BlockSpec((1, 8, D), lambda l, j: (l, 0, j)),
        out_shape=jax.ShapeDtypeStruct((DEPTH, 8, 6 * D), F32),
        compiler_params=_cparams("parallel", "parallel"),
    )(cs, w_mod, b_mod.reshape(DEPTH, 1, 6 * D))


def _rope(x, cos, sinp):
    lane = lax.broadcasted_iota(I32, (1, LANES), 1)
    first = (lane % 32) < 16
    outs = []
    for c in range(x.shape[1] // LANES):
        xc = x[:, c * LANES:(c + 1) * LANES]
        sw = jnp.where(first, pltpu.roll(xc, LANES - 16, 1), pltpu.roll(xc, 16, 1))
        outs.append(xc * cos + sw * sinp)
    return outs[0] if len(outs) == 1 else jnp.concatenate(outs, axis=-1)


def _inproj_kernel(x_ref, g_ref, sh_ref, sc_ref, w_ref, cos_ref, sin_ref, vone_ref,
                   h_ref, u_ref, bg_ref, cx_ref, q_ref, k_ref, v_ref,
                   rq_ref, rk_ref, rv_ref, rg_ref, *, rope):
    x = x_ref[...]
    y = x * lax.rsqrt(jnp.mean(x * x, axis=-1, keepdims=True) + EPS) * g_ref[...]
    hb = (y * (1.0 + sc_ref[...]) + sh_ref[...]).astype(BF16)
    h_ref[...] = hb

    def seg(g):
        return jnp.dot(hb, w_ref[:, IN_OFFS[g]:IN_OFFS[g + 1]], preferred_element_type=F32)

    z = seg(0)
    u_ref[...] = z[:, :BW] * jax.nn.sigmoid(z[:, BW:])
    z = seg(1)
    bg_ref[...] = z[:, :BW]
    cx_ref[...] = z[:, BW:2 * BW] * z[:, 2 * BW:]
    zq = seg(2)
    zk = seg(3)
    zrq = seg(5)
    zrk = seg(6)
    if rope:
        cos = cos_ref[...]
        sinp = sin_ref[...]
        zq, zk = _rope(zq, cos, sinp), _rope(zk, cos, sinp)
        zrq, zrk = _rope(zrq, cos, sinp), _rope(zrk, cos, sinp)
    q_ref[...] = (zq * (DK ** -0.5 * LOG2E)).astype(BF16)
    k_ref[...] = zk.astype(BF16)
    vr = lax.broadcasted_iota(I32, (HEADS * DV, HEADS * LANES), 0)
    vc = lax.broadcasted_iota(I32, (HEADS * DV, HEADS * LANES), 1)
    spread = jnp.where((vc // LANES == vr // DV) & (vc % LANES == vr % DV), 1.0, 0.0).astype(BF16)
    zv = seg(4).astype(BF16)
    v_ref[...] = (jnp.dot(zv, spread, preferred_element_type=F32) + vone_ref[...]).astype(BF16)
    rq_ref[...] = zrq
    rk_ref[...] = zrk * (DK ** -0.5)
    rv_ref[...] = seg(7)
    rg_ref[...] = seg(8)


def _inproj(x, g, shift, scale, w, cos, sinp, vone, rope):
    n = x.shape[0]
    tm = _row_tile(n, INPROJ_TILE)
    row = lambda c: pl.BlockSpec((tm, c), lambda i: (i, 0))
    vec = lambda c: pl.BlockSpec((1, c), lambda i: (0, 0))
    widths = [(D, BF16), (BW, F32), (BW, F32), (BW, F32), (QK_W, BF16), (QK_W, BF16), (VPAD_W, BF16),
              (RQ_W, F32), (RQ_W, F32), (RV_W, F32), (BW, F32)]
    return pl.pallas_call(
        functools.partial(_inproj_kernel, rope=rope),
        grid=(n // tm,),
        in_specs=[row(D), vec(D), vec(D), vec(D),
                  pl.BlockSpec((D, W_COLS), lambda i: (0, 0)),
                  row(LANES), row(LANES), vec(VPAD_W)],
        out_specs=[row(c) for c, _ in widths],
        out_shape=[jax.ShapeDtypeStruct((n, c), dt) for c, dt in widths],
        compiler_params=_cparams("parallel"),
    )(x, g, shift, scale, w, cos, sinp, vone)


def _conv_kernel(up_ref, um_ref, un_ref, cp_ref, cm_ref, cn_ref, bg_ref,
                 wa_ref, ba_ref, ga_ref, bta_ref, wb_ref, y0_ref, y1_ref, eu, ec):
    i = pl.program_id(0)
    last = pl.num_programs(0) - 1
    tm = um_ref.shape[0]
    pad_a = (CONF_K - 1) // 2
    pad_b = (SCONV_K - 1) // 2
    span = tm + 2 * HALO - SUBLANES
    for ext, p_ref, m_ref, n_ref, shifts in ((eu, up_ref, um_ref, un_ref, range(1, SUBLANES)),
                                             (ec, cp_ref, cm_ref, cn_ref, (1, SUBLANES - 1))):
        ext[0, 0:HALO, :] = jnp.where(i > 0, p_ref[...], 0.0)
        ext[0, HALO:HALO + tm, :] = m_ref[...]
        ext[0, HALO + tm:, :] = jnp.where(i < last, n_ref[...], 0.0)
        for s in shifts:
            ext[s, 0:span, :] = ext[0, pl.ds(s, span), :]

    def tap(ext, r0, off):
        return ext[off % SUBLANES, pl.ds(r0 + off - off % SUBLANES, rc), :]

    rc = 32
    for r0 in range(0, tm, rc):
        acc = jnp.zeros((rc, BW), F32)
        for k in range(CONF_K):
            acc = acc + tap(eu, r0, HALO + k - pad_a) * wa_ref[k:k + 1, :]
        acc = acc + ba_ref[...]
        mu = jnp.mean(acc, axis=-1, keepdims=True)
        xc = acc - mu
        var = jnp.mean(xc * xc, axis=-1, keepdims=True)
        yn = xc * lax.rsqrt(var + EPS) * ga_ref[...] + bta_ref[...]
        y0_ref[pl.ds(r0, rc), :] = (yn * jax.nn.sigmoid(yn)).astype(BF16)
        accb = jnp.zeros((rc, BW), F32)
        for k in range(SCONV_K):
            accb = accb + tap(ec, r0, HALO + k - pad_b) * wb_ref[k:k + 1, :]
        y1_ref[pl.ds(r0, rc), :] = (bg_ref[pl.ds(r0, rc), :] * accb).astype(BF16)


def _attn_kernel(lam_ref, q_ref, k_ref, v_ref, g_ref, o_ref, kmax_sc, acc_sc, *, tk, nk, out_scale):
    h = pl.program_id(0)
    kc = HEADS * 2 * DK

    @pl.when((h == 0) & (pl.program_id(1) == 0))
    def _():
        def colmax(j, mx):
            blk = k_ref[pl.ds(pl.multiple_of(j * kc, kc), kc), :].astype(F32)
            return jnp.maximum(mx, jnp.max(jnp.abs(blk), axis=0, keepdims=True))
        kmax_sc[...] = lax.fori_loop(0, nk * tk // kc, colmax, jnp.zeros((1, kc), F32))

    q = q_ref[...]
    tq = q.shape[0]
    lane_q = lax.broadcasted_iota(I32, (1, kc), 1)
    lane_o = lax.broadcasted_iota(I32, (1, LANES), 1)
    qbound = jnp.abs(q.astype(F32)) * kmax_sc[...]
    qms, shifts = [], []
    for m in range(2):
        lo = (h * 2 + m) * DK
        sel = (lane_q >= lo) & (lane_q < lo + DK)
        qms.append(jnp.where(sel, q, jnp.zeros_like(q)))
        ub = jnp.sum(jnp.where(sel, qbound, 0.0), axis=-1, keepdims=True) * BOUND_SLACK
        shifts.append(jnp.tile(jnp.broadcast_to(ub, (tq, LANES)), (1, tk // LANES)))

    def tiles(j):
        start = j * tk if isinstance(j, int) else pl.multiple_of(j * tk, tk)
        return k_ref[pl.ds(start, tk), :], v_ref[pl.ds(start, tk), :]

    def scores(m, kj):
        return lax.dot_general(qms[m], kj, (((1,), (1,)), ((), ())), preferred_element_type=F32)

    acc_sc[...] = jnp.zeros(acc_sc.shape, F32)

    def shifted(j, carry):
        kj, vj = tiles(j)
        for m in range(2):
            p = jnp.exp2(scores(m, kj) - shifts[m]).astype(BF16)
            acc_sc[m] += jnp.dot(p, vj, preferred_element_type=F32)
        return carry

    for j in range(nk):
        shifted(j, 0)
    lmin = jnp.minimum(jnp.min(acc_sc[0][:, DV:DV + 1]), jnp.min(acc_sc[1][:, DV:DV + 1]))

    @pl.when(jnp.logical_not(lmin >= L_MIN))
    def _():
        def online(j, carry):
            kj, vj = tiles(j)
            new = []
            for m in range(2):
                mx, acc = carry[m]
                s = scores(m, kj)
                mnew = jnp.maximum(mx, jnp.max(s, axis=-1, keepdims=True))
                p = jnp.exp2(s - mnew).astype(BF16)
                new.append((mnew, jnp.exp2(mx - mnew) * acc + jnp.dot(p, vj, preferred_element_type=F32)))
            return tuple(new)

        init = (jnp.full((tq, 1), -jnp.inf, F32), jnp.zeros((tq, LANES), F32))
        res = lax.fori_loop(0, nk, online, (init, init))
        for m in range(2):
            acc_sc[m] = res[m][1]

    outs = [acc_sc[m] / acc_sc[m][:, DV:DV + 1] for m in range(2)]
    o = outs[0] - lam_ref[0, 0] * outs[1]
    valid = lane_o < DV
    ms = jnp.sum(jnp.where(valid, o * o, 0.0), axis=-1, keepdims=True) * (1.0 / DV)
    y = o * lax.rsqrt(ms + EPS) * g_ref[...] * out_scale
    o_ref[...] = jnp.where(valid, y, 0.0).astype(BF16)


def _attention(q, k, v, lam, gpad, out_scale):
    n = q.shape[0]
    nkeys = k.shape[0]
    tq = _row_tile(n, ATTN_TQ)
    tk = next(t for t in ATTN_TK if nkeys % t == 0 and t * tq <= ATTN_TILE)
    kern = functools.partial(_attn_kernel, tk=tk, nk=nkeys // tk, out_scale=out_scale)
    return pl.pallas_call(
        kern,
        grid=(HEADS, n // tq),
        in_specs=[pl.BlockSpec(memory_space=pltpu.SMEM),
                  pl.BlockSpec((tq, QK_W), lambda h, i: (i, 0)),
                  pl.BlockSpec((nkeys, QK_W), lambda h, i: (0, 0)),
                  pl.BlockSpec((nkeys, LANES), lambda h, i: (0, h)),
                  pl.BlockSpec((1, LANES), lambda h, i: (0, 0))],
        out_specs=pl.BlockSpec((tq, LANES), lambda h, i: (i, h)),
        out_shape=jax.ShapeDtypeStruct((n, HEADS * LANES), BF16),
        scratch_shapes=[pltpu.VMEM((1, HEADS * 2 * DK), F32), pltpu.VMEM((2, tq, LANES), F32)],
        compiler_params=_cparams("arbitrary", "arbitrary"),
    )(lam, q, k, v, gpad)


def _ret_kernel(ld_ref, qf_ref, kf_ref, vf_ref, qb_ref, kb_ref, vb_ref, s0_ref, of_ref, ob_ref, sf_ref,
                state, dmask, qdec, kdec, cdec, *, after_init=None):
    n = pl.program_id(0)
    c = CHUNK
    lane_q = lax.broadcasted_iota(I32, (1, HEADS * DK), 1)
    lane_v = lax.broadcasted_iota(I32, (1, HEADS * DV), 1)

    @pl.when(n == 0)
    def _():
        state[...] = s0_ref[...]
        row_h = lax.broadcasted_iota(I32, (HEADS * DK, 1), 0) // DK
        for d in range(2):
            pi = lax.broadcasted_iota(I32, (c, 1), 0).astype(F32)
            pj = lax.broadcasted_iota(I32, (1, c), 1).astype(F32)
            if d == 1:
                pi, pj = c - 1.0 - pi, c - 1.0 - pj
            diff = pi - pj
            lg_q = jnp.zeros((1, HEADS * DK), F32)
            lg_r = jnp.zeros((HEADS * DK, 1), F32)
            for h in range(HEADS):
                lg = ld_ref[d, h]
                dmask[d, h] = jnp.exp(jnp.where(diff >= 0, diff * lg, -jnp.inf))
                lg_q = jnp.where(lane_q // DK == h, lg, lg_q)
                lg_r = jnp.where(row_h == h, lg, lg_r)
            qdec[d] = jnp.exp((pi + 1.0) * lg_q)
            kdec[d] = jnp.exp((c - 1.0 - pi) * lg_q)
            cdec[d] = jnp.where(row_h == lane_v // DV, jnp.exp(c * lg_r), 0.0)

    if after_init is not None:
        after_init()

    for d, (q_ref, k_ref, v_ref, o_ref) in enumerate(((qf_ref, kf_ref, vf_ref, of_ref),
                                                      (qb_ref, kb_ref, vb_ref, ob_ref))):
        nsub = q_ref.shape[0] // c
        cd = cdec[d]
        s_cur = state[d]
        for sub in (range(nsub) if d == 0 else reversed(range(nsub))):
            rows = pl.ds(sub * c, c)
            q = q_ref[rows, :]
            k = k_ref[rows, :]
            kb = k.astype(BF16)
            vb = v_ref[rows, :].astype(BF16)
            o = jnp.dot((q * qdec[d]).astype(BF16), s_cur.astype(BF16), preferred_element_type=F32)
            for h in range(HEADS):
                qm = jnp.where(lane_q // DK == h, q, 0.0).astype(BF16)
                sc = lax.dot_general(qm, kb, (((1,), (1,)), ((), ())),
                                     preferred_element_type=F32) * dmask[d, h]
                oh = jnp.dot(sc.astype(BF16), vb, preferred_element_type=F32)
                o = o + jnp.where(lane_v // DV == h, oh, 0.0)
            o_ref[rows, :] = o
            kv = lax.dot_general((k * kdec[d]).astype(BF16), vb, (((0,), (0,)), ((), ())),
                                 preferred_element_type=F32)
            s_cur = cd * s_cur + jnp.where(cd != 0.0, kv, 0.0)
        state[d] = s_cur

    @pl.when(n == pl.num_programs(0) - 1)
    def _():
        sf_ref[...] = state[...]


N_CONV_IN, N_RET_IN, N_CONV_OUT, N_RET_OUT, N_CONV_SCR = 12, 8, 2, 3, 2


def _convret_kernel(*refs):
    cin, rin = refs[:N_CONV_IN], refs[N_CONV_IN:N_CONV_IN + N_RET_IN]
    outs = refs[N_CONV_IN + N_RET_IN:]
    cout, rout = outs[:N_CONV_OUT], outs[N_CONV_OUT:N_CONV_OUT + N_RET_OUT]
    scr = outs[N_CONV_OUT + N_RET_OUT:]
    _ret_kernel(*rin, *rout, *scr[N_CONV_SCR:],
                after_init=lambda: _conv_kernel(*cin, *cout, *scr[:N_CONV_SCR]))


def _convs_retention(u, cx, bg, conv_w, rq, rk, rv, ld, s0):
    n = u.shape[0]
    rows = _row_tile(n, RET_ROWS)
    nc = n // rows
    hb = rows // HALO
    nh = n // HALO
    prev = pl.BlockSpec((HALO, BW), lambda i: (jnp.maximum(i * hb - 1, 0), 0))
    nxt = pl.BlockSpec((HALO, BW), lambda i: (jnp.minimum((i + 1) * hb, nh - 1), 0))
    vec = lambda r: pl.BlockSpec((r, BW), lambda i: (0, 0))
    fwd = lambda w: pl.BlockSpec((rows, w), lambda i: (i, 0))
    bwd = lambda w: pl.BlockSpec((rows, w), lambda i: (nc - 1 - i, 0))
    sshape = (2, HEADS * DK, HEADS * DV)
    whole = pl.BlockSpec(sshape, lambda i: (0, 0, 0))
    dq, dv = HEADS * DK, HEADS * DV
    main = fwd(BW)
    return pl.pallas_call(
        _convret_kernel,
        grid=(nc,),
        in_specs=[prev, main, nxt, prev, main, nxt, main,
                  vec(CONF_K), vec(1), vec(1), vec(1), vec(SCONV_K),
                  pl.BlockSpec(memory_space=pltpu.SMEM),
                  fwd(dq), fwd(dq), fwd(dv), bwd(dq), bwd(dq), bwd(dv), whole],
        out_specs=[main, main, fwd(dv), bwd(dv), whole],
        out_shape=[jax.ShapeDtypeStruct((n, BW), BF16), jax.ShapeDtypeStruct((n, BW), BF16),
                   jax.ShapeDtypeStruct((n, dv), F32), jax.ShapeDtypeStruct((n, dv), F32),
                   jax.ShapeDtypeStruct(sshape, F32)],
        scratch_shapes=[pltpu.VMEM((SUBLANES, rows + 2 * HALO, BW), F32)] * N_CONV_SCR
                       + [pltpu.VMEM(sshape, F32),
                          pltpu.VMEM((2, HEADS, CHUNK, CHUNK), F32),
                          pltpu.VMEM((2, CHUNK, dq), F32),
                          pltpu.VMEM((2, CHUNK, dq), F32),
                          pltpu.VMEM(sshape, F32)],
        compiler_params=_cparams("arbitrary"),
    )(u, u, u, cx, cx, cx, bg, *conv_w, ld, rq, rk, rv, rq, rk, rv, s0)


def _first_max(vals):
    idx = jnp.zeros(vals[0].shape, I32)
    best = vals[0]
    for j in range(1, len(vals)):
        upd = vals[j] > best
        idx = jnp.where(upd, j, idx)
        best = jnp.where(upd, vals[j], best)
    return idx, best


def _split_bf16(a):
    hi = a.astype(BF16)
    return hi, (a - hi.astype(F32)).astype(BF16)


def _pick(vals, idx):
    out = vals[-1]
    for j in range(len(vals) - 2, -1, -1):
        out = jnp.where(idx == j, vals[j], out)
    return out


def _merge_kernel(h_ref, y0_ref, y1_ref, y2_ref, rof_ref, rob_ref, rg_ref, x_ref, g1_ref, gn_ref, sh_ref, sc_ref,
                  wg_ref, bgate_ref, wb0_ref, wb1_ref, wb2_ref, wb3_ref, wo_ref, wr_ref, br_ref,
                  xo_ref, h2_ref, lp_ref, w_ref, tab_ref, cnt_ref, base):
    i = pl.program_id(0)
    tm = x_ref.shape[0]

    @pl.when(i == 0)
    def _():
        base[...] = jnp.zeros_like(base)

    ro = rof_ref[...] + rob_ref[...]
    gr = lax.broadcasted_iota(I32, (HEADS * DV, HEADS * DV), 0) // DV
    gc = lax.broadcasted_iota(I32, (HEADS * DV, HEADS * DV), 1) // DV
    avg = jnp.where(gr == gc, 1.0 / DV, 0.0).astype(BF16)

    def head_mean(a):
        hi, lo = _split_bf16(a)
        return (jnp.dot(hi, avg, preferred_element_type=F32) + jnp.dot(lo, avg, preferred_element_type=F32))

    mu = head_mean(ro)
    xc = ro - mu
    var = head_mean(xc * xc)
    rg = rg_ref[...]
    y3 = (rg * jax.nn.sigmoid(rg) * (xc * lax.rsqrt(var + EPS))).astype(BF16)

    hb = h_ref[...]
    ys = (y0_ref[...], y1_ref[...], y2_ref[...], y3)
    wbs = (wb0_ref, wb1_ref, wb2_ref, wb3_ref)
    parts = []
    for c0 in range(0, D, GATE_COLS):
        mc = None
        for b in range(4):
            lo = b * D + c0
            gl = jnp.dot(hb, wg_ref[:, lo:lo + GATE_COLS], preferred_element_type=F32)
            gate = jax.nn.sigmoid(gl + bgate_ref[:, lo:lo + GATE_COLS])
            t = gate * jnp.dot(ys[b], wbs[b][:, c0:c0 + GATE_COLS], preferred_element_type=F32)
            mc = t if mc is None else mc + t
        parts.append(mc.astype(BF16))
    yo = jnp.dot(jnp.concatenate(parts, axis=-1), wo_ref[...], preferred_element_type=F32)
    xn = x_ref[...] + g1_ref[...] * yo
    xo_ref[...] = xn
    yn = xn * lax.rsqrt(jnp.mean(xn * xn, axis=-1, keepdims=True) + EPS) * gn_ref[...]
    h2 = yn * (1.0 + sc_ref[...]) + sh_ref[...]
    h2_ref[...] = h2.astype(BF16)

    h_hi, h_lo = _split_bf16(h2)
    w_hi, w_lo = _split_bf16(wr_ref[...])
    nt = (((1,), (1,)), ((), ()))
    l2 = lax.dot_general(jnp.concatenate([w_hi, w_lo], axis=0), h_hi, nt, preferred_element_type=F32)
    lt = (l2[:N_EXPERTS] + l2[N_EXPERTS:]) + lax.dot_general(w_hi, h_lo, nt, preferred_element_type=F32)
    s = jax.nn.sigmoid(lt)
    sb = s + br_ref[...]
    r = [sb[e:e + 1, :] for e in range(N_EXPERTS)]
    sr = [s[e:e + 1, :] for e in range(N_EXPERTS)]
    gscore = []
    for g in range(N_GROUPS):
        a, b, c, d = r[EPG * g:EPG * (g + 1)]
        hi1, lo1, hi2, lo2 = jnp.maximum(a, b), jnp.minimum(a, b), jnp.maximum(c, d), jnp.minimum(c, d)
        gscore.append(jnp.maximum(hi1, hi2) + jnp.maximum(jnp.minimum(hi1, hi2), jnp.maximum(lo1, lo2)))
    gsel, _ = _first_max(gscore)
    v = [_pick([r[EPG * g + j] for g in range(N_GROUPS)], gsel) for j in range(EPG)]
    sv = [_pick([sr[EPG * g + j] for g in range(N_GROUPS)], gsel) for j in range(EPG)]
    i1, _ = _first_max(v)
    i2, _ = _first_max([jnp.where(i1 == j, -jnp.inf, v[j]) for j in range(EPG)])
    w1 = _pick(sv, i1)
    w2 = _pick(sv, i2)
    den = w1 + w2
    e1 = gsel * EPG + i1
    e2 = gsel * EPG + i2
    w_ref[0:1, :] = w1 / den
    w_ref[1:2, :] = w2 / den

    eio = lax.broadcasted_iota(I32, (N_EXPERTS, tm), 0)
    oh1 = eio == e1
    oh2 = eio == e2
    cnt = oh1.astype(F32) + oh2.astype(F32)
    ti = lax.broadcasted_iota(I32, (tm, tm), 0)
    tj = lax.broadcasted_iota(I32, (tm, tm), 1)
    before = jnp.where(ti < tj, 1.0, 0.0).astype(BF16)
    pref = jnp.dot(cnt.astype(BF16), before, preferred_element_type=F32)
    plen = jnp.floor((jnp.sum(cnt, axis=1, keepdims=True) + (RUN - 1.0)) * (1.0 / RUN)) * RUN
    eio1 = lax.broadcasted_iota(I32, (N_EXPERTS, 1), 0)
    loff = jnp.zeros((N_EXPERTS, 1), F32)
    run_start = jnp.zeros((1, 1), F32)
    for ex in range(N_EXPERTS):
        loff = jnp.where(eio1 == ex, run_start, loff)
        run_start = run_start + plen[ex:ex + 1, :]
    start = loff + pref
    lp_ref[0:1, :] = jnp.sum(jnp.where(oh1, start, 0.0), axis=0, keepdims=True).astype(I32)
    lp_ref[1:2, :] = jnp.sum(jnp.where(oh2, start, 0.0), axis=0, keepdims=True).astype(I32)
    goff = base[...]
    lane = lax.broadcasted_iota(I32, (1, LANES), 1)
    tab_ref[...] = jnp.where(lane == 0, plen, jnp.where(lane == 1, goff, 0.0)).astype(I32)
    base[...] = goff + plen
    cnt_ref[...] = jnp.broadcast_to(goff + plen, (N_EXPERTS, LANES)).astype(I32)


def _merge(h, y0, y1, y2, rof, rob, rg, x, g1, gn, sh, sc, wts):
    n = x.shape[0]
    tm = _row_tile(n, MOE_TILE)
    row = lambda c: pl.BlockSpec((tm, c), lambda i: (i, 0))
    full = lambda a: pl.BlockSpec(a.shape, lambda i: (0,) * a.ndim)
    tok = pl.BlockSpec((2, tm), lambda i: (0, i))
    return pl.pallas_call(
        _merge_kernel,
        grid=(n // tm,),
        in_specs=[row(D), row(BW), row(BW), row(HEADS * LANES),
                  row(HEADS * DV), row(HEADS * DV), row(BW), row(D),
                  full(g1), full(gn), full(sh), full(sc)] + [full(a) for a in wts],
        out_specs=[row(D), row(D), tok, tok,
                   pl.BlockSpec((N_EXPERTS, LANES), lambda i: (i, 0)),
                   pl.BlockSpec((N_EXPERTS, LANES), lambda i: (0, 0))],
        out_shape=[jax.ShapeDtypeStruct((n, D), F32), jax.ShapeDtypeStruct((n, D), BF16),
                   jax.ShapeDtypeStruct((2, n), I32), jax.ShapeDtypeStruct((2, n), F32),
                   jax.ShapeDtypeStruct((n // tm * N_EXPERTS, LANES), I32),
                   jax.ShapeDtypeStruct((N_EXPERTS, LANES), I32)],
        scratch_shapes=[pltpu.VMEM((N_EXPERTS, 1), F32)],
        compiler_params=_cparams("arbitrary"),
    )(h, y0, y1, y2, rof, rob, rg, x, g1, gn, sh, sc, *wts)


def _for_run_pieces(tab_ref, ps_ref, fn):
    loff = 0
    for ex in range(N_EXPERTS):
        plen = tab_ref[ex, 0]
        gbase = ps_ref[ex] + tab_ref[ex, 1]

        def piece(c, carry, loff=loff, gbase=gbase):
            fn(pl.multiple_of(loff + c * RUN, RUN), pl.multiple_of(gbase + c * RUN, RUN))
            return carry

        lax.fori_loop(0, plen // RUN, piece, 0)
        loff = loff + plen


def _dispatch_kernel(ps_ref, tab_ref, tail_ref, lp_ref, x_ref, xs_ref, buf, sem):
    rmax, tm = buf.shape[0], x_ref.shape[0]
    lp = lp_ref[...]
    r = lax.broadcasted_iota(I32, (rmax, tm), 0)
    sel = jnp.where((r == lp[0:1, :]) | (r == lp[1:2, :]), 1.0, 0.0).astype(BF16)
    buf[...] = jnp.dot(sel, x_ref[...], preferred_element_type=F32).astype(BF16)

    def copy(s, t):
        return pltpu.make_async_copy(buf.at[pl.ds(s, RUN)], xs_ref.at[pl.ds(t, RUN)], sem)

    _for_run_pieces(tab_ref, ps_ref, lambda s, t: copy(s, t).start())
    _for_run_pieces(tab_ref, ps_ref, lambda s, t: copy(s, t).wait())

    @pl.when(pl.program_id(0) == pl.num_programs(0) - 1)
    def _():
        buf[0:MOE_BLOCK, :] = jnp.zeros((MOE_BLOCK, D), BF16)

        def tails(fn):
            for ex in range(N_EXPERTS):
                first = tail_ref[0, ex]

                def piece(c, carry, first=first):
                    fn(pl.multiple_of(first + c * RUN, RUN))
                    return carry

                lax.fori_loop(0, tail_ref[1, ex] // RUN, piece, 0)

        def block_copy(b):
            dst = xs_ref.at[pl.ds(pl.multiple_of(b * MOE_BLOCK, MOE_BLOCK), MOE_BLOCK)]
            return pltpu.make_async_copy(buf.at[pl.ds(0, MOE_BLOCK)], dst, sem)

        def unused(fn):
            def blk(b, carry):
                fn(b)
                return carry
            lax.fori_loop(tail_ref[2, 0], xs_ref.shape[0] // MOE_BLOCK, blk, 0)

        tails(lambda t: copy(0, t).start())
        unused(lambda b: block_copy(b).start())
        tails(lambda t: copy(0, t).wait())
        unused(lambda b: block_copy(b).wait())


def _tile_tables(tm):
    return [pl.BlockSpec(memory_space=pltpu.SMEM),
            pl.BlockSpec((N_EXPERTS, LANES), lambda i: (i, 0), memory_space=pltpu.SMEM)]


def _dispatch(h2, lp, tab, pstart, tails, cap, rmax):
    n = h2.shape[0]
    tm = _row_tile(n, MOE_TILE)
    return pl.pallas_call(
        _dispatch_kernel,
        grid=(n // tm,),
        in_specs=_tile_tables(tm) + [pl.BlockSpec(memory_space=pltpu.SMEM),
                                     pl.BlockSpec((2, tm), lambda i: (0, i)),
                                     pl.BlockSpec((tm, D), lambda i: (i, 0))],
        out_specs=pl.BlockSpec(memory_space=pl.ANY),
        out_shape=jax.ShapeDtypeStruct((cap, D), BF16),
        scratch_shapes=[pltpu.VMEM((rmax, D), BF16), pltpu.SemaphoreType.DMA(())],
        compiler_params=_cparams("arbitrary"),
    )(pstart, tab, tails, lp, h2)


def _expert_kernel(be_ref, nu_ref, x_ref, w1_ref, w3_ref, w2_ref, y_ref, w1b, w3b, w2b):
    b = pl.program_id(0)

    @pl.when((b == 0) | (be_ref[b] != be_ref[jnp.maximum(b - 1, 0)]))
    def _():
        w1b[...] = w1_ref[0, 0].astype(BF16)
        w3b[...] = w3_ref[0, 0].astype(BF16)
        w2b[...] = w2_ref[0, 0].astype(BF16)

    @pl.when(b < nu_ref[0])
    def _():
        x = x_ref[...]
        a = jnp.dot(x, w1b[...], preferred_element_type=F32)
        c = jnp.dot(x, w3b[...], preferred_element_type=F32)
        u = (a * jax.nn.sigmoid(a) * c).astype(BF16)
        y_ref[...] = jnp.dot(u, w2b[...], preferred_element_type=F32).astype(BF16)

    @pl.when(b >= nu_ref[0])
    def _():
        y_ref[...] = jnp.zeros_like(y_ref)


def _experts(xs, blk_e, nused, l, w1, w3, w2):
    cap = xs.shape[0]
    nb = cap // MOE_BLOCK
    xmap = lambda b, be, nu: (jnp.minimum(b, jnp.maximum(nu[0] - 1, 0)), 0)
    wmap = lambda b, be, nu: (l, be[b], 0, 0)
    return pl.pallas_call(
        _expert_kernel,
        grid_spec=pltpu.PrefetchScalarGridSpec(
            num_scalar_prefetch=2,
            grid=(nb,),
            in_specs=[pl.BlockSpec((MOE_BLOCK, D), xmap),
                      pl.BlockSpec((1, 1, D, D_FF), wmap),
                      pl.BlockSpec((1, 1, D, D_FF), wmap),
                      pl.BlockSpec((1, 1, D_FF, D), wmap)],
            out_specs=pl.BlockSpec((MOE_BLOCK, D), lambda b, be, nu: (b, 0)),
            scratch_shapes=[pltpu.VMEM((D, D_FF), BF16), pltpu.VMEM((D, D_FF), BF16),
                            pltpu.VMEM((D_FF, D), BF16)]),
        out_shape=jax.ShapeDtypeStruct((cap, D), BF16),
        compiler_params=_cparams("arbitrary"),
    )(blk_e, nused, xs, w1, w3, w2)


def _combine_kernel(ps_ref, tab_ref, lp_ref, w_ref, x_ref, g2_ref, gf_ref, ys_ref, o_ref, buf, sem, *,
                    final_norm):
    rmax, tm = buf.shape[0], x_ref.shape[0]

    @pl.when(pl.program_id(0) == 0)
    def _():
        buf[...] = jnp.zeros_like(buf)

    def copy(s, t):
        return pltpu.make_async_copy(ys_ref.at[pl.ds(t, RUN)], buf.at[pl.ds(s, RUN)], sem)

    _for_run_pieces(tab_ref, ps_ref, lambda s, t: copy(s, t).start())
    lp = lp_ref[...]
    w = w_ref[...]
    r = lax.broadcasted_iota(I32, (tm, rmax), 1)
    mix = (jnp.where(r == lp[:, 0:1], w[:, 0:1], 0.0)
           + jnp.where(r == lp[:, 1:2], w[:, 1:2], 0.0)).astype(BF16)
    _for_run_pieces(tab_ref, ps_ref, lambda s, t: copy(s, t).wait())
    ml = jnp.dot(mix, buf[...], preferred_element_type=F32)
    xo = x_ref[...] + g2_ref[...] * ml
    if final_norm:
        xo = xo * lax.rsqrt(jnp.mean(xo * xo, axis=-1, keepdims=True) + EPS) * gf_ref[...]
    o_ref[...] = xo


def _combine(x, g2, lpt, wtok, tab, pstart, ys, gf, final_norm, rmax):
    n = x.shape[0]
    tm = _row_tile(n, MOE_TILE)
    vec = pl.BlockSpec((1, D), lambda i: (0, 0))
    tok = pl.BlockSpec((tm, 2), lambda i: (i, 0))
    return pl.pallas_call(
        functools.partial(_combine_kernel, final_norm=final_norm),
        grid=(n // tm,),
        in_specs=_tile_tables(tm) + [tok, tok, pl.BlockSpec((tm, D), lambda i: (i, 0)), vec, vec,
                                     pl.BlockSpec(memory_space=pl.ANY)],
        out_specs=pl.BlockSpec((tm, D), lambda i: (i, 0)),
        out_shape=jax.ShapeDtypeStruct((n, D), F32),
        scratch_shapes=[pltpu.VMEM((rmax, D), BF16), pltpu.SemaphoreType.DMA(())],
        compiler_params=_cparams("arbitrary"),
    )(pstart, tab, lpt, wtok, x, g2, gf, ys)


def _moe(x, h2, lp, wsel, tab, filled, g2, l, w1, w3, w2, gf, final_norm):
    n = x.shape[0]
    tm = _row_tile(n, MOE_TILE)
    rmax = 2 * tm + N_EXPERTS * RUN
    worst = n // tm * (2 * tm + N_EXPERTS * (RUN - 1)) + N_EXPERTS * (MOE_BLOCK - RUN)
    cap = (worst + MOE_BLOCK - 1) // MOE_BLOCK * MOE_BLOCK
    nb = cap // MOE_BLOCK
    region = (filled + MOE_BLOCK - 1) // MOE_BLOCK * MOE_BLOCK
    pend = jnp.cumsum(region)
    pstart = (pend - region).astype(I32)
    blk_start = jnp.arange(nb, dtype=I32) * MOE_BLOCK
    blk_e = jnp.minimum(jnp.sum(pend[None, :] <= blk_start[:, None], axis=-1), N_EXPERTS - 1).astype(I32)
    nused = (pend[-1:] // MOE_BLOCK).astype(I32)
    tails = jnp.stack([pstart + filled, region - filled, jnp.broadcast_to(nused, (N_EXPERTS,))]).astype(I32)
    xs = _dispatch(h2, lp, tab, pstart, tails, cap, rmax)
    ys = _experts(xs, blk_e, nused, l, w1, w3, w2)
    return _combine(x, g2, lp.T, wsel.T, tab, pstart, ys, gf, final_norm, rmax)


def _rope_tables(n):
    rows = n // GRID_W
    row = jnp.repeat(jnp.arange(rows, dtype=F32), GRID_W)
    col = jnp.tile(jnp.arange(GRID_W, dtype=F32), rows)
    nf = DK // 4
    inv = ROPE_BASE ** (-jnp.arange(nf, dtype=F32) / nf)
    ang = jnp.concatenate([row[:, None] * inv, col[:, None] * inv], axis=-1)
    cos, sin = jnp.cos(ang), jnp.sin(ang)
    reps = LANES // DK
    return (jnp.tile(jnp.concatenate([cos, cos], -1), (1, reps)),
            jnp.tile(jnp.concatenate([-sin, sin], -1), (1, reps)))


def _layer_weights(l, w_in, w_gate, b_gate, w_branch, w_o, w_router, b_router):
    w = w_in[l].astype(BF16)
    wb = w_branch[l]
    wb2 = wb[2].reshape(HEADS, DV, D)
    wb2 = jnp.concatenate([wb2, jnp.zeros((HEADS, LANES - DV, D), F32)], axis=1).reshape(HEADS * LANES, D)
    merge_w = (w_gate[l].astype(BF16), b_gate[l].reshape(1, 4 * D), wb[0].astype(BF16), wb[1].astype(BF16),
               wb2.astype(BF16), wb[3].astype(BF16), w_o[l].astype(BF16),
               w_router.T, b_router.reshape(N_EXPERTS, 1))
    return w, merge_w


def kernel(x, c, ctx, c_ctx, w_mod, b_mod, g_norm1, g_norm2, w_in, conv_a_w, conv_a_b, conv_a_g, conv_a_beta, conv_b_w, lam_q1, lam_k1, lam_q2, lam_k2, diff_g, ret_ld_f, ret_ld_b, w_gate, b_gate, w_branch, w_o, w_router, b_router, w1_e, w3_e, w2_e, g_final):
    assert x.shape[0] == 1 and ctx.shape[0] == 1
    xl, xc = x[0], ctx[0]
    n_lat, n_ctx = xl.shape[0], xc.shape[0]
    mods = _adaln(c, c_ctx, w_mod, b_mod)
    cos, sinp = _rope_tables(n_lat)
    zc = jnp.zeros((n_ctx, LANES), F32)
    vone = jnp.zeros((HEADS, LANES), F32).at[:, DV].set(1.0).reshape(1, HEADS * LANES)
    gf = g_final.reshape(1, D)
    vrow = lambda a: a.reshape(1, -1)

    for l in range(DEPTH):
        last = l == DEPTH - 1
        lam_init = 0.8 - 0.6 * math.exp(-0.3 * l)
        ml = [mods[l, 0:1, j * D:(j + 1) * D] for j in range(6)]
        mc = [mods[l, 1:2, j * D:(j + 1) * D] for j in range(6)]
        w, merge_w = _layer_weights(l, w_in, w_gate, b_gate, w_branch, w_o, w_router, b_router)
        g1 = vrow(g_norm1[l])
        g2n = vrow(g_norm2[l])
        lam = (jnp.exp(jnp.sum(lam_q1[l] * lam_k1[l])) - jnp.exp(jnp.sum(lam_q2[l] * lam_k2[l]))
               + lam_init).reshape(1, 1).astype(F32)
        gpad = jnp.concatenate([diff_g[l], jnp.zeros((LANES - DV,), F32)]).reshape(1, LANES)
        ld = jnp.stack([ret_ld_f[l], ret_ld_b[l]]).astype(F32)
        conv_w = (conv_a_w[l], vrow(conv_a_b[l]), vrow(conv_a_g[l]), vrow(conv_a_beta[l]), conv_b_w[l])

        (hc, uc, bgc, cxc, qc, kc, vc, rqc, rkc, rvc, rgc) = _inproj(
            xc, g1, mc[0], mc[1], w, zc, zc, vone, rope=False)
        (hl, ul, bgl, cxl, ql, kl, vl, rql, rkl, rvl, rgl) = _inproj(
            xl, g1, ml[0], ml[1], w, cos, sinp, vone, rope=True)

        kall = jnp.concatenate([kc, kl], axis=0)
        vall = jnp.concatenate([vc, vl], axis=0)
        out_scale = 1.0 - lam_init
        y2l = _attention(ql, kall, vall, lam, gpad, out_scale)
        s0 = jnp.zeros((2, HEADS * DK, HEADS * DV), F32)
        y0c, y1c, rofc, robc, sfin = _convs_retention(uc, cxc, bgc, conv_w, rqc, rkc, rvc, ld, s0)
        y0l, y1l, rofl, robl, _ = _convs_retention(ul, cxl, bgl, conv_w, rql, rkl, rvl, ld, sfin)
        xl, h2l, lpl, wl, tabl, cl = _merge(hl, y0l, y1l, y2l, rofl, robl, rgl, xl, ml[2], g2n, ml[3], ml[4], merge_w)
        xl = _moe(xl, h2l, lpl, wl, tabl, cl[:, 0], ml[5], l, w1_e, w3_e, w2_e, gf, last)
        if not last:
            y2c = _attention(qc, kc, vc, lam, gpad, out_scale)
            xc, h2c, lpc, wc, tabc, cc = _merge(hc, y0c, y1c, y2c, rofc, robc, rgc, xc, mc[2], g2n, mc[3], mc[4], merge_w)
            xc = _moe(xc, h2c, lpc, wc, tabc, cc[:, 0], mc[5], l, w1_e, w3_e, w2_e, gf, False)
    return xl[None]
```

```python
import functools
import itertools
import math

import jax
import jax.numpy as jnp
from jax import lax
from jax.experimental import pallas as pl
from jax.experimental.pallas import tpu as pltpu

F32 = jnp.float32
BF16 = jnp.bfloat16
I32 = jnp.int32
HI = lax.Precision.HIGHEST

D = 1024
DEPTH = 2
GRID_W = 64
BW = 256
CONF_K = 31
SCONV_K = 3
HEADS = 4
DK = 32
DV = 64
CHUNK = 128
ROPE_BASE = 10000.0
N_EXPERTS = 16
N_GROUPS = 4
EPG = N_EXPERTS // N_GROUPS
D_FF = 512
MOE_BLOCK = 512
EPS = 1e-6
LOG2E = math.log2(math.e)
BOUND_SLACK = 1.0 + 2.0 ** -10
L_MIN = 2.0 ** -64
ATTN_TQ = 512
ATTN_TK = (3328, 1280, 256)
ATTN_TILE = 512 * 3328
RET_ROWS = 1024
GATE_COLS = 512
MOE_TILE = 512
RUN = 16
HALO = 16
LANES = 128
SUBLANES = 8
QK_W = HEADS * 2 * DK
VPAD_W = HEADS * LANES
RQ_W = HEADS * DK
RV_W = HEADS * DV
IN_OFFS = tuple(itertools.accumulate((0, 2 * BW, 3 * BW, QK_W, QK_W, RV_W, RQ_W, RQ_W, RV_W, BW)))
W_COLS = IN_OFFS[-1]
INPROJ_TILE = 1024
VMEM_LIMIT = 56 * 1024 * 1024


def _cparams(*sem):
    return pltpu.CompilerParams(dimension_semantics=sem, vmem_limit_bytes=VMEM_LIMIT)


def _row_tile(n, pref):
    return pref if n % pref == 0 else n


def _mod_kernel(c_ref, w_ref, b_ref, o_ref):
    a = c_ref[...]
    a = a * jax.nn.sigmoid(a)
    o_ref[0] = jnp.dot(a, w_ref[0], preferred_element_type=F32, precision=HI) + b_ref[0]


def _adaln(c, c_ctx, w_mod, b_mod):
    cs = jnp.zeros((8, D), F32).at[0].set(c[0]).at[1].set(c_ctx)
    return pl.pallas_call(
        _mod_kernel,
        grid=(DEPTH, 6),
        in_specs=[pl.BlockSpec((8, D), lambda l, j: (0, 0)),
                  pl.BlockSpec((1, D, D), lambda l, j: (l, 0, j)),
                  pl.BlockSpec((1, 1, D), lambda l, j: (l, 0, j))],
        out_specs=pl.BlockSpec((1, 8, D), lambda l, j: (l, 0, j)),
        out_shape=jax.ShapeDtypeStruct((DEPTH, 8, 6 * D), F32),
        compiler_params=_cparams("parallel", "parallel"),
    )(cs, w_mod, b_mod.reshape(DEPTH, 1, 6 * D))


def _rope(x, cos, sinp):
    lane = lax.broadcasted_iota(I32, (1, LANES), 1)
    first = (lane % 32) < 16
    outs = []
    for c in range(x.shape[1] // LANES):
        xc = x[:, c * LANES:(c + 1) * LANES]
        sw = jnp.where(first, pltpu.roll(xc, LANES - 16, 1), pltpu.roll(xc, 16, 1))
        outs.append(xc * cos + sw * sinp)
    return outs[0] if len(outs) == 1 else jnp.concatenate(outs, axis=-1)


def _inproj_kernel(x_ref, g_ref, sh_ref, sc_ref, w_ref, cos_ref, sin_ref, vone_ref,
                   h_ref, u_ref, bg_ref, cx_ref, q_ref, k_ref, v_ref,
                   rq_ref, rk_ref, rv_ref, rg_ref, *, rope):
    x = x_ref[...]
    y = x * lax.rsqrt(jnp.mean(x * x, axis=-1, keepdims=True) + EPS) * g_ref[...]
    hb = (y * (1.0 + sc_ref[...]) + sh_ref[...]).astype(BF16)
    h_ref[...] = hb

    def seg(g):
        return jnp.dot(hb, w_ref[:, IN_OFFS[g]:IN_OFFS[g + 1]], preferred_element_type=F32)

    z = seg(0)
    u_ref[...] = z[:, :BW] * jax.nn.sigmoid(z[:, BW:])
    z = seg(1)
    bg_ref[...] = z[:, :BW]
    cx_ref[...] = z[:, BW:2 * BW] * z[:, 2 * BW:]
    zq = seg(2)
    zk = seg(3)
    zrq = seg(5)
    zrk = seg(6)
    if rope:
        cos = cos_ref[...]
        sinp = sin_ref[...]
        zq, zk = _rope(zq, cos, sinp), _rope(zk, cos, sinp)
        zrq, zrk = _rope(zrq, cos, sinp), _rope(zrk, cos, sinp)
    q_ref[...] = (zq * (DK ** -0.5 * LOG2E)).astype(BF16)
    k_ref[...] = zk.astype(BF16)
    vr = lax.broadcasted_iota(I32, (HEADS * DV, HEADS * LANES), 0)
    vc = lax.broadcasted_iota(I32, (HEADS * DV, HEADS * LANES), 1)
    spread = jnp.where((vc // LANES == vr // DV) & (vc % LANES == vr % DV), 1.0, 0.0).astype(BF16)
    zv = seg(4).astype(BF16)
    v_ref[...] = (jnp.dot(zv, spread, preferred_element_type=F32) + vone_ref[...]).astype(BF16)
    rq_ref[...] = zrq
    rk_ref[...] = zrk * (DK ** -0.5)
    rv_ref[...] = seg(7)
    rg_ref[...] = seg(8)


def _inproj(x, g, shift, scale, w, cos, sinp, vone, rope):
    n = x.shape[0]
    tm = _row_tile(n, INPROJ_TILE)
    row = lambda c: pl.BlockSpec((tm, c), lambda i: (i, 0))
    vec = lambda c: pl.BlockSpec((1, c), lambda i: (0, 0))
    widths = [(D, BF16), (BW, F32), (BW, F32), (BW, F32), (QK_W, BF16), (QK_W, BF16), (VPAD_W, BF16),
              (RQ_W, F32), (RQ_W, F32), (RV_W, F32), (BW, F32)]
    return pl.pallas_call(
        functools.partial(_inproj_kernel, rope=rope),
        grid=(n // tm,),
        in_specs=[row(D), vec(D), vec(D), vec(D),
                  pl.BlockSpec((D, W_COLS), lambda i: (0, 0)),
                  row(LANES), row(LANES), vec(VPAD_W)],
        out_specs=[row(c) for c, _ in widths],
        out_shape=[jax.ShapeDtypeStruct((n, c), dt) for c, dt in widths],
        compiler_params=_cparams("parallel"),
    )(x, g, shift, scale, w, cos, sinp, vone)


def _conv_kernel(up_ref, um_ref, un_ref, cp_ref, cm_ref, cn_ref, bg_ref,
                 wa_ref, ba_ref, ga_ref, bta_ref, wb_ref, y0_ref, y1_ref, eu, ec):
    i = pl.program_id(0)
    last = pl.num_programs(0) - 1
    tm = um_ref.shape[0]
    pad_a = (CONF_K - 1) // 2
    pad_b = (SCONV_K - 1) // 2
    span = tm + 2 * HALO - SUBLANES
    for ext, p_ref, m_ref, n_ref, shifts in ((eu, up_ref, um_ref, un_ref, range(1, SUBLANES)),
                                             (ec, cp_ref, cm_ref, cn_ref, (1, SUBLANES - 1))):
        ext[0, 0:HALO, :] = jnp.where(i > 0, p_ref[...], 0.0)
        ext[0, HALO:HALO + tm, :] = m_ref[...]
        ext[0, HALO + tm:, :] = jnp.where(i < last, n_ref[...], 0.0)
        for s in shifts:
            ext[s, 0:span, :] = ext[0, pl.ds(s, span), :]

    def tap(ext, r0, off):
        return ext[off % SUBLANES, pl.ds(r0 + off - off % SUBLANES, rc), :]

    rc = 32
    for r0 in range(0, tm, rc):
        acc = jnp.zeros((rc, BW), F32)
        for k in range(CONF_K):
            acc = acc + tap(eu, r0, HALO + k - pad_a) * wa_ref[k:k + 1, :]
        acc = acc + ba_ref[...]
        mu = jnp.mean(acc, axis=-1, keepdims=True)
        xc = acc - mu
        var = jnp.mean(xc * xc, axis=-1, keepdims=True)
        yn = xc * lax.rsqrt(var + EPS) * ga_ref[...] + bta_ref[...]
        y0_ref[pl.ds(r0, rc), :] = (yn * jax.nn.sigmoid(yn)).astype(BF16)
        accb = jnp.zeros((rc, BW), F32)
        for k in range(SCONV_K):
            accb = accb + tap(ec, r0, HALO + k - pad_b) * wb_ref[k:k + 1, :]
        y1_ref[pl.ds(r0, rc), :] = (bg_ref[pl.ds(r0, rc), :] * accb).astype(BF16)


def _attn_kernel(lam_ref, q_ref, k_ref, v_ref, g_ref, o_ref, kmax_sc, acc_sc, *, tk, nk, nq, out_scale):
    s = pl.program_id(0)
    h = jnp.minimum(s, HEADS * nq - 1) // nq
    cur = acc_sc.at[s % 2]
    prev = acc_sc.at[(s + 1) % 2]
    kc = HEADS * 2 * DK

    @pl.when(s == 0)
    def _():
        def colmax(j, mx):
            blk = k_ref[pl.ds(pl.multiple_of(j * kc, kc), kc), :].astype(F32)
            return jnp.maximum(mx, jnp.max(jnp.abs(blk), axis=0, keepdims=True))
        kmax_sc[...] = lax.fori_loop(0, nk * tk // kc, colmax, jnp.zeros((1, kc), F32))
        acc_sc[...] = jnp.ones(acc_sc.shape, F32)

    q = q_ref[...]
    tq = q.shape[0]
    lane_q = lax.broadcasted_iota(I32, (1, kc), 1)
    lane_o = lax.broadcasted_iota(I32, (1, LANES), 1)
    qbound = jnp.abs(q.astype(F32)) * kmax_sc[...]
    qms, shifts = [], []
    for m in range(2):
        lo = (h * 2 + m) * DK
        sel = (lane_q >= lo) & (lane_q < lo + DK)
        qms.append(jnp.where(sel, q, jnp.zeros_like(q)))
        ub = jnp.sum(jnp.where(sel, qbound, 0.0), axis=-1, keepdims=True) * BOUND_SLACK
        shifts.append(jnp.tile(jnp.broadcast_to(ub, (tq, LANES)), (1, tk // LANES)))

    def tiles(j):
        start = j * tk if isinstance(j, int) else pl.multiple_of(j * tk, tk)
        return k_ref[pl.ds(start, tk), :], v_ref[pl.ds(start, tk), :]

    def scores(m, kj):
        return lax.dot_general(qms[m], kj, (((1,), (1,)), ((), ())), preferred_element_type=F32)

    outs = [prev[m] / prev[m][:, DV:DV + 1] for m in range(2)]
    o = outs[0] - lam_ref[0, 0] * outs[1]
    valid = lane_o < DV
    ms = jnp.sum(jnp.where(valid, o * o, 0.0), axis=-1, keepdims=True) * (1.0 / DV)
    y = o * lax.rsqrt(ms + EPS) * g_ref[...] * out_scale
    o_ref[...] = jnp.where(valid, y, 0.0).astype(BF16)

    cur[...] = jnp.zeros(cur.shape, F32)

    def shifted(j, carry):
        kj, vj = tiles(j)
        for m in range(2):
            p = jnp.exp2(scores(m, kj) - shifts[m]).astype(BF16)
            cur[m] += jnp.dot(p, vj, preferred_element_type=F32)
        return carry

    for j in range(nk):
        shifted(j, 0)
    lmin = jnp.minimum(jnp.min(cur[0][:, DV:DV + 1]), jnp.min(cur[1][:, DV:DV + 1]))

    @pl.when(jnp.logical_not(lmin >= L_MIN))
    def _():
        def online(j, carry):
            kj, vj = tiles(j)
            new = []
            for m in range(2):
                mx, acc = carry[m]
                s = scores(m, kj)
                mnew = jnp.maximum(mx, jnp.max(s, axis=-1, keepdims=True))
                p = jnp.exp2(s - mnew).astype(BF16)
                new.append((mnew, jnp.exp2(mx - mnew) * acc + jnp.dot(p, vj, preferred_element_type=F32)))
            return tuple(new)

        init = (jnp.full((tq, 1), -jnp.inf, F32), jnp.zeros((tq, LANES), F32))
        res = lax.fori_loop(0, nk, online, (init, init))
        for m in range(2):
            cur[m] = res[m][1]


def _attention(q, k, v, lam, gpad, out_scale):
    n = q.shape[0]
    nkeys = k.shape[0]
    tq = _row_tile(n, ATTN_TQ)
    nq = n // tq
    last = HEADS * nq - 1
    tk = next(t for t in ATTN_TK if nkeys % t == 0 and t * tq <= ATTN_TILE)
    kern = functools.partial(_attn_kernel, tk=tk, nk=nkeys // tk, nq=nq, out_scale=out_scale)
    work = lambda s: jnp.minimum(s, last)
    done = lambda s: jnp.maximum(s - 1, 0)
    return pl.pallas_call(
        kern,
        grid=(HEADS * nq + 1,),
        in_specs=[pl.BlockSpec(memory_space=pltpu.SMEM),
                  pl.BlockSpec((tq, QK_W), lambda s: (work(s) % nq, 0)),
                  pl.BlockSpec((nkeys, QK_W), lambda s: (0, 0)),
                  pl.BlockSpec((nkeys, LANES), lambda s: (0, work(s) // nq)),
                  pl.BlockSpec((1, LANES), lambda s: (0, 0))],
        out_specs=pl.BlockSpec((tq, LANES), lambda s: (done(s) % nq, done(s) // nq)),
        out_shape=jax.ShapeDtypeStruct((n, HEADS * LANES), BF16),
        scratch_shapes=[pltpu.VMEM((1, HEADS * 2 * DK), F32), pltpu.VMEM((2, 2, tq, LANES), F32)],
        compiler_params=_cparams("arbitrary"),
    )(lam, q, k, v, gpad)


def _ret_kernel(ld_ref, qf_ref, kf_ref, vf_ref, qb_ref, kb_ref, vb_ref, s0_ref, of_ref, ob_ref, sf_ref,
                state, dmask, qdec, kdec, cdec, *, after_init=None):
    n = pl.program_id(0)
    c = CHUNK
    lane_q = lax.broadcasted_iota(I32, (1, HEADS * DK), 1)
    lane_v = lax.broadcasted_iota(I32, (1, HEADS * DV), 1)

    @pl.when(n == 0)
    def _():
        state[...] = s0_ref[...]
        row_h = lax.broadcasted_iota(I32, (HEADS * DK, 1), 0) // DK
        for d in range(2):
            pi = lax.broadcasted_iota(I32, (c, 1), 0).astype(F32)
            pj = lax.broadcasted_iota(I32, (1, c), 1).astype(F32)
            if d == 1:
                pi, pj = c - 1.0 - pi, c - 1.0 - pj
            diff = pi - pj
            lg_q = jnp.zeros((1, HEADS * DK), F32)
            lg_r = jnp.zeros((HEADS * DK, 1), F32)
            for h in range(HEADS):
                lg = ld_ref[d, h]
                dmask[d, h] = jnp.exp(jnp.where(diff >= 0, diff * lg, -jnp.inf))
                lg_q = jnp.where(lane_q // DK == h, lg, lg_q)
                lg_r = jnp.where(row_h == h, lg, lg_r)
            qdec[d] = jnp.exp((pi + 1.0) * lg_q)
            kdec[d] = jnp.exp((c - 1.0 - pi) * lg_q)
            cdec[d] = jnp.where(row_h == lane_v // DV, jnp.exp(c * lg_r), 0.0)

    if after_init is not None:
        after_init()

    for d, (q_ref, k_ref, v_ref, o_ref) in enumerate(((qf_ref, kf_ref, vf_ref, of_ref),
                                                      (qb_ref, kb_ref, vb_ref, ob_ref))):
        nsub = q_ref.shape[0] // c
        cd = cdec[d]
        s_cur = state[d]
        for sub in (range(nsub) if d == 0 else reversed(range(nsub))):
            rows = pl.ds(sub * c, c)
            q = q_ref[rows, :]
            k = k_ref[rows, :]
            kb = k.astype(BF16)
            vb = v_ref[rows, :].astype(BF16)
            o = jnp.dot((q * qdec[d]).astype(BF16), s_cur.astype(BF16), preferred_element_type=F32)
            for h in range(HEADS):
                qm = jnp.where(lane_q // DK == h, q, 0.0).astype(BF16)
                sc = lax.dot_general(qm, kb, (((1,), (1,)), ((), ())),
                                     preferred_element_type=F32) * dmask[d, h]
                oh = jnp.dot(sc.astype(BF16), vb, preferred_element_type=F32)
                o = o + jnp.where(lane_v // DV == h, oh, 0.0)
            o_ref[rows, :] = o
            kv = lax.dot_general((k * kdec[d]).astype(BF16), vb, (((0,), (0,)), ((), ())),
                                 preferred_element_type=F32)
            s_cur = cd * s_cur + jnp.where(cd != 0.0, kv, 0.0)
        state[d] = s_cur

    @pl.when(n == pl.num_programs(0) - 1)
    def _():
        sf_ref[...] = state[...]


N_CONV_IN, N_RET_IN, N_CONV_OUT, N_RET_OUT, N_CONV_SCR = 12, 8, 2, 3, 2


def _convret_kernel(*refs):
    cin, rin = refs[:N_CONV_IN], refs[N_CONV_IN:N_CONV_IN + N_RET_IN]
    outs = refs[N_CONV_IN + N_RET_IN:]
    cout, rout = outs[:N_CONV_OUT], outs[N_CONV_OUT:N_CONV_OUT + N_RET_OUT]
    scr = outs[N_CONV_OUT + N_RET_OUT:]
    _ret_kernel(*rin, *rout, *scr[N_CONV_SCR:],
                after_init=lambda: _conv_kernel(*cin, *cout, *scr[:N_CONV_SCR]))


def _convs_retention(u, cx, bg, conv_w, rq, rk, rv, ld, s0):
    n = u.shape[0]
    rows = _row_tile(n, RET_ROWS)
    nc = n // rows
    hb = rows // HALO
    nh = n // HALO
    prev = pl.BlockSpec((HALO, BW), lambda i: (jnp.maximum(i * hb - 1, 0), 0))
    nxt = pl.BlockSpec((HALO, BW), lambda i: (jnp.minimum((i + 1) * hb, nh - 1), 0))
    vec = lambda r: pl.BlockSpec((r, BW), lambda i: (0, 0))
    fwd = lambda w: pl.BlockSpec((rows, w), lambda i: (i, 0))
    bwd = lambda w: pl.BlockSpec((rows, w), lambda i: (nc - 1 - i, 0))
    sshape = (2, HEADS * DK, HEADS * DV)
    whole = pl.BlockSpec(sshape, lambda i: (0, 0, 0))
    dq, dv = HEADS * DK, HEADS * DV
    main = fwd(BW)
    return pl.pallas_call(
        _convret_kernel,
        grid=(nc,),
        in_specs=[prev, main, nxt, prev, main, nxt, main,
                  vec(CONF_K), vec(1), vec(1), vec(1), vec(SCONV_K),
                  pl.BlockSpec(memory_space=pltpu.SMEM),
                  fwd(dq), fwd(dq), fwd(dv), bwd(dq), bwd(dq), bwd(dv), whole],
        out_specs=[main, main, fwd(dv), bwd(dv), whole],
        out_shape=[jax.ShapeDtypeStruct((n, BW), BF16), jax.ShapeDtypeStruct((n, BW), BF16),
                   jax.ShapeDtypeStruct((n, dv), F32), jax.ShapeDtypeStruct((n, dv), F32),
                   jax.ShapeDtypeStruct(sshape, F32)],
        scratch_shapes=[pltpu.VMEM((SUBLANES, rows + 2 * HALO, BW), F32)] * N_CONV_SCR
                       + [pltpu.VMEM(sshape, F32),
                          pltpu.VMEM((2, HEADS, CHUNK, CHUNK), F32),
                          pltpu.VMEM((2, CHUNK, dq), F32),
                          pltpu.VMEM((2, CHUNK, dq), F32),
                          pltpu.VMEM(sshape, F32)],
        compiler_params=_cparams("arbitrary"),
    )(u, u, u, cx, cx, cx, bg, *conv_w, ld, rq, rk, rv, rq, rk, rv, s0)


def _first_max(vals):
    idx = jnp.zeros(vals[0].shape, I32)
    best = vals[0]
    for j in range(1, len(vals)):
        upd = vals[j] > best
        idx = jnp.where(upd, j, idx)
        best = jnp.where(upd, vals[j], best)
    return idx, best


def _split_bf16(a):
    hi = a.astype(BF16)
    return hi, (a - hi.astype(F32)).astype(BF16)


def _pick(vals, idx):
    out = vals[-1]
    for j in range(len(vals) - 2, -1, -1):
        out = jnp.where(idx == j, vals[j], out)
    return out


def _merge_kernel(h_ref, y0_ref, y1_ref, y2_ref, rof_ref, rob_ref, rg_ref, x_ref, g1_ref, gn_ref, sh_ref, sc_ref,
                  wg_ref, bgate_ref, wb0_ref, wb1_ref, wb2_ref, wb3_ref, wo_ref, wr_ref, br_ref,
                  xo_ref, h2_ref, lp_ref, w_ref, tab_ref, cnt_ref, base):
    i = pl.program_id(0)
    tm = x_ref.shape[0]

    @pl.when(i == 0)
    def _():
        base[...] = jnp.zeros_like(base)

    ro = rof_ref[...] + rob_ref[...]
    gr = lax.broadcasted_iota(I32, (HEADS * DV, HEADS * DV), 0) // DV
    gc = lax.broadcasted_iota(I32, (HEADS * DV, HEADS * DV), 1) // DV
    avg = jnp.where(gr == gc, 1.0 / DV, 0.0).astype(BF16)

    def head_mean(a):
        hi, lo = _split_bf16(a)
        return (jnp.dot(hi, avg, preferred_element_type=F32) + jnp.dot(lo, avg, preferred_element_type=F32))

    mu = head_mean(ro)
    xc = ro - mu
    var = head_mean(xc * xc)
    rg = rg_ref[...]
    y3 = (rg * jax.nn.sigmoid(rg) * (xc * lax.rsqrt(var + EPS))).astype(BF16)

    hb = h_ref[...]
    ys = (y0_ref[...], y1_ref[...], y2_ref[...], y3)
    wbs = (wb0_ref, wb1_ref, wb2_ref, wb3_ref)
    parts = []
    for c0 in range(0, D, GATE_COLS):
        mc = None
        for b in range(4):
            lo = b * D + c0
            gl = jnp.dot(hb, wg_ref[:, lo:lo + GATE_COLS], preferred_element_type=F32)
            gate = jax.nn.sigmoid(gl + bgate_ref[:, lo:lo + GATE_COLS])
            t = gate * jnp.dot(ys[b], wbs[b][:, c0:c0 + GATE_COLS], preferred_element_type=F32)
            mc = t if mc is None else mc + t
        parts.append(mc.astype(BF16))
    yo = jnp.dot(jnp.concatenate(parts, axis=-1), wo_ref[...], preferred_element_type=F32)
    xn = x_ref[...] + g1_ref[...] * yo
    xo_ref[...] = xn
    yn = xn * lax.rsqrt(jnp.mean(xn * xn, axis=-1, keepdims=True) + EPS) * gn_ref[...]
    h2 = yn * (1.0 + sc_ref[...]) + sh_ref[...]
    h2_ref[...] = h2.astype(BF16)

    h_hi, h_lo = _split_bf16(h2)
    w_hi, w_lo = _split_bf16(wr_ref[...])
    nt = (((1,), (1,)), ((), ()))
    l2 = lax.dot_general(jnp.concatenate([w_hi, w_lo], axis=0), h_hi, nt, preferred_element_type=F32)
    lt = (l2[:N_EXPERTS] + l2[N_EXPERTS:]) + lax.dot_general(w_hi, h_lo, nt, preferred_element_type=F32)
    s = jax.nn.sigmoid(lt)
    sb = s + br_ref[...]
    r = [sb[e:e + 1, :] for e in range(N_EXPERTS)]
    sr = [s[e:e + 1, :] for e in range(N_EXPERTS)]
    gscore = []
    for g in range(N_GROUPS):
        a, b, c, d = r[EPG * g:EPG * (g + 1)]
        hi1, lo1, hi2, lo2 = jnp.maximum(a, b), jnp.minimum(a, b), jnp.maximum(c, d), jnp.minimum(c, d)
        gscore.append(jnp.maximum(hi1, hi2) + jnp.maximum(jnp.minimum(hi1, hi2), jnp.maximum(lo1, lo2)))
    gsel, _ = _first_max(gscore)
    v = [_pick([r[EPG * g + j] for g in range(N_GROUPS)], gsel) for j in range(EPG)]
    sv = [_pick([sr[EPG * g + j] for g in range(N_GROUPS)], gsel) for j in range(EPG)]
    i1, _ = _first_max(v)
    i2, _ = _first_max([jnp.where(i1 == j, -jnp.inf, v[j]) for j in range(EPG)])
    w1 = _pick(sv, i1)
    w2 = _pick(sv, i2)
    den = w1 + w2
    e1 = gsel * EPG + i1
    e2 = gsel * EPG + i2
    w_ref[0:1, :] = w1 / den
    w_ref[1:2, :] = w2 / den

    eio = lax.broadcasted_iota(I32, (N_EXPERTS, tm), 0)
    oh1 = eio == e1
    oh2 = eio == e2
    cnt = oh1.astype(F32) + oh2.astype(F32)
    ti = lax.broadcasted_iota(I32, (tm, tm), 0)
    tj = lax.broadcasted_iota(I32, (tm, tm), 1)
    before = jnp.where(ti < tj, 1.0, 0.0).astype(BF16)
    pref = jnp.dot(cnt.astype(BF16), before, preferred_element_type=F32)
    plen = jnp.floor((jnp.sum(cnt, axis=1, keepdims=True) + (RUN - 1.0)) * (1.0 / RUN)) * RUN
    eio1 = lax.broadcasted_iota(I32, (N_EXPERTS, 1), 0)
    loff = jnp.zeros((N_EXPERTS, 1), F32)
    run_start = jnp.zeros((1, 1), F32)
    for ex in range(N_EXPERTS):
        loff = jnp.where(eio1 == ex, run_start, loff)
        run_start = run_start + plen[ex:ex + 1, :]
    start = loff + pref
    lp_ref[0:1, :] = jnp.sum(jnp.where(oh1, start, 0.0), axis=0, keepdims=True).astype(I32)
    lp_ref[1:2, :] = jnp.sum(jnp.where(oh2, start, 0.0), axis=0, keepdims=True).astype(I32)
    goff = base[...]
    lane = lax.broadcasted_iota(I32, (1, LANES), 1)
    tab_ref[...] = jnp.where(lane == 0, plen, jnp.where(lane == 1, goff, 0.0)).astype(I32)
    base[...] = goff + plen
    cnt_ref[...] = jnp.broadcast_to(goff + plen, (N_EXPERTS, LANES)).astype(I32)


def _merge(h, y0, y1, y2, rof, rob, rg, x, g1, gn, sh, sc, wts):
    n = x.shape[0]
    tm = _row_tile(n, MOE_TILE)
    row = lambda c: pl.BlockSpec((tm, c), lambda i: (i, 0))
    full = lambda a: pl.BlockSpec(a.shape, lambda i: (0,) * a.ndim)
    tok = pl.BlockSpec((2, tm), lambda i: (0, i))
    return pl.pallas_call(
        _merge_kernel,
        grid=(n // tm,),
        in_specs=[row(D), row(BW), row(BW), row(HEADS * LANES),
                  row(HEADS * DV), row(HEADS * DV), row(BW), row(D),
                  full(g1), full(gn), full(sh), full(sc)] + [full(a) for a in wts],
        out_specs=[row(D), row(D), tok, tok,
                   pl.BlockSpec((N_EXPERTS, LANES), lambda i: (i, 0)),
                   pl.BlockSpec((N_EXPERTS, LANES), lambda i: (0, 0))],
        out_shape=[jax.ShapeDtypeStruct((n, D), F32), jax.ShapeDtypeStruct((n, D), BF16),
                   jax.ShapeDtypeStruct((2, n), I32), jax.ShapeDtypeStruct((2, n), F32),
                   jax.ShapeDtypeStruct((n // tm * N_EXPERTS, LANES), I32),
                   jax.ShapeDtypeStruct((N_EXPERTS, LANES), I32)],
        scratch_shapes=[pltpu.VMEM((N_EXPERTS, 1), F32)],
        compiler_params=_cparams("arbitrary"),
    )(h, y0, y1, y2, rof, rob, rg, x, g1, gn, sh, sc, *wts)


def _for_run_pieces(tab_ref, ps_ref, fn):
    loff = 0
    for ex in range(N_EXPERTS):
        plen = tab_ref[ex, 0]
        gbase = ps_ref[ex] + tab_ref[ex, 1]

        def piece(c, carry, loff=loff, gbase=gbase):
            fn(pl.multiple_of(loff + c * RUN, RUN), pl.multiple_of(gbase + c * RUN, RUN))
            return carry

        lax.fori_loop(0, plen // RUN, piece, 0)
        loff = loff + plen


def _dispatch_kernel(ps_ref, tab_ref, tail_ref, lp_ref, x_ref, xs_ref, buf, sem):
    rmax, tm = buf.shape[0], x_ref.shape[0]
    lp = lp_ref[...]
    r = lax.broadcasted_iota(I32, (rmax, tm), 0)
    sel = jnp.where((r == lp[0:1, :]) | (r == lp[1:2, :]), 1.0, 0.0).astype(BF16)
    buf[...] = jnp.dot(sel, x_ref[...], preferred_element_type=F32).astype(BF16)

    def copy(s, t):
        return pltpu.make_async_copy(buf.at[pl.ds(s, RUN)], xs_ref.at[pl.ds(t, RUN)], sem)

    _for_run_pieces(tab_ref, ps_ref, lambda s, t: copy(s, t).start())
    _for_run_pieces(tab_ref, ps_ref, lambda s, t: copy(s, t).wait())

    @pl.when(pl.program_id(0) == pl.num_programs(0) - 1)
    def _():
        buf[0:MOE_BLOCK, :] = jnp.zeros((MOE_BLOCK, D), BF16)

        def tails(fn):
            for ex in range(N_EXPERTS):
                first = tail_ref[0, ex]

                def piece(c, carry, first=first):
                    fn(pl.multiple_of(first + c * RUN, RUN))
                    return carry

                lax.fori_loop(0, tail_ref[1, ex] // RUN, piece, 0)

        def block_copy(b):
            dst = xs_ref.at[pl.ds(pl.multiple_of(b * MOE_BLOCK, MOE_BLOCK), MOE_BLOCK)]
            return pltpu.make_async_copy(buf.at[pl.ds(0, MOE_BLOCK)], dst, sem)

        def unused(fn):
            def blk(b, carry):
                fn(b)
                return carry
            lax.fori_loop(tail_ref[2, 0], xs_ref.shape[0] // MOE_BLOCK, blk, 0)

        tails(lambda t: copy(0, t).start())
        unused(lambda b: block_copy(b).start())
        tails(lambda t: copy(0, t).wait())
        unused(lambda b: block_copy(b).wait())


def _tile_tables(tm):
    return [pl.BlockSpec(memory_space=pltpu.SMEM),
            pl.BlockSpec((N_EXPERTS, LANES), lambda i: (i, 0), memory_space=pltpu.SMEM)]


def _dispatch(h2, lp, tab, pstart, tails, cap, rmax):
    n = h2.shape[0]
    tm = _row_tile(n, MOE_TILE)
    return pl.pallas_call(
        _dispatch_kernel,
        grid=(n // tm,),
        in_specs=_tile_tables(tm) + [pl.BlockSpec(memory_space=pltpu.SMEM),
                                     pl.BlockSpec((2, tm), lambda i: (0, i)),
                                     pl.BlockSpec((tm, D), lambda i: (i, 0))],
        out_specs=pl.BlockSpec(memory_space=pl.ANY),
        out_shape=jax.ShapeDtypeStruct((cap, D), BF16),
        scratch_shapes=[pltpu.VMEM((rmax, D), BF16), pltpu.SemaphoreType.DMA(())],
        compiler_params=_cparams("arbitrary"),
    )(pstart, tab, tails, lp, h2)


def _expert_kernel(be_ref, nu_ref, x_ref, w1_ref, w3_ref, w2_ref, y_ref, w1b, w3b, w2b):
    b = pl.program_id(0)

    @pl.when((b == 0) | (be_ref[b] != be_ref[jnp.maximum(b - 1, 0)]))
    def _():
        w1b[...] = w1_ref[0, 0].astype(BF16)
        w3b[...] = w3_ref[0, 0].astype(BF16)
        w2b[...] = w2_ref[0, 0].astype(BF16)

    @pl.when(b < nu_ref[0])
    def _():
        x = x_ref[...]
        a = jnp.dot(x, w1b[...], preferred_element_type=F32)
        c = jnp.dot(x, w3b[...], preferred_element_type=F32)
        u = (a * jax.nn.sigmoid(a) * c).astype(BF16)
        y_ref[...] = jnp.dot(u, w2b[...], preferred_element_type=F32).astype(BF16)

    @pl.when(b >= nu_ref[0])
    def _():
        y_ref[...] = jnp.zeros_like(y_ref)


def _experts(xs, blk_e, nused, l, w1, w3, w2):
    cap = xs.shape[0]
    nb = cap // MOE_BLOCK
    xmap = lambda b, be, nu: (jnp.minimum(b, jnp.maximum(nu[0] - 1, 0)), 0)
    wmap = lambda b, be, nu: (l, be[b], 0, 0)
    return pl.pallas_call(
        _expert_kernel,
        grid_spec=pltpu.PrefetchScalarGridSpec(
            num_scalar_prefetch=2,
            grid=(nb,),
            in_specs=[pl.BlockSpec((MOE_BLOCK, D), xmap),
                      pl.BlockSpec((1, 1, D, D_FF), wmap),
                      pl.BlockSpec((1, 1, D, D_FF), wmap),
                      pl.BlockSpec((1, 1, D_FF, D), wmap)],
            out_specs=pl.BlockSpec((MOE_BLOCK, D), lambda b, be, nu: (b, 0)),
            scratch_shapes=[pltpu.VMEM((D, D_FF), BF16), pltpu.VMEM((D, D_FF), BF16),
                            pltpu.VMEM((D_FF, D), BF16)]),
        out_shape=jax.ShapeDtypeStruct((cap, D), BF16),
        compiler_params=_cparams("arbitrary"),
    )(blk_e, nused, xs, w1, w3, w2)


def _combine_kernel(ps_ref, tab_ref, lp_ref, w_ref, x_ref, g2_ref, gf_ref, ys_ref, o_ref, buf, sem, *,
                    final_norm):
    rmax, tm = buf.shape[0], x_ref.shape[0]

    @pl.when(pl.program_id(0) == 0)
    def _():
        buf[...] = jnp.zeros_like(buf)

    def copy(s, t):
        return pltpu.make_async_copy(ys_ref.at[pl.ds(t, RUN)], buf.at[pl.ds(s, RUN)], sem)

    _for_run_pieces(tab_ref, ps_ref, lambda s, t: copy(s, t).start())
    lp = lp_ref[...]
    w = w_ref[...]
    r = lax.broadcasted_iota(I32, (tm, rmax), 1)
    mix = (jnp.where(r == lp[:, 0:1], w[:, 0:1], 0.0)
           + jnp.where(r == lp[:, 1:2], w[:, 1:2], 0.0)).astype(BF16)
    _for_run_pieces(tab_ref, ps_ref, lambda s, t: copy(s, t).wait())
    ml = jnp.dot(mix, buf[...], preferred_element_type=F32)
    xo = x_ref[...] + g2_ref[...] * ml
    if final_norm:
        xo = xo * lax.rsqrt(jnp.mean(xo * xo, axis=-1, keepdims=True) + EPS) * gf_ref[...]
    o_ref[...] = xo


def _combine(x, g2, lpt, wtok, tab, pstart, ys, gf, final_norm, rmax):
    n = x.shape[0]
    tm = _row_tile(n, MOE_TILE)
    vec = pl.BlockSpec((1, D), lambda i: (0, 0))
    tok = pl.BlockSpec((tm, 2), lambda i: (i, 0))
    return pl.pallas_call(
        functools.partial(_combine_kernel, final_norm=final_norm),
        grid=(n // tm,),
        in_specs=_tile_tables(tm) + [tok, tok, pl.BlockSpec((tm, D), lambda i: (i, 0)), vec, vec,
                                     pl.BlockSpec(memory_space=pl.ANY)],
        out_specs=pl.BlockSpec((tm, D), lambda i: (i, 0)),
        out_shape=jax.ShapeDtypeStruct((n, D), F32),
        scratch_shapes=[pltpu.VMEM((rmax, D), BF16), pltpu.SemaphoreType.DMA(())],
        compiler_params=_cparams("arbitrary"),
    )(pstart, tab, lpt, wtok, x, g2, gf, ys)


def _moe(x, h2, lp, wsel, tab, filled, g2, l, w1, w3, w2, gf, final_norm):
    n = x.shape[0]
    tm = _row_tile(n, MOE_TILE)
    rmax = 2 * tm + N_EXPERTS * RUN
    worst = n // tm * (2 * tm + N_EXPERTS * (RUN - 1)) + N_EXPERTS * (MOE_BLOCK - RUN)
    cap = (worst + MOE_BLOCK - 1) // MOE_BLOCK * MOE_BLOCK
    nb = cap // MOE_BLOCK
    region = (filled + MOE_BLOCK - 1) // MOE_BLOCK * MOE_BLOCK
    pend = jnp.cumsum(region)
    pstart = (pend - region).astype(I32)
    blk_start = jnp.arange(nb, dtype=I32) * MOE_BLOCK
    blk_e = jnp.minimum(jnp.sum(pend[None, :] <= blk_start[:, None], axis=-1), N_EXPERTS - 1).astype(I32)
    nused = (pend[-1:] // MOE_BLOCK).astype(I32)
    tails = jnp.stack([pstart + filled, region - filled, jnp.broadcast_to(nused, (N_EXPERTS,))]).astype(I32)
    xs = _dispatch(h2, lp, tab, pstart, tails, cap, rmax)
    ys = _experts(xs, blk_e, nused, l, w1, w3, w2)
    return _combine(x, g2, lp.T, wsel.T, tab, pstart, ys, gf, final_norm, rmax)


def _rope_tables(n):
    rows = n // GRID_W
    row = jnp.repeat(jnp.arange(rows, dtype=F32), GRID_W)
    col = jnp.tile(jnp.arange(GRID_W, dtype=F32), rows)
    nf = DK // 4
    inv = ROPE_BASE ** (-jnp.arange(nf, dtype=F32) / nf)
    ang = jnp.concatenate([row[:, None] * inv, col[:, None] * inv], axis=-1)
    cos, sin = jnp.cos(ang), jnp.sin(ang)
    reps = LANES // DK
    return (jnp.tile(jnp.concatenate([cos, cos], -1), (1, reps)),
            jnp.tile(jnp.concatenate([-sin, sin], -1), (1, reps)))


def _layer_weights(l, w_in, w_gate, b_gate, w_branch, w_o, w_router, b_router):
    w = w_in[l].astype(BF16)
    wb = w_branch[l]
    wb2 = wb[2].reshape(HEADS, DV, D)
    wb2 = jnp.concatenate([wb2, jnp.zeros((HEADS, LANES - DV, D), F32)], axis=1).reshape(HEADS * LANES, D)
    merge_w = (w_gate[l].astype(BF16), b_gate[l].reshape(1, 4 * D), wb[0].astype(BF16), wb[1].astype(BF16),
               wb2.astype(BF16), wb[3].astype(BF16), w_o[l].astype(BF16),
               w_router.T, b_router.reshape(N_EXPERTS, 1))
    return w, merge_w


def kernel(x, c, ctx, c_ctx, w_mod, b_mod, g_norm1, g_norm2, w_in, conv_a_w, conv_a_b, conv_a_g, conv_a_beta, conv_b_w, lam_q1, lam_k1, lam_q2, lam_k2, diff_g, ret_ld_f, ret_ld_b, w_gate, b_gate, w_branch, w_o, w_router, b_router, w1_e, w3_e, w2_e, g_final):
    assert x.shape[0] == 1 and ctx.shape[0] == 1
    xl, xc = x[0], ctx[0]
    n_lat, n_ctx = xl.shape[0], xc.shape[0]
    mods = _adaln(c, c_ctx, w_mod, b_mod)
    cos, sinp = _rope_tables(n_lat)
    zc = jnp.zeros((n_ctx, LANES), F32)
    vone = jnp.zeros((HEADS, LANES), F32).at[:, DV].set(1.0).reshape(1, HEADS * LANES)
    gf = g_final.reshape(1, D)
    vrow = lambda a: a.reshape(1, -1)

    for l in range(DEPTH):
        last = l == DEPTH - 1
        lam_init = 0.8 - 0.6 * math.exp(-0.3 * l)
        ml = [mods[l, 0:1, j * D:(j + 1) * D] for j in range(6)]
        mc = [mods[l, 1:2, j * D:(j + 1) * D] for j in range(6)]
        w, merge_w = _layer_weights(l, w_in, w_gate, b_gate, w_branch, w_o, w_router, b_router)
        g1 = vrow(g_norm1[l])
        g2n = vrow(g_norm2[l])
        lam = (jnp.exp(jnp.sum(lam_q1[l] * lam_k1[l])) - jnp.exp(jnp.sum(lam_q2[l] * lam_k2[l]))
               + lam_init).reshape(1, 1).astype(F32)
        gpad = jnp.concatenate([diff_g[l], jnp.zeros((LANES - DV,), F32)]).reshape(1, LANES)
        ld = jnp.stack([ret_ld_f[l], ret_ld_b[l]]).astype(F32)
        conv_w = (conv_a_w[l], vrow(conv_a_b[l]), vrow(conv_a_g[l]), vrow(conv_a_beta[l]), conv_b_w[l])

        (hc, uc, bgc, cxc, qc, kc, vc, rqc, rkc, rvc, rgc) = _inproj(
            xc, g1, mc[0], mc[1], w, zc, zc, vone, rope=False)
        (hl, ul, bgl, cxl, ql, kl, vl, rql, rkl, rvl, rgl) = _inproj(
            xl, g1, ml[0], ml[1], w, cos, sinp, vone, rope=True)

        kall = jnp.concatenate([kc, kl], axis=0)
        vall = jnp.concatenate([vc, vl], axis=0)
        out_scale = 1.0 - lam_init
        y2l = _attention(ql, kall, vall, lam, gpad, out_scale)
        s0 = jnp.zeros((2, HEADS * DK, HEADS * DV), F32)
        y0c, y1c, rofc, robc, sfin = _convs_retention(uc, cxc, bgc, conv_w, rqc, rkc, rvc, ld, s0)
        y0l, y1l, rofl, robl, _ = _convs_retention(ul, cxl, bgl, conv_w, rql, rkl, rvl, ld, sfin)
        xl, h2l, lpl, wl, tabl, cl = _merge(hl, y0l, y1l, y2l, rofl, robl, rgl, xl, ml[2], g2n, ml[3], ml[4], merge_w)
        xl = _moe(xl, h2l, lpl, wl, tabl, cl[:, 0], ml[5], l, w1_e, w3_e, w2_e, gf, last)
        if not last:
            y2c = _attention(qc, kc, vc, lam, gpad, out_scale)
            xc, h2c, lpc, wc, tabc, cc = _merge(hc, y0c, y1c, y2c, rofc, robc, rgc, xc, mc[2], g2n, mc[3], mc[4], merge_w)
            xc = _moe(xc, h2c, lpc, wc, tabc, cc[:, 0], mc[5], l, w1_e, w3_e, w2_e, gf, False)
    return xl[None]
```
